```python
import jax, jax.numpy as jnp
from jax import lax
import numpy as np

D_MODEL = 1024
BATCH = 4
SEQ = 4096
DEPTH = 4

GRID_W = 64
CTX_LEN = 256

FOURIER_GROUPS = 4
FOURIER_GROUP_DIM = 128
FOURIER_DIM = FOURIER_GROUPS * FOURIER_GROUP_DIM

MLA_HEADS = 8
MLA_NOPE = 64
MLA_ROPE = 32
MLA_QK = MLA_NOPE + MLA_ROPE
MLA_V = 64
MLA_Q_LORA = 384
MLA_KV_LORA = 256
MLA_SCALE = MLA_QK ** -0.5
ROPE_PAIRS = MLA_ROPE // 4
ROPE_BASE = 10000.0
Q_BLOCK = 128

RWKV_HEADS = 8
RWKV_HEAD = 64
RWKV_DIM = RWKV_HEADS * RWKV_HEAD
DECAY_LORA = 64
AAA_LORA = 64
GATE_LORA = 128
RWKV_IN = 3 * RWKV_DIM + DECAY_LORA + AAA_LORA + GATE_LORA
GN_EPS = 64e-5

N_BRANCHES = 3
IN_SIZES = (FOURIER_DIM, MLA_Q_LORA, MLA_KV_LORA, MLA_ROPE, RWKV_IN, D_MODEL, D_MODEL, D_MODEL)
IN_DIM = FOURIER_DIM + MLA_Q_LORA + MLA_KV_LORA + MLA_ROPE + RWKV_IN + N_BRANCHES * D_MODEL

N_EXPERTS = 16
N_GROUPS = 4
EXPERTS_PER_GROUP = N_EXPERTS // N_GROUPS
GROUP_SCORE_TOPK = 2
TOP_K = 2
EXPERT_FF = 512
SHARED_FF = 512

NORM_EPS = 1e-6

kernel_name = "hybrid_fourier_mla_rwkv7_moe_dit"

F32 = jnp.float32


def rmsnorm(x, g):
    xf = x.astype(F32)
    return (xf * lax.rsqrt(jnp.mean(xf * xf, axis=-1, keepdims=True) + NORM_EPS)).astype(x.dtype) * g


def split_cols(p, sizes):
    return jnp.split(p, np.cumsum(sizes)[:-1].tolist(), axis=-1)


def axial_rope_tables(row, col):
    inv = ROPE_BASE ** (-jnp.arange(ROPE_PAIRS, dtype=F32) / ROPE_PAIRS)
    ang = jnp.stack([row, col], axis=-1).astype(F32)[..., None] * inv
    return jnp.cos(ang), jnp.sin(ang)


def rope_2d(x, cos, sin):
    xr = x.reshape(x.shape[:-1] + (2, 2, ROPE_PAIRS))
    x1, x2 = xr[..., 0, :], xr[..., 1, :]
    c, s = cos[None, :, None], sin[None, :, None]
    out = jnp.stack([x1 * c - x2 * s, x2 * c + x1 * s], axis=-2)
    return out.reshape(x.shape).astype(x.dtype)


def fourier_mix(p):
    B, T, _ = p.shape
    z = p.astype(F32).reshape(B, T, FOURIER_GROUPS, FOURIER_GROUP_DIM)
    f = jnp.fft.fft2(z, axes=(1, 3), norm="ortho").real
    return f.reshape(B, T, FOURIER_DIM).astype(p.dtype)


def mla_project(c_q, c_kv, k_r, q_norm_g, kv_norm_g, w_uq, w_ukv, cos=None, sin=None):
    B, T, _ = c_q.shape
    q = (rmsnorm(c_q, q_norm_g) @ w_uq).reshape(B, T, MLA_HEADS, MLA_QK)
    kv = (rmsnorm(c_kv, kv_norm_g) @ w_ukv).reshape(B, T, MLA_HEADS, MLA_NOPE + MLA_V)
    q_nope, q_rope = q[..., :MLA_NOPE], q[..., MLA_NOPE:]
    k_nope, v = kv[..., :MLA_NOPE], kv[..., MLA_NOPE:]
    k_rope = k_r[:, :, None, :]
    if cos is not None:
        q_rope = rope_2d(q_rope, cos, sin)
        k_rope = rope_2d(k_rope, cos, sin)
    q = jnp.concatenate([q_nope, q_rope], axis=-1)
    k = jnp.concatenate([k_nope, jnp.broadcast_to(k_rope, (B, T, MLA_HEADS, MLA_ROPE))], axis=-1)
    return q, k, v


def attend(q, k, v):
    s = jnp.einsum('bqhd,bkhd->bhqk', q, k).astype(F32) * MLA_SCALE
    p = jax.nn.softmax(s, axis=-1).astype(v.dtype)
    return jnp.einsum('bhqk,bkhd->bqhd', p, v)


def latent_attention(q, k_lat, v_lat, k_ctx, v_ctx):
    B, T, H, _ = q.shape
    k = jnp.concatenate([k_lat, k_ctx], axis=1)
    v = jnp.concatenate([v_lat, v_ctx], axis=1)
    qb = q.reshape(B, T // Q_BLOCK, Q_BLOCK, H, MLA_QK).transpose(1, 0, 2, 3, 4)
    o = lax.map(lambda qi: attend(qi, k, v), qb)
    return o.transpose(1, 0, 2, 3, 4).reshape(B, T, H * MLA_V)


def centred_shift(p):
    pad = jnp.pad(p, ((0, 0), (1, 1), (0, 0)))
    return 0.5 * (pad[:, :-2] + pad[:, 2:])


def wkv_scan(r, w, k, v, a, b, S0, reverse):
    xs = tuple(jnp.moveaxis(t, 1, 0) for t in (r, w, k, v, a, b))

    def step(S, inp):
        rt, wt, kt, vt, at, bt = inp
        sa = jnp.einsum('bhvk,bhk->bhv', S, at)
        S = S * wt[:, :, None, :] + sa[..., None] * bt[:, :, None, :] + vt[..., None] * kt[:, :, None, :]
        return S, jnp.einsum('bhvk,bhk->bhv', S, rt)

    S_fin, ys = lax.scan(step, S0, xs, reverse=reverse)
    return S_fin, jnp.moveaxis(ys, 0, 1)


def rwkv_mix(p, S0, mu, w0, w_up, a0, a_up, g_up, k_k, k_a, r_k, gn_g, gn_b):
    B, T, _ = p.shape
    H, N = RWKV_HEADS, RWKV_HEAD
    xs = (p + (centred_shift(p) - p) * mu).astype(F32)
    r, k, v, xw, xa, xg = split_cols(xs, (RWKV_DIM, RWKV_DIM, RWKV_DIM, DECAY_LORA, AAA_LORA, GATE_LORA))
    g = jax.nn.sigmoid(xg) @ g_up
    rh, vh = r.reshape(B, T, H, N), v.reshape(B, T, H, N)
    kk = (k * k_k).reshape(B, T, H, N)
    kk = kk * lax.rsqrt(jnp.sum(kk * kk, axis=-1, keepdims=True) + 1e-12)
    ys, bonuses, finals = [], [], []
    for d in range(2):
        w = -jax.nn.softplus(-(w0[d] + jnp.tanh(xw) @ w_up[d])) - 0.5
        decay = jnp.exp(-jnp.exp(w)).reshape(B, T, H, N)
        a = jax.nn.sigmoid(a0[d] + xa @ a_up[d]).reshape(B, T, H, N)
        kd = k.reshape(B, T, H, N) * (1.0 + (a - 1.0) * k_a.reshape(H, N))
        s_fin, yd = wkv_scan(rh, decay, kd, vh, -kk, kk * a, S0[d], reverse=(d == 1))
        ys.append(yd)
        bonuses.append(jnp.sum(rh * kd * r_k, axis=-1, keepdims=True) * vh)
        finals.append(s_fin)
    y = ys[0] + ys[1]
    mean = jnp.mean(y, axis=-1, keepdims=True)
    var = jnp.mean(jnp.square(y - mean), axis=-1, keepdims=True)
    y = ((y - mean) * lax.rsqrt(var + GN_EPS)).reshape(B, T, RWKV_DIM) * gn_g + gn_b
    y = (y + (bonuses[0] + bonuses[1]).reshape(B, T, RWKV_DIM)) * g
    return y.astype(p.dtype), jnp.stack(finals)


def merge_branches(fa, attn, rw, ga, gb, gc, w_a, w_b, w_c, w_out):
    merged = (jax.nn.sigmoid(ga) * (fa @ w_a) + jax.nn.sigmoid(gb) * (attn @ w_b)
              + jax.nn.sigmoid(gc) * (rw @ w_c))
    return merged @ w_out


def moe(h, router_w, router_b, w_gate, w_up, w_down, sw_gate, sw_up, sw_down):
    T = h.shape[0]
    scores = jax.nn.sigmoid((h @ router_w).astype(F32))
    sel = (scores + router_b.astype(F32)).reshape(T, N_GROUPS, EXPERTS_PER_GROUP)
    group_score = lax.top_k(sel, GROUP_SCORE_TOPK)[0].sum(-1)
    best = jnp.argmax(group_score, axis=-1)
    in_group = jax.nn.one_hot(best, N_GROUPS, dtype=F32)[..., None] > 0
    masked = jnp.where(in_group, sel, -jnp.inf).reshape(T, N_EXPERTS)
    _, idx = lax.top_k(masked, TOP_K)
    wts = jnp.take_along_axis(scores, idx, axis=-1)
    wts = wts / jnp.sum(wts, axis=-1, keepdims=True)
    combine = jnp.einsum('tk,tke->te', wts, jax.nn.one_hot(idx, N_EXPERTS, dtype=F32)).astype(h.dtype)
    y = (jax.nn.silu(h @ sw_gate) * (h @ sw_up)) @ sw_down
    for gi in range(N_GROUPS):
        e = slice(gi * EXPERTS_PER_GROUP, (gi + 1) * EXPERTS_PER_GROUP)
        hg = jnp.einsum('td,edf->tef', h, w_gate[e])
        hu = jnp.einsum('td,edf->tef', h, w_up[e])
        y = y + jnp.einsum('tef,efd->td', jax.nn.silu(hg) * hu * combine[:, e, None], w_down[e])
    return y


def setup_inputs(seed: int = 0) -> dict:
    key = jax.random.key(seed)
    ks = iter(jax.random.split(key, 40))

    def nrm(shape, scale):
        return jax.random.normal(next(ks), shape, F32) * scale

    def gain(shape):
        return 1.0 + nrm(shape, 0.01)

    D, L = D_MODEL, DEPTH
    return {
        "x": nrm((BATCH, SEQ, D), 1.0),
        "c": nrm((BATCH, D), 1.0),
        "ctx": nrm((BATCH, CTX_LEN, D), 1.0),
        "c_ctx": nrm((D,), 1.0),
        "w_ada": nrm((L, D, 6 * D), 0.5 * D ** -0.5),
        "b_ada": nrm((L, 6 * D), 0.01),
        "norm1_g": gain((L, D)),
        "norm2_g": gain((L, D)),
        "w_in": nrm((L, D, IN_DIM), D ** -0.5),
        "q_norm_g": gain((L, MLA_Q_LORA)),
        "kv_norm_g": gain((L, MLA_KV_LORA)),
        "w_uq": nrm((L, MLA_Q_LORA, MLA_HEADS * MLA_QK), MLA_Q_LORA ** -0.5),
        "w_ukv": nrm((L, MLA_KV_LORA, MLA_HEADS * (MLA_NOPE + MLA_V)), MLA_KV_LORA ** -0.5),
        "rwkv_mu": jax.random.uniform(next(ks), (L, RWKV_IN), F32),
        "rwkv_w0": nrm((L, 2, RWKV_DIM), 1.0),
        "rwkv_w_up": nrm((L, 2, DECAY_LORA, RWKV_DIM), 0.1 * DECAY_LORA ** -0.5),
        "rwkv_a0": nrm((L, 2, RWKV_DIM), 0.5),
        "rwkv_a_up": nrm((L, 2, AAA_LORA, RWKV_DIM), AAA_LORA ** -0.5),
        "rwkv_g_up": nrm((L, GATE_LORA, RWKV_DIM), GATE_LORA ** -0.5),
        "rwkv_k_k": 0.85 + nrm((L, RWKV_DIM), 0.05),
        "rwkv_k_a": 1.0 + nrm((L, RWKV_DIM), 0.05),
        "rwkv_r_k": nrm((L, RWKV_HEADS, RWKV_HEAD), 0.1),
        "rwkv_gn_g": gain((L, RWKV_DIM)),
        "rwkv_gn_b": nrm((L, RWKV_DIM), 0.01),
        "w_proj_a": nrm((L, FOURIER_DIM, D), FOURIER_DIM ** -0.5),
        "w_proj_b": nrm((L, MLA_HEADS * MLA_V, D), (MLA_HEADS * MLA_V) ** -0.5),
        "w_proj_c": nrm((L, RWKV_DIM, D), RWKV_DIM ** -0.5),
        "w_out": nrm((L, D, D), D ** -0.5),
        "router_w": nrm((D, N_EXPERTS), D ** -0.5),
        "router_b": nrm((N_EXPERTS,), 0.01),
        "exp_w_gate": nrm((L, N_EXPERTS, D, EXPERT_FF), D ** -0.5),
        "exp_w_up": nrm((L, N_EXPERTS, D, EXPERT_FF), D ** -0.5),
        "exp_w_down": nrm((L, N_EXPERTS, EXPERT_FF, D), EXPERT_FF ** -0.5),
        "sh_w_gate": nrm((L, D, SHARED_FF), D ** -0.5),
        "sh_w_up": nrm((L, D, SHARED_FF), D ** -0.5),
        "sh_w_down": nrm((L, SHARED_FF, D), SHARED_FF ** -0.5),
        "final_norm_g": gain((D,)),
    }


def reference(x, c, ctx, c_ctx, w_ada, b_ada, norm1_g, norm2_g, w_in, q_norm_g, kv_norm_g, w_uq, w_ukv,
              rwkv_mu, rwkv_w0, rwkv_w_up, rwkv_a0, rwkv_a_up, rwkv_g_up, rwkv_k_k, rwkv_k_a, rwkv_r_k,
              rwkv_gn_g, rwkv_gn_b, w_proj_a, w_proj_b, w_proj_c, w_out, router_w, router_b,
              exp_w_gate, exp_w_up, exp_w_down, sh_w_gate, sh_w_up, sh_w_down, final_norm_g):
    B, T, D = x.shape
    rows = T // GRID_W
    row = jnp.repeat(jnp.arange(rows), GRID_W)
    col = jnp.tile(jnp.arange(GRID_W), rows)
    cos, sin = axial_rope_tables(row, col)
    zero_state = jnp.zeros((2, B, RWKV_HEADS, RWKV_HEAD, RWKV_HEAD), F32)

    for l in range(DEPTH):
        last = l == DEPTH - 1
        mod_lat = (jax.nn.silu(c) @ w_ada[l] + b_ada[l])[:, None, :]
        mod_ctx = (jax.nn.silu(c_ctx) @ w_ada[l] + b_ada[l])[None, None, :]
        sh1, sc1, g1, sh2, sc2, g2 = jnp.split(mod_lat, 6, axis=-1)
        csh1, csc1, cg1, csh2, csc2, cg2 = jnp.split(mod_ctx, 6, axis=-1)

        h_lat = rmsnorm(x, norm1_g[l]) * (1.0 + sc1) + sh1
        h_ctx = rmsnorm(ctx, norm1_g[l]) * (1.0 + csc1) + csh1
        pf_l, pq_l, pkv_l, pkr_l, pr_l, ga_l, gb_l, gc_l = split_cols(h_lat @ w_in[l], IN_SIZES)
        pf_c, pq_c, pkv_c, pkr_c, pr_c, ga_c, gb_c, gc_c = split_cols(h_ctx @ w_in[l], IN_SIZES)

        q_l, k_l, v_l = mla_project(pq_l, pkv_l, pkr_l, q_norm_g[l], kv_norm_g[l], w_uq[l], w_ukv[l], cos, sin)
        q_c, k_c, v_c = mla_project(pq_c, pkv_c, pkr_c, q_norm_g[l], kv_norm_g[l], w_uq[l], w_ukv[l])
        attn_lat = latent_attention(q_l, k_l, v_l, k_c, v_c)

        rw = (rwkv_mu[l], rwkv_w0[l], rwkv_w_up[l], rwkv_a0[l], rwkv_a_up[l], rwkv_g_up[l],
              rwkv_k_k[l], rwkv_k_a[l], rwkv_r_k[l], rwkv_gn_g[l], rwkv_gn_b[l])
        y_ctx, s_ctx = rwkv_mix(pr_c, zero_state, *rw)
        y_lat, _ = rwkv_mix(pr_l, s_ctx, *rw)

        out_lat = merge_branches(fourier_mix(pf_l), attn_lat, y_lat, ga_l, gb_l, gc_l,
                                 w_proj_a[l], w_proj_b[l], w_proj_c[l], w_out[l])
        x = x + g1 * out_lat

        moe_w = (router_w, router_b, exp_w_gate[l], exp_w_up[l], exp_w_down[l],
                 sh_w_gate[l], sh_w_up[l], sh_w_down[l])
        m_lat = rmsnorm(x, norm2_g[l]) * (1.0 + sc2) + sh2
        x = x + g2 * moe(m_lat.reshape(B * T, D), *moe_w).reshape(B, T, D)

        if not last:
            attn_ctx = attend(q_c, k_c, v_c).reshape(B, CTX_LEN, MLA_HEADS * MLA_V)
            out_ctx = merge_branches(fourier_mix(pf_c), attn_ctx, y_ctx, ga_c, gb_c, gc_c,
                                     w_proj_a[l], w_proj_b[l], w_proj_c[l], w_out[l])
            ctx = ctx + cg1 * out_ctx
            m_ctx = rmsnorm(ctx, norm2_g[l]) * (1.0 + csc2) + csh2
            ctx = ctx + cg2 * moe(m_ctx.reshape(B * CTX_LEN, D), *moe_w).reshape(B, CTX_LEN, D)

    return rmsnorm(x, final_norm_g)
```

```python
import functools
import math

import numpy as np
import jax
import jax.numpy as jnp
from jax import lax
from jax.experimental import pallas as pl
from jax.experimental.pallas import tpu as pltpu

F32 = jnp.float32
BF16 = jnp.bfloat16

GRID_W = 64
FOURIER_GROUP_DIM = 128
MLA_HEADS = 8
MLA_NOPE = 64
MLA_ROPE = 32
MLA_QK = MLA_NOPE + MLA_ROPE
MLA_V = 64
MLA_SCALE = MLA_QK ** -0.5
ROPE_PAIRS = MLA_ROPE // 4
ROPE_BASE = 10000.0
RWKV_HEADS = 8
RWKV_HEAD = 64
RWKV_DIM = RWKV_HEADS * RWKV_HEAD
DECAY_LORA = 64
AAA_LORA = 64
GATE_LORA = 128
GN_EPS = 64e-5
N_GROUPS = 4
NORM_EPS = 1e-6

LANES = 128
VMEM_LIMIT_BYTES = 48 * 1024 * 1024
RWKV_CHUNK = 64


def _cparams(*sem):
    return pltpu.CompilerParams(dimension_semantics=sem, vmem_limit_bytes=VMEM_LIMIT_BYTES)


def _sigmoid(z):
    return 1.0 / (1.0 + jnp.exp(-z))


def _bdot(a, b):
    return jnp.dot(a.astype(BF16), b.astype(BF16), preferred_element_type=F32)


def _dot_nt(a, b):
    return lax.dot_general(a.astype(BF16), b.astype(BF16), (((1,), (1,)), ((), ())),
                           preferred_element_type=F32)


def _dot_tn(a, b):
    return lax.dot_general(a.astype(BF16), b.astype(BF16), (((0,), (0,)), ((), ())),
                           preferred_element_type=F32)


def _split2(x):
    hi = x.astype(BF16)
    lo = (x - hi.astype(F32)).astype(BF16)
    return hi, lo


def _dot_exact_rhs(x, ones_bf16):
    hi, lo = _split2(x)
    return (jnp.dot(hi, ones_bf16, preferred_element_type=F32)
            + jnp.dot(lo, ones_bf16, preferred_element_type=F32))


def _dot_split(a, b):
    ah, al = _split2(a)
    bh, bl = _split2(b)
    return (jnp.dot(ah, bh, preferred_element_type=F32)
            + jnp.dot(ah, bl, preferred_element_type=F32)
            + jnp.dot(al, bh, preferred_element_type=F32))


def _ada_kernel(c_ref, w_ref, b_ref, o_ref):
    c = c_ref[...]
    s = c * _sigmoid(c)
    o_ref[...] = _dot_split(s, w_ref[...]) + b_ref[...]


def _ada_mod(c_rows, w_ada, b_ada):
    L, D, D6 = w_ada.shape
    R = c_rows.shape[0]
    tn = 1536 if D6 % 1536 == 0 else D6
    return pl.pallas_call(
        _ada_kernel,
        out_shape=jax.ShapeDtypeStruct((L, R, D6), F32),
        grid=(L, D6 // tn),
        in_specs=[pl.BlockSpec((R, D), lambda l, j: (0, 0)),
                  pl.BlockSpec((None, D, tn), lambda l, j: (l, 0, j)),
                  pl.BlockSpec((None, 1, tn), lambda l, j: (l, 0, j))],
        out_specs=pl.BlockSpec((None, R, tn), lambda l, j: (l, 0, j)),
        compiler_params=_cparams("arbitrary", "arbitrary"),
        name="ada_mod",
    )(c_rows, w_ada, b_ada.reshape(L, 1, D6))


def _norm_kernel(x_ref, g_ref, sc_ref, sh_ref, o_ref, *, modulate):
    x = x_ref[...].astype(F32)
    y = x * lax.rsqrt(jnp.mean(x * x, axis=-1, keepdims=True) + NORM_EPS) * g_ref[...]
    if modulate:
        y = y * (1.0 + sc_ref[...]) + sh_ref[...]
    o_ref[...] = y.astype(o_ref.dtype)


def _norm(x, g, sc, sh, out_dtype, modulate=True):
    N, D = x.shape
    Bm = sc.shape[0]
    rpm = N // Bm
    tm = min(rpm, 1024)
    assert rpm % tm == 0
    return pl.pallas_call(
        functools.partial(_norm_kernel, modulate=modulate),
        out_shape=jax.ShapeDtypeStruct((N, D), out_dtype),
        grid=(N // tm,),
        in_specs=[pl.BlockSpec((tm, D), lambda i: (i, 0)),
                  pl.BlockSpec((1, D), lambda i: (0, 0)),
                  pl.BlockSpec((None, 1, D), lambda i: (i * tm // rpm, 0, 0)),
                  pl.BlockSpec((None, 1, D), lambda i: (i * tm // rpm, 0, 0))],
        out_specs=pl.BlockSpec((tm, D), lambda i: (i, 0)),
        compiler_params=_cparams("arbitrary"),
        name="rmsnorm_mod",
    )(x, g.reshape(1, D), sc, sh)


def _mm_kernel(a_ref, b_ref, o_ref):
    o_ref[...] = jnp.dot(a_ref[...], b_ref[...], preferred_element_type=F32).astype(o_ref.dtype)


def _mm_acc_kernel(a_ref, b_ref, o_ref, acc_ref, *, nk):
    k = pl.program_id(2)

    @pl.when(k == 0)
    def _():
        acc_ref[...] = jnp.zeros_like(acc_ref)

    acc_ref[...] += jnp.dot(a_ref[...], b_ref[...], preferred_element_type=F32)

    @pl.when(k == nk - 1)
    def _():
        o_ref[...] = acc_ref[...].astype(o_ref.dtype)


def _pick(n, prefs):
    for p in prefs:
        if n % p == 0:
            return p
    return n


def _mm(a, w, out_dtype, tn=None, tk=None):
    M, K = a.shape
    Nn = w.shape[1]
    tm = _pick(M, (1024, 512, 256))
    tn = tn or _pick(Nn, (1024, 896, 512))
    tk = tk or K
    if tk == K:
        return pl.pallas_call(
            _mm_kernel,
            out_shape=jax.ShapeDtypeStruct((M, Nn), out_dtype),
            grid=(M // tm, Nn // tn),
            in_specs=[pl.BlockSpec((tm, K), lambda i, j: (i, 0)),
                      pl.BlockSpec((K, tn), lambda i, j: (0, j))],
            out_specs=pl.BlockSpec((tm, tn), lambda i, j: (i, j)),
            compiler_params=_cparams("arbitrary", "arbitrary"),
            name="matmul",
        )(a, w)
    nk = K // tk
    return pl.pallas_call(
        functools.partial(_mm_acc_kernel, nk=nk),
        out_shape=jax.ShapeDtypeStruct((M, Nn), out_dtype),
        grid=(M // tm, Nn // tn, nk),
        in_specs=[pl.BlockSpec((tm, tk), lambda i, j, k: (i, k)),
                  pl.BlockSpec((tk, tn), lambda i, j, k: (k, j))],
        out_specs=pl.BlockSpec((tm, tn), lambda i, j, k: (i, j)),
        scratch_shapes=[pltpu.VMEM((tm, tn), F32)],
        compiler_params=_cparams("arbitrary", "arbitrary", "arbitrary"),
        name="matmul_kacc",
    )(a, w)


def _qkv_kernel(p_ref, gq_ref, gkv_ref, wqa_ref, wqb_ref, wk_ref, wv_ref,
                cq_ref, sq_ref, ck_ref, sk_ref, q_ref, k_ref, v_ref,
                qn_ref, kvn_ref, kr_ref, *, ql, kvl):
    h = pl.program_id(1)

    @pl.when(h == 0)
    def _():
        p = p_ref[...].astype(F32)
        pq = p[:, :ql]
        pkv = p[:, ql:ql + kvl]
        kra = p[:, ql + kvl:ql + kvl + LANES]
        krb = p[:, ql + kvl + LANES:ql + kvl + 2 * LANES]
        qn = pq * lax.rsqrt(jnp.mean(pq * pq, axis=-1, keepdims=True) + NORM_EPS) * gq_ref[...]
        kvn = pkv * lax.rsqrt(jnp.mean(pkv * pkv, axis=-1, keepdims=True) + NORM_EPS) * gkv_ref[...]
        qn_ref[...] = qn.astype(BF16)
        kvn_ref[...] = kvn.astype(BF16)
        kr_ref[...] = kra * ck_ref[...] + krb * sk_ref[...]

    qn = qn_ref[...]
    kvn = kvn_ref[...]
    qa = jnp.dot(qn, wqa_ref[...], preferred_element_type=F32)
    qb = jnp.dot(qn, wqb_ref[...], preferred_element_type=F32)
    q_ref[...] = ((qa * cq_ref[...] + qb * sq_ref[...]) * MLA_SCALE).astype(q_ref.dtype)
    k_ref[...] = (jnp.dot(kvn, wk_ref[...], preferred_element_type=F32) + kr_ref[...]).astype(k_ref.dtype)
    v_ref[...] = jnp.dot(kvn, wv_ref[...], preferred_element_type=F32).astype(v_ref.dtype)


def _qkv_project(pqkv, gq, gkv, wqa, wqb, wk, wv, tabs, B, T):
    N, W = pqkv.shape
    H = wqa.shape[0]
    ql, kvl = wqa.shape[1], wk.shape[1]
    tm = _pick(T, (512, 256))
    nt = T // tm
    cq, sq, ck, sk = tabs
    tab_spec = pl.BlockSpec((tm, LANES), lambda i, h: (i % nt, 0))
    out_spec = pl.BlockSpec((None, None, tm, LANES), lambda i, h: (i // nt, h, i % nt, 0))
    out_sds = jax.ShapeDtypeStruct((B, H, T, LANES), BF16)
    return pl.pallas_call(
        functools.partial(_qkv_kernel, ql=ql, kvl=kvl),
        out_shape=(out_sds, out_sds, out_sds),
        grid=(N // tm, H),
        in_specs=[pl.BlockSpec((tm, W), lambda i, h: (i, 0)),
                  pl.BlockSpec((1, ql), lambda i, h: (0, 0)),
                  pl.BlockSpec((1, kvl), lambda i, h: (0, 0)),
                  pl.BlockSpec((None, ql, LANES), lambda i, h: (h, 0, 0)),
                  pl.BlockSpec((None, ql, LANES), lambda i, h: (h, 0, 0)),
                  pl.BlockSpec((None, kvl, LANES), lambda i, h: (h, 0, 0)),
                  pl.BlockSpec((None, kvl, LANES), lambda i, h: (h, 0, 0)),
                  tab_spec, tab_spec, tab_spec, tab_spec],
        out_specs=(out_spec, out_spec, out_spec),
        scratch_shapes=[pltpu.VMEM((tm, ql), BF16), pltpu.VMEM((tm, kvl), BF16),
                        pltpu.VMEM((tm, LANES), F32)],
        compiler_params=_cparams("arbitrary", "arbitrary"),
        name="mla_qkv",
    )(pqkv, gq.reshape(1, ql), gkv.reshape(1, kvl), wqa, wqb, wk, wv, cq, sq, ck, sk)


def _attn_kernel(q_ref, k_ref, v_ref, o_ref):
    s = _dot_nt(q_ref[...], k_ref[...])
    m = jnp.max(s, axis=-1, keepdims=True)
    p = jnp.exp(s - m)
    l = jnp.sum(p, axis=-1, keepdims=True)
    o = jnp.dot(p.astype(BF16), v_ref[...], preferred_element_type=F32)
    o_ref[...] = (o / l).astype(o_ref.dtype)


def _attention(q, k, v):
    B, H, T, dq = q.shape
    Tk = k.shape[2]
    tq = _pick(T, (256,))
    return pl.pallas_call(
        _attn_kernel,
        out_shape=jax.ShapeDtypeStruct((B, H, T, LANES), BF16),
        grid=(B, H, T // tq),
        in_specs=[pl.BlockSpec((None, None, tq, dq), lambda b, h, i: (b, h, i, 0)),
                  pl.BlockSpec((None, None, Tk, dq), lambda b, h, i: (b, h, 0, 0)),
                  pl.BlockSpec((None, None, Tk, LANES), lambda b, h, i: (b, h, 0, 0))],
        out_specs=pl.BlockSpec((None, None, tq, LANES), lambda b, h, i: (b, h, i, 0)),
        compiler_params=_cparams("arbitrary", "arbitrary", "arbitrary"),
        name="attention",
    )(q, k, v)


def _softplus(u):
    return jnp.maximum(u, 0.0) + jnp.log(1.0 + jnp.exp(-jnp.abs(u)))


def _rwkv_prep_kernel(p_ref, pp_ref, pn_ref, mu_ref, w0_ref, wup_ref, a0_ref, aup_ref, gup_ref,
                      kk_ref, ka_ref, rk_ref, bd_ref,
                      r_o, k_o, v_o, kn_o, lw_o, a_o, g_o, bonus_o, *, tps):
    i = pl.program_id(0)
    p = p_ref[...].astype(F32)
    tt = p.shape[0]
    C = RWKV_DIM
    halo = pp_ref.shape[0]
    prev_row = jnp.where(i % tps == 0, 0.0, pp_ref[...].astype(F32)[halo - 1:halo, :])
    next_row = jnp.where(i % tps == tps - 1, 0.0, pn_ref[...].astype(F32)[0:1, :])
    rows = lax.broadcasted_iota(jnp.int32, (tt, 1), 0)
    pprev = jnp.where(rows == 0, prev_row, pltpu.roll(p, 1, axis=0))
    pnext = jnp.where(rows == tt - 1, next_row, pltpu.roll(p, tt - 1, axis=0))
    xs = p + (0.5 * (pprev + pnext) - p) * mu_ref[...]
    r = xs[:, :C]
    k = xs[:, C:2 * C]
    v = xs[:, 2 * C:3 * C]
    xwa = xs[:, 3 * C:3 * C + DECAY_LORA + AAA_LORA]
    xg = xs[:, 3 * C + DECAY_LORA + AAA_LORA:]
    bd = bd_ref[...]
    g_o[...] = _bdot(_sigmoid(xg), gup_ref[...])
    kn = k * kk_ref[...]
    kn = kn * lax.rsqrt(_dot_exact_rhs(kn * kn, bd) + 1e-12)
    th = jnp.tanh(xwa)
    bonus = jnp.zeros_like(r)
    for d in range(2):
        w = -_softplus(-(w0_ref[d:d + 1, :] + _bdot(th, wup_ref[d]))) - 0.5
        lw_o[d] = -jnp.exp(w)
        a = _sigmoid(a0_ref[d:d + 1, :] + _bdot(xwa, aup_ref[d]))
        a_o[d] = a
        kd = k * (1.0 + (a - 1.0) * ka_ref[...])
        bonus = bonus + _dot_exact_rhs(r * kd * rk_ref[...], bd) * v
    r_o[...] = r
    k_o[...] = k
    v_o[...] = v
    kn_o[...] = kn
    bonus_o[...] = bonus


def _rwkv_prep(pr, prm, B, T):
    N, W = pr.shape
    C = RWKV_DIM
    tt = _pick(T, (256,))
    tps = T // tt
    halo = 16
    nh = N // halo
    full = lambda shape: pl.BlockSpec(shape, lambda i: (0,) * len(shape))
    o_spec = pl.BlockSpec((tt, C), lambda i: (i, 0))
    o2_spec = pl.BlockSpec((2, tt, C), lambda i: (0, i, 0))
    sds = jax.ShapeDtypeStruct((N, C), F32)
    sds2 = jax.ShapeDtypeStruct((2, N, C), F32)
    return pl.pallas_call(
        functools.partial(_rwkv_prep_kernel, tps=tps),
        out_shape=(sds, sds, sds, sds, sds2, sds2, sds, sds),
        grid=(N // tt,),
        in_specs=[pl.BlockSpec((tt, W), lambda i: (i, 0)),
                  pl.BlockSpec((halo, W), lambda i: (jnp.maximum(i * (tt // halo) - 1, 0), 0)),
                  pl.BlockSpec((halo, W), lambda i: (jnp.minimum((i + 1) * (tt // halo), nh - 1), 0)),
                  full((1, W)), full((2, C)), full((2, LANES, C)), full((2, C)), full((2, LANES, C)),
                  full((GATE_LORA, C)), full((1, C)), full((1, C)), full((1, C)), full((C, C))],
        out_specs=(o_spec, o_spec, o_spec, o_spec, o2_spec, o2_spec, o_spec, o_spec),
        compiler_params=_cparams("arbitrary"),
        name="rwkv_prep",
    )(pr, pr, pr, prm["mu"], prm["w0"], prm["wup"], prm["a0"], prm["aup"], prm["gup"],
      prm["k_k"], prm["k_a"], prm["r_k"], prm["bd"])


def _rwkv_chunk_kernel(r_ref, k_ref, v_ref, kn_ref, lw_ref, a_ref, ka_ref, s0_ref,
                       y_ref, sf_ref, S_ref):
    d = pl.program_id(1)
    c = pl.program_id(2)

    @pl.when(c == 0)
    def _():
        S_ref[...] = s0_ref[...]

    lw = lw_ref[...]
    a = a_ref[...]
    r = r_ref[...]
    k = k_ref[...]
    v = v_ref[...]
    kn = kn_ref[...]
    C = lw.shape[0]
    N = RWKV_HEAD
    ti = lax.broadcasted_iota(jnp.int32, (C, C), 0)
    ii = lax.broadcasted_iota(jnp.int32, (C, C), 1)
    order = (ti - ii) * (1 - 2 * d)
    incl = order >= 0
    strict = order > 0
    eye = (order == 0).astype(F32)
    m_incl = incl.astype(F32).astype(BF16)

    h1 = lw.astype(BF16)
    r1 = lw - h1.astype(F32)
    h2 = r1.astype(BF16)
    h3 = (r1 - h2.astype(F32)).astype(BF16)
    cum = (jnp.dot(m_incl, h1, preferred_element_type=F32)
           + jnp.dot(m_incl, h2, preferred_element_type=F32)
           + jnp.dot(m_incl, h3, preferred_element_type=F32))
    tot = jnp.sum(lw, axis=0, keepdims=True)
    e_in = jnp.exp(cum)
    e_ex = jnp.exp(cum - lw)
    e_neg = jnp.exp(-cum)
    e_rem = jnp.exp(tot - cum)
    wc = jnp.exp(tot)
    b = kn * a
    kd = k * (1.0 + (a - 1.0) * ka_ref[...])
    Rt = r * e_in
    At = -kn * e_ex
    Bt = b * e_neg
    Kt = kd * e_neg
    Bh = b * e_rem
    Kh = kd * e_rem

    for h in range(RWKV_HEADS):
        sl = slice(h * N, (h + 1) * N)
        A_h, R_h, B_h, K_h, V_h = At[:, sl], Rt[:, sl], Bt[:, sl], Kt[:, sl], v[:, sl]
        Lab = jnp.where(strict, _dot_nt(A_h, B_h), 0.0)
        Lak = jnp.where(strict, _dot_nt(A_h, K_h), 0.0)
        Mrb = jnp.where(incl, _dot_nt(R_h, B_h), 0.0)
        Mrk = jnp.where(incl, _dot_nt(R_h, K_h), 0.0)
        X = eye + Lab
        P = Lab
        for _ in range(int(math.log2(C)) - 1):
            P = _bdot(P, P)
            X = X + _bdot(X, P)
        Ah = _bdot(X, A_h)
        Uh = _bdot(X, _bdot(Lak, V_h))
        Rh = R_h + _bdot(Mrb, Ah)
        Yh = _bdot(Mrb, Uh) + _bdot(Mrk, V_h)
        G = _dot_tn(Ah, Bh[:, sl])
        Hm = _dot_tn(Uh, Bh[:, sl]) + _dot_tn(V_h, Kh[:, sl])
        S = S_ref[h]
        y_ref[:, sl] = _dot_nt(Rh, S) + Yh
        S_new = S * wc[:, sl] + _bdot(S, G) + Hm
        S_ref[h] = S_new
        sf_ref[h] = S_new


def _rwkv_scan(r, k, v, kn, lw, a, k_a, s0, B, T):
    N, C = r.shape
    Cc = RWKV_CHUNK
    nc = T // Cc
    H, Nh = RWKV_HEADS, RWKV_HEAD

    def row(b, d, c):
        return b * nc + c + d * (nc - 1 - 2 * c)

    tok = pl.BlockSpec((Cc, C), lambda b, d, c: (row(b, d, c), 0))
    tok2 = pl.BlockSpec((None, Cc, C), lambda b, d, c: (d, row(b, d, c), 0))
    st = pl.BlockSpec((None, None, H, Nh, Nh), lambda b, d, c: (d, b, 0, 0, 0))
    return pl.pallas_call(
        _rwkv_chunk_kernel,
        out_shape=(jax.ShapeDtypeStruct((2, N, C), F32),
                   jax.ShapeDtypeStruct((2, B, H, Nh, Nh), F32)),
        grid=(B, 2, nc),
        in_specs=[tok, tok, tok, tok, tok2, tok2,
                  pl.BlockSpec((1, C), lambda b, d, c: (0, 0)), st],
        out_specs=(tok2, st),
        scratch_shapes=[pltpu.VMEM((H, Nh, Nh), F32)],
        compiler_params=_cparams("arbitrary", "arbitrary", "arbitrary"),
        name="rwkv_scan",
    )(r, k, v, kn, lw, a, k_a, s0)


def _rwkv_post_kernel(y_ref, bonus_ref, g_ref, gg_ref, gb_ref, bd_ref, o_ref):
    y = y_ref[0] + y_ref[1]
    bd = bd_ref[...]
    inv_n = 1.0 / RWKV_HEAD
    mean = _dot_exact_rhs(y, bd) * inv_n
    yc = y - mean
    var = _dot_exact_rhs(yc * yc, bd) * inv_n
    yn = yc * lax.rsqrt(var + GN_EPS) * gg_ref[...] + gb_ref[...]
    o_ref[...] = ((yn + bonus_ref[...]) * g_ref[...]).astype(o_ref.dtype)


def _rwkv_post(y, bonus, g, gn_g, gn_b, bd):
    _, N, C = y.shape
    tm = _pick(N, (1024, 512, 256))
    row = pl.BlockSpec((tm, C), lambda i: (i, 0))
    vec = pl.BlockSpec((1, C), lambda i: (0, 0))
    return pl.pallas_call(
        _rwkv_post_kernel,
        out_shape=jax.ShapeDtypeStruct((N, C), BF16),
        grid=(N // tm,),
        in_specs=[pl.BlockSpec((2, tm, C), lambda i: (0, i, 0)), row, row, vec, vec,
                  pl.BlockSpec((C, C), lambda i: (0, 0))],
        out_specs=row,
        compiler_params=_cparams("arbitrary"),
        name="rwkv_post",
    )(y, bonus, g, gn_g.reshape(1, C), gn_b.reshape(1, C), bd)


def _fourier_ch_kernel(p_ref, m_ref, o_ref):
    p = p_ref[...]
    mat = m_ref[...]
    G = FOURIER_GROUP_DIM
    for g in range(p.shape[1] // G):
        o_ref[:, g * G:(g + 1) * G] = jnp.dot(p[:, g * G:(g + 1) * G], mat,
                                              preferred_element_type=F32).astype(o_ref.dtype)


def _fourier_channel(pf, ch_mats, B, T):
    N, Fd = pf.shape
    tm = _pick(T, (1024, 512, 256))
    nt = T // tm
    G = FOURIER_GROUP_DIM
    return pl.pallas_call(
        _fourier_ch_kernel,
        out_shape=jax.ShapeDtypeStruct((2, T, B * Fd), BF16),
        grid=(B, nt, 2),
        in_specs=[pl.BlockSpec((tm, Fd), lambda b, i, s: (b * nt + i, 0)),
                  pl.BlockSpec((None, G, G), lambda b, i, s: (s, 0, 0))],
        out_specs=pl.BlockSpec((None, tm, Fd), lambda b, i, s: (s, i, b)),
        compiler_params=_cparams("arbitrary", "arbitrary", "arbitrary"),
        name="fourier_channel",
    )(pf, ch_mats)


def _dft_mats(n):
    idx = jnp.arange(n, dtype=jnp.int32)
    ph = (idx[:, None] * idx[None, :]) % n
    ang = ph.astype(F32) * (2.0 * math.pi / n)
    s = 1.0 / math.sqrt(n)
    return jnp.cos(ang) * s, jnp.sin(ang) * s


def _merge_kernel(fa_ref, at_ref, rw_ref, gt_ref, x_ref, g1_ref, wa_ref, wb_ref, wc_ref, wo_ref, o_ref):
    D = x_ref.shape[1]
    ya = jnp.dot(fa_ref[...], wa_ref[...], preferred_element_type=F32)
    yb = jnp.dot(at_ref[0], wb_ref[0], preferred_element_type=F32)
    for h in range(1, at_ref.shape[0]):
        yb = yb + jnp.dot(at_ref[h], wb_ref[h], preferred_element_type=F32)
    yc = jnp.dot(rw_ref[...], wc_ref[...], preferred_element_type=F32)
    gt = gt_ref[...].astype(F32)
    merged = (_sigmoid(gt[:, :D]) * ya + _sigmoid(gt[:, D:2 * D]) * yb + _sigmoid(gt[:, 2 * D:]) * yc)
    out = jnp.dot(merged.astype(BF16), wo_ref[...], preferred_element_type=F32)
    o_ref[...] = x_ref[...] + g1_ref[...] * out


def _merge(fa, attn, rw, gates, x, g1, wa, wb, wc, wo, B, T):
    N, D = x.shape
    H = attn.shape[1]
    Fd = rw.shape[1]
    tm = _pick(T, (512, 256))
    nt = T // tm
    Bm = g1.shape[0]
    full = lambda shape: pl.BlockSpec(shape, lambda b, i: (0,) * len(shape))
    return pl.pallas_call(
        _merge_kernel,
        out_shape=jax.ShapeDtypeStruct((N, D), F32),
        grid=(B, nt),
        in_specs=[pl.BlockSpec((tm, Fd), lambda b, i: (i, b)),
                  pl.BlockSpec((None, H, tm, LANES), lambda b, i: (b, 0, i, 0)),
                  pl.BlockSpec((tm, Fd), lambda b, i: (b * nt + i, 0)),
                  pl.BlockSpec((tm, 3 * D), lambda b, i: (b * nt + i, 0)),
                  pl.BlockSpec((tm, D), lambda b, i: (b * nt + i, 0)),
                  pl.BlockSpec((None, 1, D), lambda b, i: (b % Bm, 0, 0)),
                  full(wa.shape), full(wb.shape), full(wc.shape), full(wo.shape)],
        out_specs=pl.BlockSpec((tm, D), lambda b, i: (b * nt + i, 0)),
        compiler_params=_cparams("arbitrary", "arbitrary"),
        name="merge_out",
    )(fa, attn, rw, gates, x, g1, wa, wb, wc, wo)


def _route(scores, sel, n_exp):
    tm = scores.shape[0]
    lane = lax.broadcasted_iota(jnp.int32, (tm, LANES), 1)
    neg = -jnp.inf
    sel = jnp.where(lane < n_exp, sel, neg)
    epg = n_exp // N_GROUPS
    assert epg == 4
    odd = (lane % 2) == 1
    p1 = jnp.where(odd, pltpu.roll(sel, 1, axis=1), pltpu.roll(sel, LANES - 1, axis=1))
    hi1, lo1 = jnp.maximum(sel, p1), jnp.minimum(sel, p1)
    upper = (lane % 4) >= 2
    hi2 = jnp.where(upper, pltpu.roll(hi1, 2, axis=1), pltpu.roll(hi1, LANES - 2, axis=1))
    lo2 = jnp.where(upper, pltpu.roll(lo1, 2, axis=1), pltpu.roll(lo1, LANES - 2, axis=1))
    gscore = jnp.maximum(hi1, hi2) + jnp.maximum(jnp.minimum(hi1, hi2), jnp.maximum(lo1, lo2))
    gscore = jnp.where(lane < n_exp, gscore, neg)
    gmax = jnp.max(gscore, axis=1, keepdims=True)
    lane_f = lane.astype(F32)
    grp = (lane // epg).astype(F32)
    big = float(LANES)
    best = jnp.min(jnp.where(gscore == gmax, grp, big), axis=1, keepdims=True)
    masked = jnp.where(grp == best, sel, neg)
    m1 = jnp.max(masked, axis=1, keepdims=True)
    i1 = jnp.min(jnp.where(masked == m1, lane_f, big), axis=1, keepdims=True)
    masked2 = jnp.where(lane_f == i1, neg, masked)
    m2 = jnp.max(masked2, axis=1, keepdims=True)
    i2 = jnp.min(jnp.where(masked2 == m2, lane_f, big), axis=1, keepdims=True)
    pick1 = lane_f == i1
    pick2 = lane_f == i2
    s1 = jnp.sum(jnp.where(pick1, scores, 0.0), axis=1, keepdims=True)
    s2 = jnp.sum(jnp.where(pick2, scores, 0.0), axis=1, keepdims=True)
    return (jnp.where(pick1, s1, 0.0) + jnp.where(pick2, s2, 0.0)) / (s1 + s2)


def _moe_kernel(x_ref, g_ref, sc_ref, sh_ref, g2_ref, rw_ref, rb_ref,
                wg_ref, wu_ref, wd_ref, swg_ref, swu_ref, swd_ref, o_ref,
                m_ref, comb_ref, acc_ref, *, n_exp):
    e = pl.program_id(1)

    @pl.when(e == 0)
    def _():
        x = x_ref[...]
        m = x * lax.rsqrt(jnp.mean(x * x, axis=-1, keepdims=True) + NORM_EPS) * g_ref[...]
        m = m * (1.0 + sc_ref[...]) + sh_ref[...]
        m_ref[...] = m.astype(BF16)
        scores = _sigmoid(_dot_split(m, rw_ref[...]))
        comb_ref[...] = _route(scores, scores + rb_ref[...], n_exp)
        acc_ref[...] = jnp.zeros_like(acc_ref)

    m = m_ref[...]

    @pl.when(e < n_exp)
    def _():
        lane = lax.broadcasted_iota(jnp.int32, comb_ref.shape, 1)
        ce = jnp.sum(jnp.where(lane == e, comb_ref[...], 0.0), axis=1, keepdims=True)
        hg = jnp.dot(m, wg_ref[...].astype(BF16), preferred_element_type=F32)
        hu = jnp.dot(m, wu_ref[...].astype(BF16), preferred_element_type=F32)
        hh = hg * _sigmoid(hg) * hu * ce
        acc_ref[...] += jnp.dot(hh.astype(BF16), wd_ref[...].astype(BF16), preferred_element_type=F32)

    @pl.when(e == n_exp)
    def _():
        hg = jnp.dot(m, swg_ref[...], preferred_element_type=F32)
        hu = jnp.dot(m, swu_ref[...], preferred_element_type=F32)
        hh = hg * _sigmoid(hg) * hu
        y = acc_ref[...] + jnp.dot(hh.astype(BF16), swd_ref[...], preferred_element_type=F32)
        o_ref[...] = x_ref[...] + g2_ref[...] * y


def _moe(x, g, sc, sh, g2, rw_pad, rb_pad, wg, wu, wd, swg, swu, swd):
    N, D = x.shape
    E, _, Ff = wg.shape
    Bm = sc.shape[0]
    rpm = N // Bm
    tm = min(rpm, 1024)
    assert rpm % tm == 0
    mod = pl.BlockSpec((None, 1, D), lambda i, e: (i * tm // rpm, 0, 0))
    full = lambda shape: pl.BlockSpec(shape, lambda i, e: (0,) * len(shape))
    return pl.pallas_call(
        functools.partial(_moe_kernel, n_exp=E),
        out_shape=jax.ShapeDtypeStruct((N, D), F32),
        grid=(N // tm, E + 1),
        in_specs=[pl.BlockSpec((tm, D), lambda i, e: (i, 0)),
                  full((1, D)), mod, mod, mod,
                  full(rw_pad.shape), full(rb_pad.shape),
                  pl.BlockSpec((None, D, Ff), lambda i, e: (jnp.minimum(e, E - 1), 0, 0)),
                  pl.BlockSpec((None, D, Ff), lambda i, e: (jnp.minimum(e, E - 1), 0, 0)),
                  pl.BlockSpec((None, Ff, D), lambda i, e: (jnp.minimum(e, E - 1), 0, 0)),
                  full(swg.shape), full(swu.shape), full(swd.shape)],
        out_specs=pl.BlockSpec((tm, D), lambda i, e: (i, 0)),
        scratch_shapes=[pltpu.VMEM((tm, D), BF16), pltpu.VMEM((tm, LANES), F32),
                        pltpu.VMEM((tm, D), F32)],
        compiler_params=_cparams("arbitrary", "arbitrary"),
        name="moe",
    )(x, g.reshape(1, D), sc, sh, g2, rw_pad, rb_pad, wg, wu, wd, swg, swu, swd)


def _rope_partner(w):
    lead = w.shape[:-1]
    wr = w.reshape(lead + (2, 2, ROPE_PAIRS))
    return jnp.stack([-wr[..., 1, :], wr[..., 0, :]], axis=-2).reshape(w.shape)


def _rope_tables(T):
    rows = T // GRID_W
    row = jnp.repeat(jnp.arange(rows), GRID_W)
    col = jnp.tile(jnp.arange(GRID_W), rows)
    inv = ROPE_BASE ** (-jnp.arange(ROPE_PAIRS, dtype=F32) / ROPE_PAIRS)
    ang = jnp.stack([row, col], axis=-1).astype(F32)[..., None] * inv
    cos = jnp.broadcast_to(jnp.cos(ang)[:, :, None, :], (T, 2, 2, ROPE_PAIRS)).reshape(T, MLA_ROPE)
    sin = jnp.broadcast_to(jnp.sin(ang)[:, :, None, :], (T, 2, 2, ROPE_PAIRS)).reshape(T, MLA_ROPE)
    pad = LANES - MLA_QK
    z_n, z_p = jnp.zeros((T, MLA_NOPE), F32), jnp.zeros((T, pad), F32)
    cq = jnp.concatenate([jnp.ones((T, MLA_NOPE), F32), cos, z_p], axis=1)
    sq = jnp.concatenate([z_n, sin, z_p], axis=1)
    ck = jnp.concatenate([z_n, cos, z_p], axis=1)
    return cq, sq, ck, sq


def _identity_tables(T):
    pad = LANES - MLA_QK
    z_n, z_p = jnp.zeros((T, MLA_NOPE), F32), jnp.zeros((T, pad), F32)
    one_r = jnp.ones((T, MLA_ROPE), F32)
    cq = jnp.concatenate([jnp.ones((T, MLA_NOPE), F32), one_r, z_p], axis=1)
    ck = jnp.concatenate([z_n, one_r, z_p], axis=1)
    zero = jnp.zeros((T, LANES), F32)
    return cq, zero, ck, zero


def kernel(x, c, ctx, c_ctx, w_ada, b_ada, norm1_g, norm2_g, w_in, q_norm_g, kv_norm_g, w_uq, w_ukv, rwkv_mu, rwkv_w0, rwkv_w_up, rwkv_a0, rwkv_a_up, rwkv_g_up, rwkv_k_k, rwkv_k_a, rwkv_r_k, rwkv_gn_g, rwkv_gn_b, w_proj_a, w_proj_b, w_proj_c, w_out, router_w, router_b, exp_w_gate, exp_w_up, exp_w_down, sh_w_gate, sh_w_up, sh_w_down, final_norm_g):
    B, T, D = x.shape
    Tc = ctx.shape[1]
    L = w_ada.shape[0]
    H = MLA_HEADS
    QL, KVL = q_norm_g.shape[1], kv_norm_g.shape[1]
    Fd = w_proj_a.shape[1]
    C = RWKV_DIM
    E = router_w.shape[1]
    N, Nc = B * T, B * Tc

    sizes = (Fd, QL, KVL, MLA_ROPE, 3 * C + DECAY_LORA + AAA_LORA + GATE_LORA, D, D, D)
    offs = np.concatenate([[0], np.cumsum(sizes)])

    lat_tabs = _rope_tables(T)
    ctx_tabs = _identity_tables(Tc)
    cc, cs = _dft_mats(FOURIER_GROUP_DIM)
    ch_mats = jnp.stack([cc, cs]).astype(BF16)
    tcos, tsin = _dft_mats(T)
    dft_lat = jnp.concatenate([tcos, -tsin], axis=1).astype(BF16)
    ccos, csin = _dft_mats(Tc)
    dft_ctx = jnp.concatenate([ccos, -csin], axis=1).astype(BF16)
    hid = jnp.arange(C) // RWKV_HEAD
    bd = (hid[:, None] == hid[None, :]).astype(BF16)
    rw_pad = jnp.zeros((D, LANES), F32).at[:, :E].set(router_w)
    rb_pad = jnp.zeros((1, LANES), F32).at[0, :E].set(router_b)

    R = -(-(B + 1) // 8) * 8
    c_rows = jnp.zeros((R, D), F32).at[:B].set(c).at[B].set(c_ctx)
    mod = _ada_mod(c_rows, w_ada, b_ada)

    x = x.reshape(N, D)
    ctx = ctx.reshape(Nc, D)
    zero_state = jnp.zeros((2, B, RWKV_HEADS, RWKV_HEAD, RWKV_HEAD), F32)

    for l in range(L):
        last = l == L - 1
        ml = mod[l, :B].reshape(B, 1, 6, D)
        mc = mod[l, B:B + 1].reshape(1, 1, 6, D)
        sh1, sc1, g1, sh2, sc2, g2 = [ml[:, :, j] for j in range(6)]
        csh1, csc1, cg1, csh2, csc2, cg2 = [mc[:, :, j] for j in range(6)]

        wi = w_in[l]
        w_f = wi[:, offs[0]:offs[1]].astype(BF16)
        w_kr = wi[:, offs[3]:offs[4]]
        zl = jnp.zeros((D, MLA_NOPE), F32)
        zr = jnp.zeros((D, LANES - MLA_QK), F32)
        w_qkv = jnp.concatenate([wi[:, offs[1]:offs[3]], zl, w_kr, zr, zl, _rope_partner(w_kr), zr],
                                axis=1).astype(BF16)
        w_rw = wi[:, offs[4]:offs[5]].astype(BF16)
        w_gt = wi[:, offs[5]:offs[8]].astype(BF16)
        uq = w_uq[l].reshape(QL, H, MLA_QK).transpose(1, 0, 2)
        padq = jnp.zeros((H, QL, LANES - MLA_QK), F32)
        wqa = jnp.concatenate([uq, padq], axis=2).astype(BF16)
        wqb = jnp.concatenate([jnp.zeros((H, QL, MLA_NOPE), F32), _rope_partner(uq[..., MLA_NOPE:]), padq],
                              axis=2).astype(BF16)
        ukv = w_ukv[l].reshape(KVL, H, MLA_NOPE + MLA_V).transpose(1, 0, 2)
        wk = jnp.concatenate([ukv[..., :MLA_NOPE], jnp.zeros((H, KVL, LANES - MLA_NOPE), F32)], axis=2).astype(BF16)
        wv = jnp.concatenate([ukv[..., MLA_NOPE:], jnp.zeros((H, KVL, LANES - MLA_V), F32)], axis=2).astype(BF16)
        wb = jnp.concatenate([w_proj_b[l].reshape(H, MLA_V, D), jnp.zeros((H, LANES - MLA_V, D), F32)],
                             axis=1).astype(BF16)
        wa = w_proj_a[l].astype(BF16)
        wc = w_proj_c[l].astype(BF16)
        wo = w_out[l].astype(BF16)
        zd = jnp.zeros((2, DECAY_LORA, C), F32)
        prm = dict(
            mu=rwkv_mu[l].reshape(1, -1), w0=rwkv_w0[l], a0=rwkv_a0[l],
            wup=jnp.concatenate([rwkv_w_up[l], zd], axis=1).astype(BF16),
            aup=jnp.concatenate([zd, rwkv_a_up[l]], axis=1).astype(BF16),
            gup=rwkv_g_up[l].astype(BF16), k_k=rwkv_k_k[l].reshape(1, C), k_a=rwkv_k_a[l].reshape(1, C),
            r_k=rwkv_r_k[l].reshape(1, C), bd=bd)

        def mixers(xx, sc, sh, tabs, Tx):
            hcur = _norm(xx, norm1_g[l], sc, sh, BF16)
            pf = _mm(hcur, w_f, BF16)
            pqkv = _mm(hcur, w_qkv, BF16)
            pr = _mm(hcur, w_rw, BF16)
            gates = _mm(hcur, w_gt, BF16)
            q, k, v = _qkv_project(pqkv, q_norm_g[l], kv_norm_g[l], wqa, wqb, wk, wv, tabs, B, Tx)
            rp = _rwkv_prep(pr, prm, B, Tx)
            return pf, gates, q, k, v, rp

        pf_l, gates_l, q_l, k_l, v_l, rp_l = mixers(x, sc1, sh1, lat_tabs, T)
        pf_c, gates_c, q_c, k_c, v_c, rp_c = mixers(ctx, csc1, csh1, ctx_tabs, Tc)

        attn_l = _attention(q_l, jnp.concatenate([k_l, k_c], axis=2), jnp.concatenate([v_l, v_c], axis=2))

        y_c, s_ctx = _rwkv_scan(*rp_c[:6], prm["k_a"], zero_state, B, Tc)
        y_l, _ = _rwkv_scan(*rp_l[:6], prm["k_a"], s_ctx, B, T)
        rwo_l = _rwkv_post(y_l, rp_l[7], rp_l[6], rwkv_gn_g[l], rwkv_gn_b[l], bd)

        zcs_l = _fourier_channel(pf_l, ch_mats, B, T).reshape(2 * T, B * Fd)
        fa_l = _mm(dft_lat, zcs_l, BF16, tn=B * Fd, tk=min(2 * T, 1024))

        x = _merge(fa_l, attn_l, rwo_l, gates_l, x, g1, wa, wb, wc, wo, B, T)
        moe_w = (rw_pad, rb_pad, exp_w_gate[l], exp_w_up[l], exp_w_down[l],
                 sh_w_gate[l].astype(BF16), sh_w_up[l].astype(BF16), sh_w_down[l].astype(BF16))
        x = _moe(x, norm2_g[l], sc2, sh2, g2, *moe_w)

        if not last:
            attn_c = _attention(q_c, k_c, v_c)
            rwo_c = _rwkv_post(y_c, rp_c[7], rp_c[6], rwkv_gn_g[l], rwkv_gn_b[l], bd)
            zcs_c = _fourier_channel(pf_c, ch_mats, B, Tc).reshape(2 * Tc, B * Fd)
            fa_c = _mm(dft_ctx, zcs_c, BF16, tn=B * Fd, tk=min(2 * Tc, 1024))
            ctx = _merge(fa_c, attn_c, rwo_c, gates_c, ctx, cg1, wa, wb, wc, wo, B, Tc)
            ctx = _moe(ctx, norm2_g[l], csc2, csh2, cg2, *moe_w)

    ones = jnp.ones((1, 1, D), F32)
    out = _norm(x, final_norm_g, ones, ones, F32, modulate=False)
    return out.reshape(B, T, D)
```

```python
import functools
import math

import numpy as np
import jax
import jax.numpy as jnp
from jax import lax
from jax.experimental import pallas as pl
from jax.experimental.pallas import tpu as pltpu

F32 = jnp.float32
BF16 = jnp.bfloat16

GRID_W = 64
FOURIER_GROUP_DIM = 128
MLA_HEADS = 8
MLA_NOPE = 64
MLA_ROPE = 32
MLA_QK = MLA_NOPE + MLA_ROPE
MLA_V = 64
MLA_SCALE = MLA_QK ** -0.5
ROPE_PAIRS = MLA_ROPE // 4
ROPE_BASE = 10000.0
RWKV_HEADS = 8
RWKV_HEAD = 64
RWKV_DIM = RWKV_HEADS * RWKV_HEAD
DECAY_LORA = 64
AAA_LORA = 64
GATE_LORA = 128
GN_EPS = 64e-5
N_GROUPS = 4
NORM_EPS = 1e-6

LANES = 128
VMEM_LIMIT_BYTES = 48 * 1024 * 1024
RWKV_CHUNK = 64


def _cparams(*sem):
    return pltpu.CompilerParams(dimension_semantics=sem, vmem_limit_bytes=VMEM_LIMIT_BYTES)


def _sigmoid(z):
    return 1.0 / (1.0 + jnp.exp(-z))


def _bdot(a, b):
    return jnp.dot(a.astype(BF16), b.astype(BF16), preferred_element_type=F32)


def _dot_nt(a, b):
    return lax.dot_general(a.astype(BF16), b.astype(BF16), (((1,), (1,)), ((), ())),
                           preferred_element_type=F32)


def _dot_tn(a, b):
    return lax.dot_general(a.astype(BF16), b.astype(BF16), (((0,), (0,)), ((), ())),
                           preferred_element_type=F32)


def _split2(x):
    hi = x.astype(BF16)
    lo = (x - hi.astype(F32)).astype(BF16)
    return hi, lo


def _dot_exact_rhs(x, ones_bf16):
    hi, lo = _split2(x)
    return (jnp.dot(hi, ones_bf16, preferred_element_type=F32)
            + jnp.dot(lo, ones_bf16, preferred_element_type=F32))


def _dot_split(a, b):
    ah, al = _split2(a)
    bh, bl = _split2(b)
    return (jnp.dot(ah, bh, preferred_element_type=F32)
            + jnp.dot(ah, bl, preferred_element_type=F32)
            + jnp.dot(al, bh, preferred_element_type=F32))


def _ada_kernel(c_ref, w_ref, b_ref, o_ref):
    c = c_ref[...]
    s = c * _sigmoid(c)
    o_ref[...] = _dot_split(s, w_ref[...]) + b_ref[...]


def _ada_mod(c_rows, w_ada, b_ada):
    L, D, D6 = w_ada.shape
    R = c_rows.shape[0]
    tn = 1536 if D6 % 1536 == 0 else D6
    return pl.pallas_call(
        _ada_kernel,
        out_shape=jax.ShapeDtypeStruct((L, R, D6), F32),
        grid=(L, D6 // tn),
        in_specs=[pl.BlockSpec((R, D), lambda l, j: (0, 0)),
                  pl.BlockSpec((None, D, tn), lambda l, j: (l, 0, j)),
                  pl.BlockSpec((None, 1, tn), lambda l, j: (l, 0, j))],
        out_specs=pl.BlockSpec((None, R, tn), lambda l, j: (l, 0, j)),
        compiler_params=_cparams("arbitrary", "arbitrary"),
        name="ada_mod",
    )(c_rows, w_ada, b_ada.reshape(L, 1, D6))


def _norm_kernel(x_ref, g_ref, sc_ref, sh_ref, o_ref, *, modulate):
    x = x_ref[...].astype(F32)
    y = x * lax.rsqrt(jnp.mean(x * x, axis=-1, keepdims=True) + NORM_EPS) * g_ref[...]
    if modulate:
        y = y * (1.0 + sc_ref[...]) + sh_ref[...]
    o_ref[...] = y.astype(o_ref.dtype)


def _norm(x, g, sc, sh, out_dtype, modulate=True):
    N, D = x.shape
    Bm = sc.shape[0]
    rpm = N // Bm
    tm = min(rpm, 1024)
    assert rpm % tm == 0
    return pl.pallas_call(
        functools.partial(_norm_kernel, modulate=modulate),
        out_shape=jax.ShapeDtypeStruct((N, D), out_dtype),
        grid=(N // tm,),
        in_specs=[pl.BlockSpec((tm, D), lambda i: (i, 0)),
                  pl.BlockSpec((1, D), lambda i: (0, 0)),
                  pl.BlockSpec((None, 1, D), lambda i: (i * tm // rpm, 0, 0)),
                  pl.BlockSpec((None, 1, D), lambda i: (i * tm // rpm, 0, 0))],
        out_specs=pl.BlockSpec((tm, D), lambda i: (i, 0)),
        compiler_params=_cparams("arbitrary"),
        name="rmsnorm_mod",
    )(x, g.reshape(1, D), sc, sh)


def _mm_kernel(a_ref, b_ref, o_ref):
    o_ref[...] = jnp.dot(a_ref[...], b_ref[...], preferred_element_type=F32).astype(o_ref.dtype)


def _mm_acc_kernel(a_ref, b_ref, o_ref, acc_ref, *, nk):
    k = pl.program_id(2)

    @pl.when(k == 0)
    def _():
        acc_ref[...] = jnp.zeros_like(acc_ref)

    acc_ref[...] += jnp.dot(a_ref[...], b_ref[...], preferred_element_type=F32)

    @pl.when(k == nk - 1)
    def _():
        o_ref[...] = acc_ref[...].astype(o_ref.dtype)


def _pick(n, prefs):
    for p in prefs:
        if n % p == 0:
            return p
    return n


def _mm(a, w, out_dtype, tn=None, tk=None):
    M, K = a.shape
    Nn = w.shape[1]
    tm = _pick(M, (1024, 512, 256))
    tn = tn or _pick(Nn, (1024, 896, 512))
    tk = tk or K
    if tk == K:
        return pl.pallas_call(
            _mm_kernel,
            out_shape=jax.ShapeDtypeStruct((M, Nn), out_dtype),
            grid=(M // tm, Nn // tn),
            in_specs=[pl.BlockSpec((tm, K), lambda i, j: (i, 0)),
                      pl.BlockSpec((K, tn), lambda i, j: (0, j))],
            out_specs=pl.BlockSpec((tm, tn), lambda i, j: (i, j)),
            compiler_params=_cparams("arbitrary", "arbitrary"),
            name="matmul",
        )(a, w)
    nk = K // tk
    return pl.pallas_call(
        functools.partial(_mm_acc_kernel, nk=nk),
        out_shape=jax.ShapeDtypeStruct((M, Nn), out_dtype),
        grid=(M // tm, Nn // tn, nk),
        in_specs=[pl.BlockSpec((tm, tk), lambda i, j, k: (i, k)),
                  pl.BlockSpec((tk, tn), lambda i, j, k: (k, j))],
        out_specs=pl.BlockSpec((tm, tn), lambda i, j, k: (i, j)),
        scratch_shapes=[pltpu.VMEM((tm, tn), F32)],
        compiler_params=_cparams("arbitrary", "arbitrary", "arbitrary"),
        name="matmul_kacc",
    )(a, w)


def _qkv_kernel(p_ref, gq_ref, gkv_ref, wqa_ref, wqb_ref, wk_ref, wv_ref,
                cq_ref, sq_ref, ck_ref, sk_ref, q_ref, k_ref, v_ref,
                qn_ref, kvn_ref, kr_ref, *, ql, kvl):
    h = pl.program_id(1)

    @pl.when(h == 0)
    def _():
        p = p_ref[...].astype(F32)
        pq = p[:, :ql]
        pkv = p[:, ql:ql + kvl]
        kra = p[:, ql + kvl:ql + kvl + LANES]
        krb = p[:, ql + kvl + LANES:ql + kvl + 2 * LANES]
        qn = pq * lax.rsqrt(jnp.mean(pq * pq, axis=-1, keepdims=True) + NORM_EPS) * gq_ref[...]
        kvn = pkv * lax.rsqrt(jnp.mean(pkv * pkv, axis=-1, keepdims=True) + NORM_EPS) * gkv_ref[...]
        qn_ref[...] = qn.astype(BF16)
        kvn_ref[...] = kvn.astype(BF16)
        kr_ref[...] = kra * ck_ref[...] + krb * sk_ref[...]

    qn = qn_ref[...]
    kvn = kvn_ref[...]
    qa = jnp.dot(qn, wqa_ref[...], preferred_element_type=F32)
    qb = jnp.dot(qn, wqb_ref[...], preferred_element_type=F32)
    q_ref[...] = ((qa * cq_ref[...] + qb * sq_ref[...]) * MLA_SCALE).astype(q_ref.dtype)
    k_ref[...] = (jnp.dot(kvn, wk_ref[...], preferred_element_type=F32) + kr_ref[...]).astype(k_ref.dtype)
    v_ref[...] = jnp.dot(kvn, wv_ref[...], preferred_element_type=F32).astype(v_ref.dtype)


def _qkv_project(pqkv, gq, gkv, wqa, wqb, wk, wv, tabs, B, T):
    N, W = pqkv.shape
    H = wqa.shape[0]
    ql, kvl = wqa.shape[1], wk.shape[1]
    tm = _pick(T, (512, 256))
    nt = T // tm
    cq, sq, ck, sk = tabs
    tab_spec = pl.BlockSpec((tm, LANES), lambda i, h: (i % nt, 0))
    out_spec = pl.BlockSpec((None, None, tm, LANES), lambda i, h: (i // nt, h, i % nt, 0))
    out_sds = jax.ShapeDtypeStruct((B, H, T, LANES), BF16)
    return pl.pallas_call(
        functools.partial(_qkv_kernel, ql=ql, kvl=kvl),
        out_shape=(out_sds, out_sds, out_sds),
        grid=(N // tm, H),
        in_specs=[pl.BlockSpec((tm, W), lambda i, h: (i, 0)),
                  pl.BlockSpec((1, ql), lambda i, h: (0, 0)),
                  pl.BlockSpec((1, kvl), lambda i, h: (0, 0)),
                  pl.BlockSpec((None, ql, LANES), lambda i, h: (h, 0, 0)),
                  pl.BlockSpec((None, ql, LANES), lambda i, h: (h, 0, 0)),
                  pl.BlockSpec((None, kvl, LANES), lambda i, h: (h, 0, 0)),
                  pl.BlockSpec((None, kvl, LANES), lambda i, h: (h, 0, 0)),
                  tab_spec, tab_spec, tab_spec, tab_spec],
        out_specs=(out_spec, out_spec, out_spec),
        scratch_shapes=[pltpu.VMEM((tm, ql), BF16), pltpu.VMEM((tm, kvl), BF16),
                        pltpu.VMEM((tm, LANES), F32)],
        compiler_params=_cparams("arbitrary", "arbitrary"),
        name="mla_qkv",
    )(pqkv, gq.reshape(1, ql), gkv.reshape(1, kvl), wqa, wqb, wk, wv, cq, sq, ck, sk)


def _attn_kernel(q_ref, k_ref, v_ref, o_ref):
    s = _dot_nt(q_ref[...], k_ref[...])
    m = jnp.max(s, axis=-1, keepdims=True)
    p = jnp.exp(s - m)
    l = jnp.sum(p, axis=-1, keepdims=True)
    o = jnp.dot(p.astype(BF16), v_ref[...], preferred_element_type=F32)
    o_ref[...] = (o / l).astype(o_ref.dtype)


def _attention(q, k, v):
    B, H, T, dq = q.shape
    Tk = k.shape[2]
    tq = _pick(T, (256,))
    return pl.pallas_call(
        _attn_kernel,
        out_shape=jax.ShapeDtypeStruct((B, H, T, LANES), BF16),
        grid=(B, H, T // tq),
        in_specs=[pl.BlockSpec((None, None, tq, dq), lambda b, h, i: (b, h, i, 0)),
                  pl.BlockSpec((None, None, Tk, dq), lambda b, h, i: (b, h, 0, 0)),
                  pl.BlockSpec((None, None, Tk, LANES), lambda b, h, i: (b, h, 0, 0))],
        out_specs=pl.BlockSpec((None, None, tq, LANES), lambda b, h, i: (b, h, i, 0)),
        compiler_params=_cparams("arbitrary", "arbitrary", "arbitrary"),
        name="attention",
    )(q, k, v)


def _softplus(u):
    return jnp.maximum(u, 0.0) + jnp.log(1.0 + jnp.exp(-jnp.abs(u)))


def _rwkv_prep_kernel(p_ref, pp_ref, pn_ref, mu_ref, w0_ref, wup_ref, a0_ref, aup_ref, gup_ref,
                      kk_ref, ka_ref, rk_ref, bd_ref,
                      r_o, k_o, v_o, kn_o, lw_o, a_o, g_o, bonus_o, *, tps):
    i = pl.program_id(0)
    p = p_ref[...].astype(F32)
    tt = p.shape[0]
    C = RWKV_DIM
    halo = pp_ref.shape[0]
    prev_row = jnp.where(i % tps == 0, 0.0, pp_ref[...].astype(F32)[halo - 1:halo, :])
    next_row = jnp.where(i % tps == tps - 1, 0.0, pn_ref[...].astype(F32)[0:1, :])
    rows = lax.broadcasted_iota(jnp.int32, (tt, 1), 0)
    pprev = jnp.where(rows == 0, prev_row, pltpu.roll(p, 1, axis=0))
    pnext = jnp.where(rows == tt - 1, next_row, pltpu.roll(p, tt - 1, axis=0))
    xs = p + (0.5 * (pprev + pnext) - p) * mu_ref[...]
    r = xs[:, :C]
    k = xs[:, C:2 * C]
    v = xs[:, 2 * C:3 * C]
    xwa = xs[:, 3 * C:3 * C + DECAY_LORA + AAA_LORA]
    xg = xs[:, 3 * C + DECAY_LORA + AAA_LORA:]
    bd = bd_ref[...]
    g_o[...] = _bdot(_sigmoid(xg), gup_ref[...])
    kn = k * kk_ref[...]
    kn = kn * lax.rsqrt(_dot_exact_rhs(kn * kn, bd) + 1e-12)
    th = jnp.tanh(xwa)
    bonus = jnp.zeros_like(r)
    for d in range(2):
        w = -_softplus(-(w0_ref[d:d + 1, :] + _bdot(th, wup_ref[d]))) - 0.5
        lw_o[d] = -jnp.exp(w)
        a = _sigmoid(a0_ref[d:d + 1, :] + _bdot(xwa, aup_ref[d]))
        a_o[d] = a
        kd = k * (1.0 + (a - 1.0) * ka_ref[...])
        bonus = bonus + _dot_exact_rhs(r * kd * rk_ref[...], bd) * v
    r_o[...] = r
    k_o[...] = k
    v_o[...] = v
    kn_o[...] = kn
    bonus_o[...] = bonus


def _rwkv_prep(pr, prm, B, T):
    N, W = pr.shape
    C = RWKV_DIM
    tt = _pick(T, (256,))
    tps = T // tt
    halo = 16
    nh = N // halo
    full = lambda shape: pl.BlockSpec(shape, lambda i: (0,) * len(shape))
    o_spec = pl.BlockSpec((tt, C), lambda i: (i, 0))
    o2_spec = pl.BlockSpec((2, tt, C), lambda i: (0, i, 0))
    sds = jax.ShapeDtypeStruct((N, C), F32)
    sds2 = jax.ShapeDtypeStruct((2, N, C), F32)
    return pl.pallas_call(
        functools.partial(_rwkv_prep_kernel, tps=tps),
        out_shape=(sds, sds, sds, sds, sds2, sds2, sds, sds),
        grid=(N // tt,),
        in_specs=[pl.BlockSpec((tt, W), lambda i: (i, 0)),
                  pl.BlockSpec((halo, W), lambda i: (jnp.maximum(i * (tt // halo) - 1, 0), 0)),
                  pl.BlockSpec((halo, W), lambda i: (jnp.minimum((i + 1) * (tt // halo), nh - 1), 0)),
                  full((1, W)), full((2, C)), full((2, LANES, C)), full((2, C)), full((2, LANES, C)),
                  full((GATE_LORA, C)), full((1, C)), full((1, C)), full((1, C)), full((C, C))],
        out_specs=(o_spec, o_spec, o_spec, o_spec, o2_spec, o2_spec, o_spec, o_spec),
        compiler_params=_cparams("arbitrary"),
        name="rwkv_prep",
    )(pr, pr, pr, prm["mu"], prm["w0"], prm["wup"], prm["a0"], prm["aup"], prm["gup"],
      prm["k_k"], prm["k_a"], prm["r_k"], prm["bd"])


def _bmm(a, b):
    return lax.dot_general(a.astype(BF16), b.astype(BF16), (((2,), (1,)), ((0,), (0,))),
                           preferred_element_type=F32)


def _bmm_nt(a, b):
    return lax.dot_general(a.astype(BF16), b.astype(BF16), (((2,), (2,)), ((0,), (0,))),
                           preferred_element_type=F32)


def _bmm_tn(a, b):
    return lax.dot_general(a.astype(BF16), b.astype(BF16), (((1,), (1,)), ((0,), (0,))),
                           preferred_element_type=F32)


def _rwkv_scan_kernel(rf_ref, kf_ref, vf_ref, knf_ref, lwf_ref, af_ref,
                      rr_ref, kr_ref, vr_ref, knr_ref, lwr_ref, ar_ref, ka_ref, s0_ref,
                      yf_ref, yr_ref, sf_ref, S_ref, *, G):
    j = pl.program_id(1)

    @pl.when(j == 0)
    def _():
        S_ref[...] = s0_ref[...]

    Cc, N, H = RWKV_CHUNK, RWKV_HEAD, RWKV_HEADS
    R = G * Cc
    ti = lax.broadcasted_iota(jnp.int32, (R, R), 0)
    ii = lax.broadcasted_iota(jnp.int32, (R, R), 1)
    same = (ti // Cc) == (ii // Cc)
    m_all = same.astype(F32).astype(BF16)

    def heads(x):
        return jnp.stack([x[g * Cc:(g + 1) * Cc, h * N:(h + 1) * N] for g in range(G) for h in range(H)])

    per_dir = []
    for d, refs in enumerate(((rf_ref, kf_ref, vf_ref, knf_ref, lwf_ref, af_ref),
                              (rr_ref, kr_ref, vr_ref, knr_ref, lwr_ref, ar_ref))):
        r, k, v, kn, lw, a = [x[...] for x in refs]
        before = (ii <= ti) if d == 0 else (ii >= ti)
        m_incl = (same & before).astype(F32).astype(BF16)
        h1 = lw.astype(BF16)
        r1 = lw - h1.astype(F32)
        h2 = r1.astype(BF16)
        h3 = (r1 - h2.astype(F32)).astype(BF16)
        cum = (jnp.dot(m_incl, h1, preferred_element_type=F32)
               + jnp.dot(m_incl, h2, preferred_element_type=F32)
               + jnp.dot(m_incl, h3, preferred_element_type=F32))
        tot = (jnp.dot(m_all, h1, preferred_element_type=F32)
               + jnp.dot(m_all, h2, preferred_element_type=F32)
               + jnp.dot(m_all, h3, preferred_element_type=F32))
        e_neg = jnp.exp(-cum)
        e_rem = jnp.exp(tot - cum)
        b = kn * a
        kd = k * (1.0 + (a - 1.0) * ka_ref[...])
        per_dir.append([heads(x) for x in (-kn * jnp.exp(cum - lw), r * jnp.exp(cum), b * e_neg, kd * e_neg,
                                           b * e_rem, kd * e_rem, v, jnp.exp(tot))])

    A, Rt, Bt, Kt, Bh, Kh, V, Wc = [jnp.concatenate([per_dir[0][i], per_dir[1][i]], axis=0) for i in range(8)]
    nb = 2 * G * H
    bi = lax.broadcasted_iota(jnp.int32, (nb, Cc, Cc), 0)
    t3 = lax.broadcasted_iota(jnp.int32, (nb, Cc, Cc), 1)
    i3 = lax.broadcasted_iota(jnp.int32, (nb, Cc, Cc), 2)
    order = jnp.where(bi < G * H, t3 - i3, i3 - t3)
    strict = order > 0
    incl = order >= 0

    AR = jnp.concatenate([A, Rt], axis=1)
    ab = _bmm_nt(AR, Bt)
    ak = _bmm_nt(AR, Kt)
    Lab = jnp.where(strict, ab[:, :Cc], 0.0)
    Mrb = jnp.where(incl, ab[:, Cc:], 0.0)
    Lak = jnp.where(strict, ak[:, :Cc], 0.0)
    Mrk = jnp.where(incl, ak[:, Cc:], 0.0)
    lv = _bmm(jnp.concatenate([Lak, Mrk], axis=1), V)
    Z = jnp.concatenate([A, lv[:, :Cc]], axis=2)
    P = Lab
    steps = int(math.log2(Cc))
    for s in range(steps):
        if s < steps - 1:
            out = _bmm(P, jnp.concatenate([Z, P], axis=2))
            Z = Z + out[:, :, :2 * N]
            P = out[:, :, 2 * N:]
        else:
            Z = Z + _bmm(P, Z)
    mz = _bmm(Mrb, Z)
    Rh = Rt + mz[:, :, :N]
    Yh = mz[:, :, N:] + lv[:, Cc:]
    gh = _bmm_tn(Z, Bh)
    Gm = gh[:, :N]
    Hm = gh[:, N:] + _bmm_tn(V, Kh)

    S = S_ref[...].reshape(2 * H, N, N)
    for g in range(G):
        gr = G - 1 - g
        pick = lambda x: jnp.concatenate([x[g * H:(g + 1) * H], x[(G + gr) * H:(G + gr + 1) * H]], axis=0)
        y = _bmm_nt(pick(Rh), S) + pick(Yh)
        S = S * pick(Wc)[:, 0:1, :] + _bmm(S, pick(Gm)) + pick(Hm)
        yf_ref[g * Cc:(g + 1) * Cc, :] = jnp.concatenate([y[h] for h in range(H)], axis=1)
        yr_ref[gr * Cc:(gr + 1) * Cc, :] = jnp.concatenate([y[H + h] for h in range(H)], axis=1)
    S = S.reshape(2, H, N, N)
    S_ref[...] = S
    sf_ref[...] = S


def _rwkv_scan(r, k, v, kn, lw, a, k_a, s0, B, T):
    N, C = r.shape
    Cc = RWKV_CHUNK
    nc = T // Cc
    G = 2 if nc % 2 == 0 else 1
    ng = nc // G
    H, Nh = RWKV_HEADS, RWKV_HEAD
    fwd = pl.BlockSpec((G * Cc, C), lambda b, j: (b * ng + j, 0))
    rev = pl.BlockSpec((G * Cc, C), lambda b, j: (b * ng + ng - 1 - j, 0))
    fwd2 = pl.BlockSpec((None, G * Cc, C), lambda b, j: (0, b * ng + j, 0))
    rev2 = pl.BlockSpec((None, G * Cc, C), lambda b, j: (1, b * ng + ng - 1 - j, 0))
    st = pl.BlockSpec((2, None, H, Nh, Nh), lambda b, j: (0, b, 0, 0, 0))
    yf, yr, sf = pl.pallas_call(
        functools.partial(_rwkv_scan_kernel, G=G),
        out_shape=(jax.ShapeDtypeStruct((N, C), F32), jax.ShapeDtypeStruct((N, C), F32),
                   jax.ShapeDtypeStruct((2, B, H, Nh, Nh), F32)),
        grid=(B, ng),
        in_specs=[fwd, fwd, fwd, fwd, fwd2, fwd2, rev, rev, rev, rev, rev2, rev2,
                  pl.BlockSpec((1, C), lambda b, j: (0, 0)), st],
        out_specs=(fwd, rev, st),
        scratch_shapes=[pltpu.VMEM((2, H, Nh, Nh), F32)],
        compiler_params=_cparams("arbitrary", "arbitrary"),
        name="rwkv_scan",
    )(r, k, v, kn, lw, a, r, k, v, kn, lw, a, k_a, s0)
    return yf, yr, sf


def _rwkv_post_kernel(yf_ref, yr_ref, bonus_ref, g_ref, gg_ref, gb_ref, bd_ref, o_ref):
    y = yf_ref[...] + yr_ref[...]
    bd = bd_ref[...]
    inv_n = 1.0 / RWKV_HEAD
    mean = _dot_exact_rhs(y, bd) * inv_n
    yc = y - mean
    var = _dot_exact_rhs(yc * yc, bd) * inv_n
    yn = yc * lax.rsqrt(var + GN_EPS) * gg_ref[...] + gb_ref[...]
    o_ref[...] = ((yn + bonus_ref[...]) * g_ref[...]).astype(o_ref.dtype)


def _rwkv_post(yf, yr, bonus, g, gn_g, gn_b, bd):
    N, C = yf.shape
    tm = _pick(N, (1024, 512, 256))
    row = pl.BlockSpec((tm, C), lambda i: (i, 0))
    vec = pl.BlockSpec((1, C), lambda i: (0, 0))
    return pl.pallas_call(
        _rwkv_post_kernel,
        out_shape=jax.ShapeDtypeStruct((N, C), BF16),
        grid=(N // tm,),
        in_specs=[row, row, row, row, vec, vec, pl.BlockSpec((C, C), lambda i: (0, 0))],
        out_specs=row,
        compiler_params=_cparams("arbitrary"),
        name="rwkv_post",
    )(yf, yr, bonus, g, gn_g.reshape(1, C), gn_b.reshape(1, C), bd)


def _fourier_ch_kernel(p_ref, m_ref, o_ref):
    p = p_ref[...]
    mat = m_ref[...]
    G = FOURIER_GROUP_DIM
    for g in range(p.shape[1] // G):
        o_ref[:, g * G:(g + 1) * G] = jnp.dot(p[:, g * G:(g + 1) * G], mat,
                                              preferred_element_type=F32).astype(o_ref.dtype)


def _fourier_channel(pf, ch_mats, B, T):
    N, Fd = pf.shape
    tm = _pick(T, (1024, 512, 256))
    nt = T // tm
    G = FOURIER_GROUP_DIM
    return pl.pallas_call(
        _fourier_ch_kernel,
        out_shape=jax.ShapeDtypeStruct((2, T, B * Fd), BF16),
        grid=(B, nt, 2),
        in_specs=[pl.BlockSpec((tm, Fd), lambda b, i, s: (b * nt + i, 0)),
                  pl.BlockSpec((None, G, G), lambda b, i, s: (s, 0, 0))],
        out_specs=pl.BlockSpec((None, tm, Fd), lambda b, i, s: (s, i, b)),
        compiler_params=_cparams("arbitrary", "arbitrary", "arbitrary"),
        name="fourier_channel",
    )(pf, ch_mats)


def _dft_mats(n):
    idx = jnp.arange(n, dtype=jnp.int32)
    ph = (idx[:, None] * idx[None, :]) % n
    ang = ph.astype(F32) * (2.0 * math.pi / n)
    s = 1.0 / math.sqrt(n)
    return jnp.cos(ang) * s, jnp.sin(ang) * s


def _merge_kernel(fa_ref, at_ref, rw_ref, gt_ref, x_ref, g1_ref, wa_ref, wb_ref, wc_ref, wo_ref, o_ref):
    D = x_ref.shape[1]
    ya = jnp.dot(fa_ref[...], wa_ref[...], preferred_element_type=F32)
    yb = jnp.dot(at_ref[0], wb_ref[0], preferred_element_type=F32)
    for h in range(1, at_ref.shape[0]):
        yb = yb + jnp.dot(at_ref[h], wb_ref[h], preferred_element_type=F32)
    yc = jnp.dot(rw_ref[...], wc_ref[...], preferred_element_type=F32)
    gt = gt_ref[...].astype(F32)
    merged = (_sigmoid(gt[:, :D]) * ya + _sigmoid(gt[:, D:2 * D]) * yb + _sigmoid(gt[:, 2 * D:]) * yc)
    out = jnp.dot(merged.astype(BF16), wo_ref[...], preferred_element_type=F32)
    o_ref[...] = x_ref[...] + g1_ref[...] * out


def _merge(fa, attn, rw, gates, x, g1, wa, wb, wc, wo, B, T):
    N, D = x.shape
    H = attn.shape[1]
    Fd = rw.shape[1]
    tm = _pick(T, (512, 256))
    nt = T // tm
    Bm = g1.shape[0]
    full = lambda shape: pl.BlockSpec(shape, lambda b, i: (0,) * len(shape))
    return pl.pallas_call(
        _merge_kernel,
        out_shape=jax.ShapeDtypeStruct((N, D), F32),
        grid=(B, nt),
        in_specs=[pl.BlockSpec((tm, Fd), lambda b, i: (i, b)),
                  pl.BlockSpec((None, H, tm, LANES), lambda b, i: (b, 0, i, 0)),
                  pl.BlockSpec((tm, Fd), lambda b, i: (b * nt + i, 0)),
                  pl.BlockSpec((tm, 3 * D), lambda b, i: (b * nt + i, 0)),
                  pl.BlockSpec((tm, D), lambda b, i: (b * nt + i, 0)),
                  pl.BlockSpec((None, 1, D), lambda b, i: (b % Bm, 0, 0)),
                  full(wa.shape), full(wb.shape), full(wc.shape), full(wo.shape)],
        out_specs=pl.BlockSpec((tm, D), lambda b, i: (b * nt + i, 0)),
        compiler_params=_cparams("arbitrary", "arbitrary"),
        name="merge_out",
    )(fa, attn, rw, gates, x, g1, wa, wb, wc, wo)


def _route(scores, sel, n_exp):
    tm = scores.shape[0]
    lane = lax.broadcasted_iota(jnp.int32, (tm, LANES), 1)
    neg = -jnp.inf
    sel = jnp.where(lane < n_exp, sel, neg)
    epg = n_exp // N_GROUPS
    assert epg == 4
    odd = (lane % 2) == 1
    p1 = jnp.where(odd, pltpu.roll(sel, 1, axis=1), pltpu.roll(sel, LANES - 1, axis=1))
    hi1, lo1 = jnp.maximum(sel, p1), jnp.minimum(sel, p1)
    upper = (lane % 4) >= 2
    hi2 = jnp.where(upper, pltpu.roll(hi1, 2, axis=1), pltpu.roll(hi1, LANES - 2, axis=1))
    lo2 = jnp.where(upper, pltpu.roll(lo1, 2, axis=1), pltpu.roll(lo1, LANES - 2, axis=1))
    gscore = jnp.maximum(hi1, hi2) + jnp.maximum(jnp.minimum(hi1, hi2), jnp.maximum(lo1, lo2))
    gscore = jnp.where(lane < n_exp, gscore, neg)
    gmax = jnp.max(gscore, axis=1, keepdims=True)
    lane_f = lane.astype(F32)
    grp = (lane // epg).astype(F32)
    big = float(LANES)
    best = jnp.min(jnp.where(gscore == gmax, grp, big), axis=1, keepdims=True)
    masked = jnp.where(grp == best, sel, neg)
    m1 = jnp.max(masked, axis=1, keepdims=True)
    i1 = jnp.min(jnp.where(masked == m1, lane_f, big), axis=1, keepdims=True)
    masked2 = jnp.where(lane_f == i1, neg, masked)
    m2 = jnp.max(masked2, axis=1, keepdims=True)
    i2 = jnp.min(jnp.where(masked2 == m2, lane_f, big), axis=1, keepdims=True)
    pick1 = lane_f == i1
    pick2 = lane_f == i2
    s1 = jnp.sum(jnp.where(pick1, scores, 0.0), axis=1, keepdims=True)
    s2 = jnp.sum(jnp.where(pick2, scores, 0.0), axis=1, keepdims=True)
    return (jnp.where(pick1, s1, 0.0) + jnp.where(pick2, s2, 0.0)) / (s1 + s2)


def _moe_kernel(x_ref, g_ref, sc_ref, sh_ref, g2_ref, rw_ref, rb_ref,
                wg_ref, wu_ref, wd_ref, swg_ref, swu_ref, swd_ref, o_ref,
                m_ref, comb_ref, acc_ref, *, n_exp):
    e = pl.program_id(1)

    @pl.when(e == 0)
    def _():
        x = x_ref[...]
        m = x * lax.rsqrt(jnp.mean(x * x, axis=-1, keepdims=True) + NORM_EPS) * g_ref[...]
        m = m * (1.0 + sc_ref[...]) + sh_ref[...]
        m_ref[...] = m.astype(BF16)
        scores = _sigmoid(_dot_split(m, rw_ref[...]))
        comb_ref[...] = _route(scores, scores + rb_ref[...], n_exp)
        acc_ref[...] = jnp.zeros_like(acc_ref)

    m = m_ref[...]

    @pl.when(e < n_exp)
    def _():
        lane = lax.broadcasted_iota(jnp.int32, comb_ref.shape, 1)
        ce = jnp.sum(jnp.where(lane == e, comb_ref[...], 0.0), axis=1, keepdims=True)
        hg = jnp.dot(m, wg_ref[...].astype(BF16), preferred_element_type=F32)
        hu = jnp.dot(m, wu_ref[...].astype(BF16), preferred_element_type=F32)
        hh = hg * _sigmoid(hg) * hu * ce
        acc_ref[...] += jnp.dot(hh.astype(BF16), wd_ref[...].astype(BF16), preferred_element_type=F32)

    @pl.when(e == n_exp)
    def _():
        hg = jnp.dot(m, swg_ref[...], preferred_element_type=F32)
        hu = jnp.dot(m, swu_ref[...], preferred_element_type=F32)
        hh = hg * _sigmoid(hg) * hu
        y = acc_ref[...] + jnp.dot(hh.astype(BF16), swd_ref[...], preferred_element_type=F32)
        o_ref[...] = x_ref[...] + g2_ref[...] * y


def _moe(x, g, sc, sh, g2, rw_pad, rb_pad, wg, wu, wd, swg, swu, swd):
    N, D = x.shape
    E, _, Ff = wg.shape
    Bm = sc.shape[0]
    rpm = N // Bm
    tm = min(rpm, 1024)
    assert rpm % tm == 0
    mod = pl.BlockSpec((None, 1, D), lambda i, e: (i * tm // rpm, 0, 0))
    full = lambda shape: pl.BlockSpec(shape, lambda i, e: (0,) * len(shape))
    return pl.pallas_call(
        functools.partial(_moe_kernel, n_exp=E),
        out_shape=jax.ShapeDtypeStruct((N, D), F32),
        grid=(N // tm, E + 1),
        in_specs=[pl.BlockSpec((tm, D), lambda i, e: (i, 0)),
                  full((1, D)), mod, mod, mod,
                  full(rw_pad.shape), full(rb_pad.shape),
                  pl.BlockSpec((None, D, Ff), lambda i, e: (jnp.minimum(e, E - 1), 0, 0)),
                  pl.BlockSpec((None, D, Ff), lambda i, e: (jnp.minimum(e, E - 1), 0, 0)),
                  pl.BlockSpec((None, Ff, D), lambda i, e: (jnp.minimum(e, E - 1), 0, 0)),
                  full(swg.shape), full(swu.shape), full(swd.shape)],
        out_specs=pl.BlockSpec((tm, D), lambda i, e: (i, 0)),
        scratch_shapes=[pltpu.VMEM((tm, D), BF16), pltpu.VMEM((tm, LANES), F32),
                        pltpu.VMEM((tm, D), F32)],
        compiler_params=_cparams("arbitrary", "arbitrary"),
        name="moe",
    )(x, g.reshape(1, D), sc, sh, g2, rw_pad, rb_pad, wg, wu, wd, swg, swu, swd)


def _rope_partner(w):
    lead = w.shape[:-1]
    wr = w.reshape(lead + (2, 2, ROPE_PAIRS))
    return jnp.stack([-wr[..., 1, :], wr[..., 0, :]], axis=-2).reshape(w.shape)


def _rope_tables(T):
    rows = T // GRID_W
    row = jnp.repeat(jnp.arange(rows), GRID_W)
    col = jnp.tile(jnp.arange(GRID_W), rows)
    inv = ROPE_BASE ** (-jnp.arange(ROPE_PAIRS, dtype=F32) / ROPE_PAIRS)
    ang = jnp.stack([row, col], axis=-1).astype(F32)[..., None] * inv
    cos = jnp.broadcast_to(jnp.cos(ang)[:, :, None, :], (T, 2, 2, ROPE_PAIRS)).reshape(T, MLA_ROPE)
    sin = jnp.broadcast_to(jnp.sin(ang)[:, :, None, :], (T, 2, 2, ROPE_PAIRS)).reshape(T, MLA_ROPE)
    pad = LANES - MLA_QK
    z_n, z_p = jnp.zeros((T, MLA_NOPE), F32), jnp.zeros((T, pad), F32)
    cq = jnp.concatenate([jnp.ones((T, MLA_NOPE), F32), cos, z_p], axis=1)
    sq = jnp.concatenate([z_n, sin, z_p], axis=1)
    ck = jnp.concatenate([z_n, cos, z_p], axis=1)
    return cq, sq, ck, sq


def _identity_tables(T):
    pad = LANES - MLA_QK
    z_n, z_p = jnp.zeros((T, MLA_NOPE), F32), jnp.zeros((T, pad), F32)
    one_r = jnp.ones((T, MLA_ROPE), F32)
    cq = jnp.concatenate([jnp.ones((T, MLA_NOPE), F32), one_r, z_p], axis=1)
    ck = jnp.concatenate([z_n, one_r, z_p], axis=1)
    zero = jnp.zeros((T, LANES), F32)
    return cq, zero, ck, zero


def kernel(x, c, ctx, c_ctx, w_ada, b_ada, norm1_g, norm2_g, w_in, q_norm_g, kv_norm_g, w_uq, w_ukv, rwkv_mu, rwkv_w0, rwkv_w_up, rwkv_a0, rwkv_a_up, rwkv_g_up, rwkv_k_k, rwkv_k_a, rwkv_r_k, rwkv_gn_g, rwkv_gn_b, w_proj_a, w_proj_b, w_proj_c, w_out, router_w, router_b, exp_w_gate, exp_w_up, exp_w_down, sh_w_gate, sh_w_up, sh_w_down, final_norm_g):
    B, T, D = x.shape
    Tc = ctx.shape[1]
    L = w_ada.shape[0]
    H = MLA_HEADS
    QL, KVL = q_norm_g.shape[1], kv_norm_g.shape[1]
    Fd = w_proj_a.shape[1]
    C = RWKV_DIM
    E = router_w.shape[1]
    N, Nc = B * T, B * Tc

    sizes = (Fd, QL, KVL, MLA_ROPE, 3 * C + DECAY_LORA + AAA_LORA + GATE_LORA, D, D, D)
    offs = np.concatenate([[0], np.cumsum(sizes)])

    lat_tabs = _rope_tables(T)
    ctx_tabs = _identity_tables(Tc)
    cc, cs = _dft_mats(FOURIER_GROUP_DIM)
    ch_mats = jnp.stack([cc, cs]).astype(BF16)
    tcos, tsin = _dft_mats(T)
    dft_lat = jnp.concatenate([tcos, -tsin], axis=1).astype(BF16)
    ccos, csin = _dft_mats(Tc)
    dft_ctx = jnp.concatenate([ccos, -csin], axis=1).astype(BF16)
    hid = jnp.arange(C) // RWKV_HEAD
    bd = (hid[:, None] == hid[None, :]).astype(BF16)
    rw_pad = jnp.zeros((D, LANES), F32).at[:, :E].set(router_w)
    rb_pad = jnp.zeros((1, LANES), F32).at[0, :E].set(router_b)

    R = -(-(B + 1) // 8) * 8
    c_rows = jnp.zeros((R, D), F32).at[:B].set(c).at[B].set(c_ctx)
    mod = _ada_mod(c_rows, w_ada, b_ada)

    x = x.reshape(N, D)
    ctx = ctx.reshape(Nc, D)
    zero_state = jnp.zeros((2, B, RWKV_HEADS, RWKV_HEAD, RWKV_HEAD), F32)

    for l in range(L):
        last = l == L - 1
        ml = mod[l, :B].reshape(B, 1, 6, D)
        mc = mod[l, B:B + 1].reshape(1, 1, 6, D)
        sh1, sc1, g1, sh2, sc2, g2 = [ml[:, :, j] for j in range(6)]
        csh1, csc1, cg1, csh2, csc2, cg2 = [mc[:, :, j] for j in range(6)]

        wi = w_in[l]
        w_f = wi[:, offs[0]:offs[1]].astype(BF16)
        w_kr = wi[:, offs[3]:offs[4]]
        zl = jnp.zeros((D, MLA_NOPE), F32)
        zr = jnp.zeros((D, LANES - MLA_QK), F32)
        w_qkv = jnp.concatenate([wi[:, offs[1]:offs[3]], zl, w_kr, zr, zl, _rope_partner(w_kr), zr],
                                axis=1).astype(BF16)
        w_rw = wi[:, offs[4]:offs[5]].astype(BF16)
        w_gt = wi[:, offs[5]:offs[8]].astype(BF16)
        uq = w_uq[l].reshape(QL, H, MLA_QK).transpose(1, 0, 2)
        padq = jnp.zeros((H, QL, LANES - MLA_QK), F32)
        wqa = jnp.concatenate([uq, padq], axis=2).astype(BF16)
        wqb = jnp.concatenate([jnp.zeros((H, QL, MLA_NOPE), F32), _rope_partner(uq[..., MLA_NOPE:]), padq],
                              axis=2).astype(BF16)
        ukv = w_ukv[l].reshape(KVL, H, MLA_NOPE + MLA_V).transpose(1, 0, 2)
        wk = jnp.concatenate([ukv[..., :MLA_NOPE], jnp.zeros((H, KVL, LANES - MLA_NOPE), F32)], axis=2).astype(BF16)
        wv = jnp.concatenate([ukv[..., MLA_NOPE:], jnp.zeros((H, KVL, LANES - MLA_V), F32)], axis=2).astype(BF16)
        wb = jnp.concatenate([w_proj_b[l].reshape(H, MLA_V, D), jnp.zeros((H, LANES - MLA_V, D), F32)],
                             axis=1).astype(BF16)
        wa = w_proj_a[l].astype(BF16)
        wc = w_proj_c[l].astype(BF16)
        wo = w_out[l].astype(BF16)
        zd = jnp.zeros((2, DECAY_LORA, C), F32)
        prm = dict(
            mu=rwkv_mu[l].reshape(1, -1), w0=rwkv_w0[l], a0=rwkv_a0[l],
            wup=jnp.concatenate([rwkv_w_up[l], zd], axis=1).astype(BF16),
            aup=jnp.concatenate([zd, rwkv_a_up[l]], axis=1).astype(BF16),
            gup=rwkv_g_up[l].astype(BF16), k_k=rwkv_k_k[l].reshape(1, C), k_a=rwkv_k_a[l].reshape(1, C),
            r_k=rwkv_r_k[l].reshape(1, C), bd=bd)

        def mixers(xx, sc, sh, tabs, Tx):
            hcur = _norm(xx, norm1_g[l], sc, sh, BF16)
            pf = _mm(hcur, w_f, BF16)
            pqkv = _mm(hcur, w_qkv, BF16)
            pr = _mm(hcur, w_rw, BF16)
            gates = _mm(hcur, w_gt, BF16)
            q, k, v = _qkv_project(pqkv, q_norm_g[l], kv_norm_g[l], wqa, wqb, wk, wv, tabs, B, Tx)
            rp = _rwkv_prep(pr, prm, B, Tx)
            return pf, gates, q, k, v, rp

        pf_l, gates_l, q_l, k_l, v_l, rp_l = mixers(x, sc1, sh1, lat_tabs, T)
        pf_c, gates_c, q_c, k_c, v_c, rp_c = mixers(ctx, csc1, csh1, ctx_tabs, Tc)

        attn_l = _attention(q_l, jnp.concatenate([k_l, k_c], axis=2), jnp.concatenate([v_l, v_c], axis=2))

        yf_c, yr_c, s_ctx = _rwkv_scan(*rp_c[:6], prm["k_a"], zero_state, B, Tc)
        yf_l, yr_l, _ = _rwkv_scan(*rp_l[:6], prm["k_a"], s_ctx, B, T)
        rwo_l = _rwkv_post(yf_l, yr_l, rp_l[7], rp_l[6], rwkv_gn_g[l], rwkv_gn_b[l], bd)

        zcs_l = _fourier_channel(pf_l, ch_mats, B, T).reshape(2 * T, B * Fd)
        fa_l = _mm(dft_lat, zcs_l, BF16, tn=B * Fd, tk=min(2 * T, 1024))

        x = _merge(fa_l, attn_l, rwo_l, gates_l, x, g1, wa, wb, wc, wo, B, T)
        moe_w = (rw_pad, rb_pad, exp_w_gate[l], exp_w_up[l], exp_w_down[l],
                 sh_w_gate[l].astype(BF16), sh_w_up[l].astype(BF16), sh_w_down[l].astype(BF16))
        x = _moe(x, norm2_g[l], sc2, sh2, g2, *moe_w)

        if not last:
            attn_c = _attention(q_c, k_c, v_c)
            rwo_c = _rwkv_post(yf_c, yr_c, rp_c[7], rp_c[6], rwkv_gn_g[l], rwkv_gn_b[l], bd)
            zcs_c = _fourier_channel(pf_c, ch_mats, B, Tc).reshape(2 * Tc, B * Fd)
            fa_c = _mm(dft_ctx, zcs_c, BF16, tn=B * Fd, tk=min(2 * Tc, 1024))
            ctx = _merge(fa_c, attn_c, rwo_c, gates_c, ctx, cg1, wa, wb, wc, wo, B, Tc)
            ctx = _moe(ctx, norm2_g[l], csc2, csh2, cg2, *moe_w)

    ones = jnp.ones((1, 1, D), F32)
    out = _norm(x, final_norm_g, ones, ones, F32, modulate=False)
    return out.reshape(B, T, D)
```

```python
import functools
import math

import numpy as np
import jax
import jax.numpy as jnp
from jax import lax
from jax.experimental import pallas as pl
from jax.experimental.pallas import tpu as pltpu

F32 = jnp.float32
BF16 = jnp.bfloat16

GRID_W = 64
FOURIER_GROUP_DIM = 128
MLA_HEADS = 8
MLA_NOPE = 64
MLA_ROPE = 32
MLA_QK = MLA_NOPE + MLA_ROPE
MLA_V = 64
MLA_SCALE = MLA_QK ** -0.5
QK_PRESCALE = MLA_SCALE * math.log2(math.e)
ROPE_PAIRS = MLA_ROPE // 4
ROPE_BASE = 10000.0
RWKV_HEADS = 8
RWKV_HEAD = 64
RWKV_DIM = RWKV_HEADS * RWKV_HEAD
DECAY_LORA = 64
AAA_LORA = 64
GATE_LORA = 128
GN_EPS = 64e-5
N_GROUPS = 4
NORM_EPS = 1e-6

LANES = 128
VMEM_LIMIT_BYTES = 48 * 1024 * 1024
RWKV_CHUNK = 64


def _cparams(*sem):
    return pltpu.CompilerParams(dimension_semantics=sem, vmem_limit_bytes=VMEM_LIMIT_BYTES)


def _sigmoid(z):
    return 1.0 / (1.0 + jnp.exp(-z))


def _bdot(a, b):
    return jnp.dot(a.astype(BF16), b.astype(BF16), preferred_element_type=F32)


def _dot_nt(a, b):
    return lax.dot_general(a.astype(BF16), b.astype(BF16), (((1,), (1,)), ((), ())),
                           preferred_element_type=F32)


def _dot_tn(a, b):
    return lax.dot_general(a.astype(BF16), b.astype(BF16), (((0,), (0,)), ((), ())),
                           preferred_element_type=F32)


def _split2(x):
    hi = x.astype(BF16)
    lo = (x - hi.astype(F32)).astype(BF16)
    return hi, lo


def _dot_exact_rhs(x, ones_bf16):
    hi, lo = _split2(x)
    return (jnp.dot(hi, ones_bf16, preferred_element_type=F32)
            + jnp.dot(lo, ones_bf16, preferred_element_type=F32))


def _dot_split(a, b):
    ah, al = _split2(a)
    bh, bl = _split2(b)
    return (jnp.dot(ah, bh, preferred_element_type=F32)
            + jnp.dot(ah, bl, preferred_element_type=F32)
            + jnp.dot(al, bh, preferred_element_type=F32))


def _ada_kernel(c_ref, w_ref, b_ref, o_ref):
    c = c_ref[...]
    s = c * _sigmoid(c)
    o_ref[...] = _dot_split(s, w_ref[...]) + b_ref[...]


def _ada_mod(c_rows, w_ada, b_ada):
    L, D, D6 = w_ada.shape
    R = c_rows.shape[0]
    tn = 1536 if D6 % 1536 == 0 else D6
    return pl.pallas_call(
        _ada_kernel,
        out_shape=jax.ShapeDtypeStruct((L, R, D6), F32),
        grid=(L, D6 // tn),
        in_specs=[pl.BlockSpec((R, D), lambda l, j: (0, 0)),
                  pl.BlockSpec((None, D, tn), lambda l, j: (l, 0, j)),
                  pl.BlockSpec((None, 1, tn), lambda l, j: (l, 0, j))],
        out_specs=pl.BlockSpec((None, R, tn), lambda l, j: (l, 0, j)),
        compiler_params=_cparams("arbitrary", "arbitrary"),
        name="ada_mod",
    )(c_rows, w_ada, b_ada.reshape(L, 1, D6))


def _norm_kernel(x_ref, g_ref, sc_ref, sh_ref, o_ref, *, modulate):
    x = x_ref[...].astype(F32)
    y = x * lax.rsqrt(jnp.mean(x * x, axis=-1, keepdims=True) + NORM_EPS) * g_ref[...]
    if modulate:
        y = y * (1.0 + sc_ref[...]) + sh_ref[...]
    o_ref[...] = y.astype(o_ref.dtype)


def _norm(x, g, sc, sh, out_dtype, modulate=True):
    N, D = x.shape
    Bm = sc.shape[0]
    rpm = N // Bm
    tm = min(rpm, 1024)
    assert rpm % tm == 0
    return pl.pallas_call(
        functools.partial(_norm_kernel, modulate=modulate),
        out_shape=jax.ShapeDtypeStruct((N, D), out_dtype),
        grid=(N // tm,),
        in_specs=[pl.BlockSpec((tm, D), lambda i: (i, 0)),
                  pl.BlockSpec((1, D), lambda i: (0, 0)),
                  pl.BlockSpec((None, 1, D), lambda i: (i * tm // rpm, 0, 0)),
                  pl.BlockSpec((None, 1, D), lambda i: (i * tm // rpm, 0, 0))],
        out_specs=pl.BlockSpec((tm, D), lambda i: (i, 0)),
        compiler_params=_cparams("arbitrary"),
        name="rmsnorm_mod",
    )(x, g.reshape(1, D), sc, sh)


def _in_proj_kernel(x_ref, g_ref, sc_ref, sh_ref, w_ref, o_ref, h_ref):
    @pl.when(pl.program_id(1) == 0)
    def _():
        x = x_ref[...]
        y = x * lax.rsqrt(jnp.mean(x * x, axis=-1, keepdims=True) + NORM_EPS) * g_ref[...]
        h_ref[...] = (y * (1.0 + sc_ref[...]) + sh_ref[...]).astype(BF16)

    o_ref[...] = jnp.dot(h_ref[...], w_ref[...], preferred_element_type=F32).astype(o_ref.dtype)


def _in_proj(x, g, sc, sh, w_all):
    N, D = x.shape
    Nout = w_all.shape[1]
    Bm = sc.shape[0]
    rpm = N // Bm
    tm = min(rpm, 1024)
    assert rpm % tm == 0
    tn = _pick(Nout, (896, 512, 256, 128))
    mod = pl.BlockSpec((None, 1, D), lambda i, j: (i * tm // rpm, 0, 0))
    return pl.pallas_call(
        _in_proj_kernel,
        out_shape=jax.ShapeDtypeStruct((N, Nout), BF16),
        grid=(N // tm, Nout // tn),
        in_specs=[pl.BlockSpec((tm, D), lambda i, j: (i, 0)),
                  pl.BlockSpec((1, D), lambda i, j: (0, 0)), mod, mod,
                  pl.BlockSpec((D, tn), lambda i, j: (0, j))],
        out_specs=pl.BlockSpec((tm, tn), lambda i, j: (i, j)),
        scratch_shapes=[pltpu.VMEM((tm, D), BF16)],
        compiler_params=_cparams("arbitrary", "arbitrary"),
        name="in_proj",
    )(x, g.reshape(1, D), sc, sh, w_all)


def _column_layout(widths):
    offs, cur = [], 0
    for w in widths:
        cur = -(-cur // w) * w
        offs.append(cur)
        cur += w
    return offs, -(-cur // LANES) * LANES


def _mm_kernel(a_ref, b_ref, o_ref):
    o_ref[...] = jnp.dot(a_ref[...], b_ref[...], preferred_element_type=F32).astype(o_ref.dtype)


def _mm_acc_kernel(a_ref, b_ref, o_ref, acc_ref, *, nk):
    k = pl.program_id(2)

    @pl.when(k == 0)
    def _():
        acc_ref[...] = jnp.zeros_like(acc_ref)

    acc_ref[...] += jnp.dot(a_ref[...], b_ref[...], preferred_element_type=F32)

    @pl.when(k == nk - 1)
    def _():
        o_ref[...] = acc_ref[...].astype(o_ref.dtype)


def _pick(n, prefs):
    for p in prefs:
        if n % p == 0:
            return p
    return n


def _mm(a, w, out_dtype, tn=None, tk=None):
    M, K = a.shape
    Nn = w.shape[1]
    tm = _pick(M, (1024, 512, 256))
    tn = tn or _pick(Nn, (1024, 896, 512))
    tk = tk or K
    if tk == K:
        return pl.pallas_call(
            _mm_kernel,
            out_shape=jax.ShapeDtypeStruct((M, Nn), out_dtype),
            grid=(M // tm, Nn // tn),
            in_specs=[pl.BlockSpec((tm, K), lambda i, j: (i, 0)),
                      pl.BlockSpec((K, tn), lambda i, j: (0, j))],
            out_specs=pl.BlockSpec((tm, tn), lambda i, j: (i, j)),
            compiler_params=_cparams("arbitrary", "arbitrary"),
            name="matmul",
        )(a, w)
    nk = K // tk
    return pl.pallas_call(
        functools.partial(_mm_acc_kernel, nk=nk),
        out_shape=jax.ShapeDtypeStruct((M, Nn), out_dtype),
        grid=(M // tm, Nn // tn, nk),
        in_specs=[pl.BlockSpec((tm, tk), lambda i, j, k: (i, k)),
                  pl.BlockSpec((tk, tn), lambda i, j, k: (k, j))],
        out_specs=pl.BlockSpec((tm, tn), lambda i, j, k: (i, j)),
        scratch_shapes=[pltpu.VMEM((tm, tn), F32)],
        compiler_params=_cparams("arbitrary", "arbitrary", "arbitrary"),
        name="matmul_kacc",
    )(a, w)


def _qkv_kernel(p_ref, gq_ref, gkv_ref, wqa_ref, wqb_ref, wk_ref, wv_ref,
                cq_ref, sq_ref, ck_ref, sk_ref, q_ref, k_ref, v_ref, *, ql, kvl):
    p = p_ref[...].astype(F32)
    pq = p[:, :ql]
    pkv = p[:, ql:ql + kvl]
    kra = p[:, ql + kvl:ql + kvl + LANES]
    krb = p[:, ql + kvl + LANES:ql + kvl + 2 * LANES]
    qn = (pq * lax.rsqrt(jnp.mean(pq * pq, axis=-1, keepdims=True) + NORM_EPS) * gq_ref[...]).astype(BF16)
    kvn = (pkv * lax.rsqrt(jnp.mean(pkv * pkv, axis=-1, keepdims=True) + NORM_EPS) * gkv_ref[...]).astype(BF16)
    kr = kra * ck_ref[...] + krb * sk_ref[...]
    cq = cq_ref[...] * QK_PRESCALE
    sq = sq_ref[...] * QK_PRESCALE
    for h in range(q_ref.shape[0]):
        qa = jnp.dot(qn, wqa_ref[h], preferred_element_type=F32)
        qb = jnp.dot(qn, wqb_ref[h], preferred_element_type=F32)
        q_ref[h] = (qa * cq + qb * sq).astype(q_ref.dtype)
        k_ref[h] = (jnp.dot(kvn, wk_ref[h], preferred_element_type=F32) + kr).astype(k_ref.dtype)
        v_ref[h] = jnp.dot(kvn, wv_ref[h], preferred_element_type=F32).astype(v_ref.dtype)


def _qkv_project(proj, col, gq, gkv, wqa, wqb, wk, wv, tabs, B, T):
    N = proj.shape[0]
    H = wqa.shape[0]
    ql, kvl = wqa.shape[1], wk.shape[1]
    W = ql + kvl + 2 * LANES
    tm = _pick(T, (512, 256))
    nt = T // tm
    cq, sq, ck, sk = tabs
    full = lambda shape: pl.BlockSpec(shape, lambda i: (0,) * len(shape))
    tab_spec = pl.BlockSpec((tm, LANES), lambda i: (i % nt, 0))
    out_spec = pl.BlockSpec((None, H, tm, LANES), lambda i: (i // nt, 0, i % nt, 0))
    out_sds = jax.ShapeDtypeStruct((B, H, T, LANES), BF16)
    return pl.pallas_call(
        functools.partial(_qkv_kernel, ql=ql, kvl=kvl),
        out_shape=(out_sds, out_sds, out_sds),
        grid=(N // tm,),
        in_specs=[pl.BlockSpec((tm, W), lambda i: (i, col)),
                  full((1, ql)), full((1, kvl)),
                  full(wqa.shape), full(wqb.shape), full(wk.shape), full(wv.shape),
                  tab_spec, tab_spec, tab_spec, tab_spec],
        out_specs=(out_spec, out_spec, out_spec),
        compiler_params=_cparams("arbitrary"),
        name="mla_qkv",
    )(proj, gq.reshape(1, ql), gkv.reshape(1, kvl), wqa, wqb, wk, wv, cq, sq, ck, sk)


def _attn_kernel(q_ref, *refs, tiles):
    o_ref = refs[-1]
    q = q_ref[...]
    m = l = acc = None
    for src, start, size in tiles:
        k = refs[2 * src][start:start + size, :]
        v = refs[2 * src + 1][start:start + size, :]
        s = _dot_nt(q, k)
        mt = jnp.max(s, axis=-1, keepdims=True)
        if m is None:
            m_new = mt
            p = jnp.exp2(s - m_new)
            l = jnp.sum(p, axis=-1, keepdims=True)
            acc = jnp.dot(p.astype(BF16), v, preferred_element_type=F32)
        else:
            m_new = jnp.maximum(m, mt)
            alpha = jnp.exp2(m - m_new)
            p = jnp.exp2(s - m_new)
            l = alpha * l + jnp.sum(p, axis=-1, keepdims=True)
            acc = alpha * acc + jnp.dot(p.astype(BF16), v, preferred_element_type=F32)
        m = m_new
    o_ref[...] = (acc / l).astype(o_ref.dtype)


def _attention(q, kvs):
    B, H, T, dq = q.shape
    tq = _pick(T, (512, 256))
    tiles = []
    for src, (k, _) in enumerate(kvs):
        Tk = k.shape[2]
        tk = _pick(Tk, (1024, 512, 256))
        tiles += [(src, st, tk) for st in range(0, Tk, tk)]
    in_specs = [pl.BlockSpec((None, None, tq, dq), lambda b, h, i: (b, h, i, 0))]
    args = [q]
    for k, v in kvs:
        Tk = k.shape[2]
        in_specs += [pl.BlockSpec((None, None, Tk, dq), lambda b, h, i: (b, h, 0, 0)),
                     pl.BlockSpec((None, None, Tk, LANES), lambda b, h, i: (b, h, 0, 0))]
        args += [k, v]
    return pl.pallas_call(
        functools.partial(_attn_kernel, tiles=tuple(tiles)),
        out_shape=jax.ShapeDtypeStruct((B, H, T, LANES), BF16),
        grid=(B, H, T // tq),
        in_specs=in_specs,
        out_specs=pl.BlockSpec((None, None, tq, LANES), lambda b, h, i: (b, h, i, 0)),
        compiler_params=_cparams("arbitrary", "arbitrary", "arbitrary"),
        name="attention",
    )(*args)


def _softplus(u):
    return jnp.maximum(u, 0.0) + jnp.log(1.0 + jnp.exp(-jnp.abs(u)))


def _rwkv_prep_kernel(p_ref, pp_ref, pn_ref, mu_ref, w0_ref, wup_ref, a0_ref, aup_ref, gup_ref,
                      kk_ref, ka_ref, rk_ref, bd_ref,
                      r_o, k_o, v_o, kn_o, lw_o, a_o, g_o, bonus_o, *, tps):
    i = pl.program_id(0)
    p = p_ref[...].astype(F32)
    tt = p.shape[0]
    C = RWKV_DIM
    halo = pp_ref.shape[0]
    prev_row = jnp.where(i % tps == 0, 0.0, pp_ref[...].astype(F32)[halo - 1:halo, :])
    next_row = jnp.where(i % tps == tps - 1, 0.0, pn_ref[...].astype(F32)[0:1, :])
    rows = lax.broadcasted_iota(jnp.int32, (tt, 1), 0)
    pprev = jnp.where(rows == 0, prev_row, pltpu.roll(p, 1, axis=0))
    pnext = jnp.where(rows == tt - 1, next_row, pltpu.roll(p, tt - 1, axis=0))
    xs = p + (0.5 * (pprev + pnext) - p) * mu_ref[...]
    r = xs[:, :C]
    k = xs[:, C:2 * C]
    v = xs[:, 2 * C:3 * C]
    xwa = xs[:, 3 * C:3 * C + DECAY_LORA + AAA_LORA]
    xg = xs[:, 3 * C + DECAY_LORA + AAA_LORA:]
    bd = bd_ref[...]
    g_o[...] = _bdot(_sigmoid(xg), gup_ref[...])
    kn = k * kk_ref[...]
    kn = kn * lax.rsqrt(_dot_exact_rhs(kn * kn, bd) + 1e-12)
    th = jnp.tanh(xwa)
    bonus = jnp.zeros_like(r)
    for d in range(2):
        w = -_softplus(-(w0_ref[d:d + 1, :] + _bdot(th, wup_ref[d]))) - 0.5
        lw_o[d] = -jnp.exp(w)
        a = _sigmoid(a0_ref[d:d + 1, :] + _bdot(xwa, aup_ref[d]))
        a_o[d] = a
        kd = k * (1.0 + (a - 1.0) * ka_ref[...])
        bonus = bonus + _dot_exact_rhs(r * kd * rk_ref[...], bd) * v
    r_o[...] = r
    k_o[...] = k
    v_o[...] = v
    kn_o[...] = kn
    bonus_o[...] = bonus


def _rwkv_prep(proj, col, prm, B, T):
    N = proj.shape[0]
    W = prm["mu"].shape[1]
    C = RWKV_DIM
    tt = _pick(T, (256,))
    tps = T // tt
    halo = 16
    nh = N // halo
    full = lambda shape: pl.BlockSpec(shape, lambda i: (0,) * len(shape))
    o_spec = pl.BlockSpec((tt, C), lambda i: (i, 0))
    o2_spec = pl.BlockSpec((2, tt, C), lambda i: (0, i, 0))
    sds = jax.ShapeDtypeStruct((N, C), F32)
    sds2 = jax.ShapeDtypeStruct((2, N, C), F32)
    return pl.pallas_call(
        functools.partial(_rwkv_prep_kernel, tps=tps),
        out_shape=(sds, sds, sds, sds, sds2, sds2, sds, sds),
        grid=(N // tt,),
        in_specs=[pl.BlockSpec((tt, W), lambda i: (i, col)),
                  pl.BlockSpec((halo, W), lambda i: (jnp.maximum(i * (tt // halo) - 1, 0), col)),
                  pl.BlockSpec((halo, W), lambda i: (jnp.minimum((i + 1) * (tt // halo), nh - 1), col)),
                  full((1, W)), full((2, C)), full((2, LANES, C)), full((2, C)), full((2, LANES, C)),
                  full((GATE_LORA, C)), full((1, C)), full((1, C)), full((1, C)), full((C, C))],
        out_specs=(o_spec, o_spec, o_spec, o_spec, o2_spec, o2_spec, o_spec, o_spec),
        compiler_params=_cparams("arbitrary"),
        name="rwkv_prep",
    )(proj, proj, proj, prm["mu"], prm["w0"], prm["wup"], prm["a0"], prm["aup"], prm["gup"],
      prm["k_k"], prm["k_a"], prm["r_k"], prm["bd"])


def _bmm(a, b):
    return lax.dot_general(a.astype(BF16), b.astype(BF16), (((2,), (1,)), ((0,), (0,))),
                           preferred_element_type=F32)


def _bmm_nt(a, b):
    return lax.dot_general(a.astype(BF16), b.astype(BF16), (((2,), (2,)), ((0,), (0,))),
                           preferred_element_type=F32)


def _bmm_tn(a, b):
    return lax.dot_general(a.astype(BF16), b.astype(BF16), (((1,), (1,)), ((0,), (0,))),
                           preferred_element_type=F32)


def _rwkv_scan_kernel(rf_ref, kf_ref, vf_ref, knf_ref, lwf_ref, af_ref,
                      rr_ref, kr_ref, vr_ref, knr_ref, lwr_ref, ar_ref, ka_ref, s0_ref,
                      yf_ref, yr_ref, sf_ref, S_ref, *, G):
    j = pl.program_id(1)

    @pl.when(j == 0)
    def _():
        S_ref[...] = s0_ref[...]

    Cc, N, H = RWKV_CHUNK, RWKV_HEAD, RWKV_HEADS
    R = G * Cc
    ti = lax.broadcasted_iota(jnp.int32, (R, R), 0)
    ii = lax.broadcasted_iota(jnp.int32, (R, R), 1)
    same = (ti // Cc) == (ii // Cc)
    m_all = same.astype(F32).astype(BF16)

    def heads(x):
        return jnp.stack([x[g * Cc:(g + 1) * Cc, h * N:(h + 1) * N] for g in range(G) for h in range(H)])

    per_dir = []
    for d, refs in enumerate(((rf_ref, kf_ref, vf_ref, knf_ref, lwf_ref, af_ref),
                              (rr_ref, kr_ref, vr_ref, knr_ref, lwr_ref, ar_ref))):
        r, k, v, kn, lw, a = [x[...] for x in refs]
        before = (ii <= ti) if d == 0 else (ii >= ti)
        m_incl = (same & before).astype(F32).astype(BF16)
        h1 = lw.astype(BF16)
        r1 = lw - h1.astype(F32)
        h2 = r1.astype(BF16)
        h3 = (r1 - h2.astype(F32)).astype(BF16)
        cum = (jnp.dot(m_incl, h1, preferred_element_type=F32)
               + jnp.dot(m_incl, h2, preferred_element_type=F32)
               + jnp.dot(m_incl, h3, preferred_element_type=F32))
        tot = (jnp.dot(m_all, h1, preferred_element_type=F32)
               + jnp.dot(m_all, h2, preferred_element_type=F32)
               + jnp.dot(m_all, h3, preferred_element_type=F32))
        e_neg = jnp.exp(-cum)
        e_rem = jnp.exp(tot - cum)
        b = kn * a
        kd = k * (1.0 + (a - 1.0) * ka_ref[...])
        per_dir.append([heads(x) for x in (-kn * jnp.exp(cum - lw), r * jnp.exp(cum), b * e_neg, kd * e_neg,
                                           b * e_rem, kd * e_rem, v, jnp.exp(tot))])

    A, Rt, Bt, Kt, Bh, Kh, V, Wc = [jnp.concatenate([per_dir[0][i], per_dir[1][i]], axis=0) for i in range(8)]
    nb = 2 * G * H
    bi = lax.broadcasted_iota(jnp.int32, (nb, Cc, Cc), 0)
    t3 = lax.broadcasted_iota(jnp.int32, (nb, Cc, Cc), 1)
    i3 = lax.broadcasted_iota(jnp.int32, (nb, Cc, Cc), 2)
    order = jnp.where(bi < G * H, t3 - i3, i3 - t3)
    strict = order > 0
    incl = order >= 0

    AR = jnp.concatenate([A, Rt], axis=1)
    ab = _bmm_nt(AR, Bt)
    ak = _bmm_nt(AR, Kt)
    Lab = jnp.where(strict, ab[:, :Cc], 0.0)
    Mrb = jnp.where(incl, ab[:, Cc:], 0.0)
    Lak = jnp.where(strict, ak[:, :Cc], 0.0)
    Mrk = jnp.where(incl, ak[:, Cc:], 0.0)
    lv = _bmm(jnp.concatenate([Lak, Mrk], axis=1), V)
    Z = jnp.concatenate([A, lv[:, :Cc]], axis=2)
    P = Lab
    steps = int(math.log2(Cc))
    for s in range(steps):
        if s < steps - 1:
            out = _bmm(P, jnp.concatenate([Z, P], axis=2))
            Z = Z + out[:, :, :2 * N]
            P = out[:, :, 2 * N:]
        else:
            Z = Z + _bmm(P, Z)
    mz = _bmm(Mrb, Z)
    Rh = Rt + mz[:, :, :N]
    Yh = mz[:, :, N:] + lv[:, Cc:]
    gh = _bmm_tn(Z, Bh)
    Gm = gh[:, :N]
    Hm = gh[:, N:] + _bmm_tn(V, Kh)

    S = S_ref[...].reshape(2 * H, N, N)
    for g in range(G):
        gr = G - 1 - g
        pick = lambda x: jnp.concatenate([x[g * H:(g + 1) * H], x[(G + gr) * H:(G + gr + 1) * H]], axis=0)
        y = _bmm_nt(pick(Rh), S) + pick(Yh)
        S = S * pick(Wc)[:, 0:1, :] + _bmm(S, pick(Gm)) + pick(Hm)
        yf_ref[g * Cc:(g + 1) * Cc, :] = jnp.concatenate([y[h] for h in range(H)], axis=1)
        yr_ref[gr * Cc:(gr + 1) * Cc, :] = jnp.concatenate([y[H + h] for h in range(H)], axis=1)
    S = S.reshape(2, H, N, N)
    S_ref[...] = S
    sf_ref[...] = S


def _rwkv_scan(r, k, v, kn, lw, a, k_a, s0, B, T):
    N, C = r.shape
    Cc = RWKV_CHUNK
    nc = T // Cc
    G = 2 if nc % 2 == 0 else 1
    ng = nc // G
    H, Nh = RWKV_HEADS, RWKV_HEAD
    fwd = pl.BlockSpec((G * Cc, C), lambda b, j: (b * ng + j, 0))
    rev = pl.BlockSpec((G * Cc, C), lambda b, j: (b * ng + ng - 1 - j, 0))
    fwd2 = pl.BlockSpec((None, G * Cc, C), lambda b, j: (0, b * ng + j, 0))
    rev2 = pl.BlockSpec((None, G * Cc, C), lambda b, j: (1, b * ng + ng - 1 - j, 0))
    st = pl.BlockSpec((2, None, H, Nh, Nh), lambda b, j: (0, b, 0, 0, 0))
    yf, yr, sf = pl.pallas_call(
        functools.partial(_rwkv_scan_kernel, G=G),
        out_shape=(jax.ShapeDtypeStruct((N, C), F32), jax.ShapeDtypeStruct((N, C), F32),
                   jax.ShapeDtypeStruct((2, B, H, Nh, Nh), F32)),
        grid=(B, ng),
        in_specs=[fwd, fwd, fwd, fwd, fwd2, fwd2, rev, rev, rev, rev, rev2, rev2,
                  pl.BlockSpec((1, C), lambda b, j: (0, 0)), st],
        out_specs=(fwd, rev, st),
        scratch_shapes=[pltpu.VMEM((2, H, Nh, Nh), F32)],
        compiler_params=_cparams("arbitrary", "arbitrary"),
        name="rwkv_scan",
    )(r, k, v, kn, lw, a, r, k, v, kn, lw, a, k_a, s0)
    return yf, yr, sf


def _rwkv_post_kernel(yf_ref, yr_ref, bonus_ref, g_ref, gg_ref, gb_ref, bd_ref, o_ref):
    y = yf_ref[...] + yr_ref[...]
    bd = bd_ref[...]
    inv_n = 1.0 / RWKV_HEAD
    mean = _dot_exact_rhs(y, bd) * inv_n
    yc = y - mean
    var = _dot_exact_rhs(yc * yc, bd) * inv_n
    yn = yc * lax.rsqrt(var + GN_EPS) * gg_ref[...] + gb_ref[...]
    o_ref[...] = ((yn + bonus_ref[...]) * g_ref[...]).astype(o_ref.dtype)


def _rwkv_post(yf, yr, bonus, g, gn_g, gn_b, bd):
    N, C = yf.shape
    tm = _pick(N, (1024, 512, 256))
    row = pl.BlockSpec((tm, C), lambda i: (i, 0))
    vec = pl.BlockSpec((1, C), lambda i: (0, 0))
    return pl.pallas_call(
        _rwkv_post_kernel,
        out_shape=jax.ShapeDtypeStruct((N, C), BF16),
        grid=(N // tm,),
        in_specs=[row, row, row, row, vec, vec, pl.BlockSpec((C, C), lambda i: (0, 0))],
        out_specs=row,
        compiler_params=_cparams("arbitrary"),
        name="rwkv_post",
    )(yf, yr, bonus, g, gn_g.reshape(1, C), gn_b.reshape(1, C), bd)


def _fourier_ch_kernel(p_ref, m_ref, o_ref):
    p = p_ref[...]
    mat = m_ref[...]
    G = FOURIER_GROUP_DIM
    for g in range(p.shape[1] // G):
        o_ref[:, g * G:(g + 1) * G] = jnp.dot(p[:, g * G:(g + 1) * G], mat,
                                              preferred_element_type=F32).astype(o_ref.dtype)


def _fourier_channel(proj, col, Fd, ch_mats, B, T):
    tm = _pick(T, (1024, 512, 256))
    nt = T // tm
    G = FOURIER_GROUP_DIM
    return pl.pallas_call(
        _fourier_ch_kernel,
        out_shape=jax.ShapeDtypeStruct((2, T, B * Fd), BF16),
        grid=(B, nt, 2),
        in_specs=[pl.BlockSpec((tm, Fd), lambda b, i, s: (b * nt + i, col)),
                  pl.BlockSpec((None, G, G), lambda b, i, s: (s, 0, 0))],
        out_specs=pl.BlockSpec((None, tm, Fd), lambda b, i, s: (s, i, b)),
        compiler_params=_cparams("arbitrary", "arbitrary", "arbitrary"),
        name="fourier_channel",
    )(proj, ch_mats)


def _dft_mats(n):
    idx = jnp.arange(n, dtype=jnp.int32)
    ph = (idx[:, None] * idx[None, :]) % n
    ang = ph.astype(F32) * (2.0 * math.pi / n)
    s = 1.0 / math.sqrt(n)
    return jnp.cos(ang) * s, jnp.sin(ang) * s


def _merge_kernel(fa_ref, at_ref, rw_ref, gt_ref, x_ref, g1_ref, wa_ref, wb_ref, wc_ref, wo_ref, o_ref):
    D = x_ref.shape[1]
    ya = jnp.dot(fa_ref[...], wa_ref[...], preferred_element_type=F32)
    yb = jnp.dot(at_ref[0], wb_ref[0], preferred_element_type=F32)
    for h in range(1, at_ref.shape[0]):
        yb = yb + jnp.dot(at_ref[h], wb_ref[h], preferred_element_type=F32)
    yc = jnp.dot(rw_ref[...], wc_ref[...], preferred_element_type=F32)
    gt = gt_ref[...].astype(F32)
    merged = (_sigmoid(gt[:, :D]) * ya + _sigmoid(gt[:, D:2 * D]) * yb + _sigmoid(gt[:, 2 * D:]) * yc)
    out = jnp.dot(merged.astype(BF16), wo_ref[...], preferred_element_type=F32)
    o_ref[...] = x_ref[...] + g1_ref[...] * out


def _merge(fa, attn, rw, proj, gcol, x, g1, wa, wb, wc, wo, B, T):
    N, D = x.shape
    H = attn.shape[1]
    Fd = rw.shape[1]
    tm = _pick(T, (512, 256))
    nt = T // tm
    Bm = g1.shape[0]
    full = lambda shape: pl.BlockSpec(shape, lambda b, i: (0,) * len(shape))
    return pl.pallas_call(
        _merge_kernel,
        out_shape=jax.ShapeDtypeStruct((N, D), F32),
        grid=(B, nt),
        in_specs=[pl.BlockSpec((tm, Fd), lambda b, i: (i, b)),
                  pl.BlockSpec((None, H, tm, LANES), lambda b, i: (b, 0, i, 0)),
                  pl.BlockSpec((tm, Fd), lambda b, i: (b * nt + i, 0)),
                  pl.BlockSpec((tm, 3 * D), lambda b, i: (b * nt + i, gcol)),
                  pl.BlockSpec((tm, D), lambda b, i: (b * nt + i, 0)),
                  pl.BlockSpec((None, 1, D), lambda b, i: (b % Bm, 0, 0)),
                  full(wa.shape), full(wb.shape), full(wc.shape), full(wo.shape)],
        out_specs=pl.BlockSpec((tm, D), lambda b, i: (b * nt + i, 0)),
        compiler_params=_cparams("arbitrary", "arbitrary"),
        name="merge_out",
    )(fa, attn, rw, proj, x, g1, wa, wb, wc, wo)


def _route(scores, sel, n_exp):
    tm = scores.shape[0]
    lane = lax.broadcasted_iota(jnp.int32, (tm, LANES), 1)
    neg = -jnp.inf
    sel = jnp.where(lane < n_exp, sel, neg)
    epg = n_exp // N_GROUPS
    assert epg == 4
    odd = (lane % 2) == 1
    p1 = jnp.where(odd, pltpu.roll(sel, 1, axis=1), pltpu.roll(sel, LANES - 1, axis=1))
    hi1, lo1 = jnp.maximum(sel, p1), jnp.minimum(sel, p1)
    upper = (lane % 4) >= 2
    hi2 = jnp.where(upper, pltpu.roll(hi1, 2, axis=1), pltpu.roll(hi1, LANES - 2, axis=1))
    lo2 = jnp.where(upper, pltpu.roll(lo1, 2, axis=1), pltpu.roll(lo1, LANES - 2, axis=1))
    gscore = jnp.maximum(hi1, hi2) + jnp.maximum(jnp.minimum(hi1, hi2), jnp.maximum(lo1, lo2))
    gscore = jnp.where(lane < n_exp, gscore, neg)
    gmax = jnp.max(gscore, axis=1, keepdims=True)
    lane_f = lane.astype(F32)
    grp = (lane // epg).astype(F32)
    big = float(LANES)
    best = jnp.min(jnp.where(gscore == gmax, grp, big), axis=1, keepdims=True)
    masked = jnp.where(grp == best, sel, neg)
    m1 = jnp.max(masked, axis=1, keepdims=True)
    i1 = jnp.min(jnp.where(masked == m1, lane_f, big), axis=1, keepdims=True)
    masked2 = jnp.where(lane_f == i1, neg, masked)
    m2 = jnp.max(masked2, axis=1, keepdims=True)
    i2 = jnp.min(jnp.where(masked2 == m2, lane_f, big), axis=1, keepdims=True)
    pick1 = lane_f == i1
    pick2 = lane_f == i2
    s1 = jnp.sum(jnp.where(pick1, scores, 0.0), axis=1, keepdims=True)
    s2 = jnp.sum(jnp.where(pick2, scores, 0.0), axis=1, keepdims=True)
    return (jnp.where(pick1, s1, 0.0) + jnp.where(pick2, s2, 0.0)) / (s1 + s2)


def _moe_kernel(x_ref, g_ref, sc_ref, sh_ref, g2_ref, rw_ref, rb_ref,
                wg_ref, wu_ref, wd_ref, swg_ref, swu_ref, swd_ref, o_ref,
                m_ref, comb_ref, acc_ref, *, n_exp):
    e = pl.program_id(1)

    @pl.when(e == 0)
    def _():
        x = x_ref[...]
        m = x * lax.rsqrt(jnp.mean(x * x, axis=-1, keepdims=True) + NORM_EPS) * g_ref[...]
        m = m * (1.0 + sc_ref[...]) + sh_ref[...]
        m_ref[...] = m.astype(BF16)
        scores = _sigmoid(_dot_split(m, rw_ref[...]))
        comb_ref[...] = _route(scores, scores + rb_ref[...], n_exp)
        acc_ref[...] = jnp.zeros_like(acc_ref)

    m = m_ref[...]

    @pl.when(e < n_exp)
    def _():
        lane = lax.broadcasted_iota(jnp.int32, comb_ref.shape, 1)
        ce = jnp.sum(jnp.where(lane == e, comb_ref[...], 0.0), axis=1, keepdims=True)
        hg = jnp.dot(m, wg_ref[...].astype(BF16), preferred_element_type=F32)
        hu = jnp.dot(m, wu_ref[...].astype(BF16), preferred_element_type=F32)
        hh = hg * _sigmoid(hg) * hu * ce
        acc_ref[...] += jnp.dot(hh.astype(BF16), wd_ref[...].astype(BF16), preferred_element_type=F32)

    @pl.when(e == n_exp)
    def _():
        hg = jnp.dot(m, swg_ref[...], preferred_element_type=F32)
        hu = jnp.dot(m, swu_ref[...], preferred_element_type=F32)
        hh = hg * _sigmoid(hg) * hu
        y = acc_ref[...] + jnp.dot(hh.astype(BF16), swd_ref[...], preferred_element_type=F32)
        o_ref[...] = x_ref[...] + g2_ref[...] * y


def _moe(x, g, sc, sh, g2, rw_pad, rb_pad, wg, wu, wd, swg, swu, swd):
    N, D = x.shape
    E, _, Ff = wg.shape
    Bm = sc.shape[0]
    rpm = N // Bm
    tm = min(rpm, 1024)
    assert rpm % tm == 0
    mod = pl.BlockSpec((None, 1, D), lambda i, e: (i * tm // rpm, 0, 0))
    full = lambda shape: pl.BlockSpec(shape, lambda i, e: (0,) * len(shape))
    return pl.pallas_call(
        functools.partial(_moe_kernel, n_exp=E),
        out_shape=jax.ShapeDtypeStruct((N, D), F32),
        grid=(N // tm, E + 1),
        in_specs=[pl.BlockSpec((tm, D), lambda i, e: (i, 0)),
                  full((1, D)), mod, mod, mod,
                  full(rw_pad.shape), full(rb_pad.shape),
                  pl.BlockSpec((None, D, Ff), lambda i, e: (jnp.minimum(e, E - 1), 0, 0)),
                  pl.BlockSpec((None, D, Ff), lambda i, e: (jnp.minimum(e, E - 1), 0, 0)),
                  pl.BlockSpec((None, Ff, D), lambda i, e: (jnp.minimum(e, E - 1), 0, 0)),
                  full(swg.shape), full(swu.shape), full(swd.shape)],
        out_specs=pl.BlockSpec((tm, D), lambda i, e: (i, 0)),
        scratch_shapes=[pltpu.VMEM((tm, D), BF16), pltpu.VMEM((tm, LANES), F32),
                        pltpu.VMEM((tm, D), F32)],
        compiler_params=_cparams("arbitrary", "arbitrary"),
        name="moe",
    )(x, g.reshape(1, D), sc, sh, g2, rw_pad, rb_pad, wg, wu, wd, swg, swu, swd)


def _rope_partner(w):
    lead = w.shape[:-1]
    wr = w.reshape(lead + (2, 2, ROPE_PAIRS))
    return jnp.stack([-wr[..., 1, :], wr[..., 0, :]], axis=-2).reshape(w.shape)


def _rope_tables(T):
    rows = T // GRID_W
    row = jnp.repeat(jnp.arange(rows), GRID_W)
    col = jnp.tile(jnp.arange(GRID_W), rows)
    inv = ROPE_BASE ** (-jnp.arange(ROPE_PAIRS, dtype=F32) / ROPE_PAIRS)
    ang = jnp.stack([row, col], axis=-1).astype(F32)[..., None] * inv
    cos = jnp.broadcast_to(jnp.cos(ang)[:, :, None, :], (T, 2, 2, ROPE_PAIRS)).reshape(T, MLA_ROPE)
    sin = jnp.broadcast_to(jnp.sin(ang)[:, :, None, :], (T, 2, 2, ROPE_PAIRS)).reshape(T, MLA_ROPE)
    pad = LANES - MLA_QK
    z_n, z_p = jnp.zeros((T, MLA_NOPE), F32), jnp.zeros((T, pad), F32)
    cq = jnp.concatenate([jnp.ones((T, MLA_NOPE), F32), cos, z_p], axis=1)
    sq = jnp.concatenate([z_n, sin, z_p], axis=1)
    ck = jnp.concatenate([z_n, cos, z_p], axis=1)
    return cq, sq, ck, sq


def _identity_tables(T):
    pad = LANES - MLA_QK
    z_n, z_p = jnp.zeros((T, MLA_NOPE), F32), jnp.zeros((T, pad), F32)
    one_r = jnp.ones((T, MLA_ROPE), F32)
    cq = jnp.concatenate([jnp.ones((T, MLA_NOPE), F32), one_r, z_p], axis=1)
    ck = jnp.concatenate([z_n, one_r, z_p], axis=1)
    zero = jnp.zeros((T, LANES), F32)
    return cq, zero, ck, zero


def kernel(x, c, ctx, c_ctx, w_ada, b_ada, norm1_g, norm2_g, w_in, q_norm_g, kv_norm_g, w_uq, w_ukv, rwkv_mu, rwkv_w0, rwkv_w_up, rwkv_a0, rwkv_a_up, rwkv_g_up, rwkv_k_k, rwkv_k_a, rwkv_r_k, rwkv_gn_g, rwkv_gn_b, w_proj_a, w_proj_b, w_proj_c, w_out, router_w, router_b, exp_w_gate, exp_w_up, exp_w_down, sh_w_gate, sh_w_up, sh_w_down, final_norm_g):
    B, T, D = x.shape
    Tc = ctx.shape[1]
    L = w_ada.shape[0]
    H = MLA_HEADS
    QL, KVL = q_norm_g.shape[1], kv_norm_g.shape[1]
    Fd = w_proj_a.shape[1]
    C = RWKV_DIM
    E = router_w.shape[1]
    N, Nc = B * T, B * Tc

    sizes = (Fd, QL, KVL, MLA_ROPE, 3 * C + DECAY_LORA + AAA_LORA + GATE_LORA, D, D, D)
    offs = np.concatenate([[0], np.cumsum(sizes)])

    lat_tabs = _rope_tables(T)
    ctx_tabs = _identity_tables(Tc)
    cc, cs = _dft_mats(FOURIER_GROUP_DIM)
    ch_mats = jnp.stack([cc, cs]).astype(BF16)
    tcos, tsin = _dft_mats(T)
    dft_lat = jnp.concatenate([tcos, -tsin], axis=1).astype(BF16)
    ccos, csin = _dft_mats(Tc)
    dft_ctx = jnp.concatenate([ccos, -csin], axis=1).astype(BF16)
    hid = jnp.arange(C) // RWKV_HEAD
    bd = (hid[:, None] == hid[None, :]).astype(BF16)
    rw_pad = jnp.zeros((D, LANES), F32).at[:, :E].set(router_w)
    rb_pad = jnp.zeros((1, LANES), F32).at[0, :E].set(router_b)

    R = -(-(B + 1) // 8) * 8
    c_rows = jnp.zeros((R, D), F32).at[:B].set(c).at[B].set(c_ctx)
    mod = _ada_mod(c_rows, w_ada, b_ada)

    x = x.reshape(N, D)
    ctx = ctx.reshape(Nc, D)
    zero_state = jnp.zeros((2, B, RWKV_HEADS, RWKV_HEAD, RWKV_HEAD), F32)

    for l in range(L):
        last = l == L - 1
        ml = mod[l, :B].reshape(B, 1, 6, D)
        mc = mod[l, B:B + 1].reshape(1, 1, 6, D)
        sh1, sc1, g1, sh2, sc2, g2 = [ml[:, :, j] for j in range(6)]
        csh1, csc1, cg1, csh2, csc2, cg2 = [mc[:, :, j] for j in range(6)]

        wi = w_in[l]
        w_kr = wi[:, offs[3]:offs[4]]
        zl = jnp.zeros((D, MLA_NOPE), F32)
        zr = jnp.zeros((D, LANES - MLA_QK), F32)
        segs = [wi[:, offs[5]:offs[8]],
                wi[:, offs[0]:offs[1]],
                wi[:, offs[4]:offs[5]],
                jnp.concatenate([wi[:, offs[1]:offs[3]], zl, w_kr, zr, zl, _rope_partner(w_kr), zr], axis=1)]
        seg_off, n_all = _column_layout([s.shape[1] for s in segs])
        w_all = jnp.zeros((D, n_all), BF16)
        for s, o in zip(segs, seg_off):
            w_all = w_all.at[:, o:o + s.shape[1]].set(s.astype(BF16))
        gt_col, pf_col, rw_col, qkv_col = [o // s.shape[1] for s, o in zip(segs, seg_off)]
        uq = w_uq[l].reshape(QL, H, MLA_QK).transpose(1, 0, 2)
        padq = jnp.zeros((H, QL, LANES - MLA_QK), F32)
        wqa = jnp.concatenate([uq, padq], axis=2).astype(BF16)
        wqb = jnp.concatenate([jnp.zeros((H, QL, MLA_NOPE), F32), _rope_partner(uq[..., MLA_NOPE:]), padq],
                              axis=2).astype(BF16)
        ukv = w_ukv[l].reshape(KVL, H, MLA_NOPE + MLA_V).transpose(1, 0, 2)
        wk = jnp.concatenate([ukv[..., :MLA_NOPE], jnp.zeros((H, KVL, LANES - MLA_NOPE), F32)], axis=2).astype(BF16)
        wv = jnp.concatenate([ukv[..., MLA_NOPE:], jnp.zeros((H, KVL, LANES - MLA_V), F32)], axis=2).astype(BF16)
        wb = jnp.concatenate([w_proj_b[l].reshape(H, MLA_V, D), jnp.zeros((H, LANES - MLA_V, D), F32)],
                             axis=1).astype(BF16)
        wa = w_proj_a[l].astype(BF16)
        wc = w_proj_c[l].astype(BF16)
        wo = w_out[l].astype(BF16)
        zd = jnp.zeros((2, DECAY_LORA, C), F32)
        prm = dict(
            mu=rwkv_mu[l].reshape(1, -1), w0=rwkv_w0[l], a0=rwkv_a0[l],
            wup=jnp.concatenate([rwkv_w_up[l], zd], axis=1).astype(BF16),
            aup=jnp.concatenate([zd, rwkv_a_up[l]], axis=1).astype(BF16),
            gup=rwkv_g_up[l].astype(BF16), k_k=rwkv_k_k[l].reshape(1, C), k_a=rwkv_k_a[l].reshape(1, C),
            r_k=rwkv_r_k[l].reshape(1, C), bd=bd)

        def mixers(xx, sc, sh, tabs, Tx):
            proj = _in_proj(xx, norm1_g[l], sc, sh, w_all)
            q, k, v = _qkv_project(proj, qkv_col, q_norm_g[l], kv_norm_g[l], wqa, wqb, wk, wv, tabs, B, Tx)
            rp = _rwkv_prep(proj, rw_col, prm, B, Tx)
            return proj, q, k, v, rp

        proj_l, q_l, k_l, v_l, rp_l = mixers(x, sc1, sh1, lat_tabs, T)
        proj_c, q_c, k_c, v_c, rp_c = mixers(ctx, csc1, csh1, ctx_tabs, Tc)

        attn_l = _attention(q_l, [(k_l, v_l), (k_c, v_c)])

        yf_c, yr_c, s_ctx = _rwkv_scan(*rp_c[:6], prm["k_a"], zero_state, B, Tc)
        yf_l, yr_l, _ = _rwkv_scan(*rp_l[:6], prm["k_a"], s_ctx, B, T)
        rwo_l = _rwkv_post(yf_l, yr_l, rp_l[7], rp_l[6], rwkv_gn_g[l], rwkv_gn_b[l], bd)

        zcs_l = _fourier_channel(proj_l, pf_col, Fd, ch_mats, B, T).reshape(2 * T, B * Fd)
        fa_l = _mm(dft_lat, zcs_l, BF16, tn=B * Fd, tk=min(2 * T, 1024))

        x = _merge(fa_l, attn_l, rwo_l, proj_l, gt_col, x, g1, wa, wb, wc, wo, B, T)
        moe_w = (rw_pad, rb_pad, exp_w_gate[l], exp_w_up[l], exp_w_down[l],
                 sh_w_gate[l].astype(BF16), sh_w_up[l].astype(BF16), sh_w_down[l].astype(BF16))
        x = _moe(x, norm2_g[l], sc2, sh2, g2, *moe_w)

        if not last:
            attn_c = _attention(q_c, [(k_c, v_c)])
            rwo_c = _rwkv_post(yf_c, yr_c, rp_c[7], rp_c[6], rwkv_gn_g[l], rwkv_gn_b[l], bd)
            zcs_c = _fourier_channel(proj_c, pf_col, Fd, ch_mats, B, Tc).reshape(2 * Tc, B * Fd)
            fa_c = _mm(dft_ctx, zcs_c, BF16, tn=B * Fd, tk=min(2 * Tc, 1024))
            ctx = _merge(fa_c, attn_c, rwo_c, proj_c, gt_col, ctx, cg1, wa, wb, wc, wo, B, Tc)
            ctx = _moe(ctx, norm2_g[l], csc2, csh2, cg2, *moe_w)

    ones = jnp.ones((1, 1, D), F32)
    out = _norm(x, final_norm_g, ones, ones, F32, modulate=False)
    return out.reshape(B, T, D)
```

```python
import functools
import math

import numpy as np
import jax
import jax.numpy as jnp
from jax import lax
from jax.experimental import pallas as pl
from jax.experimental.pallas import tpu as pltpu

F32 = jnp.float32
BF16 = jnp.bfloat16

GRID_W = 64
FOURIER_GROUP_DIM = 128
MLA_HEADS = 8
MLA_NOPE = 64
MLA_ROPE = 32
MLA_QK = MLA_NOPE + MLA_ROPE
MLA_V = 64
MLA_SCALE = MLA_QK ** -0.5
QK_PRESCALE = MLA_SCALE * math.log2(math.e)
ROPE_PAIRS = MLA_ROPE // 4
ROPE_BASE = 10000.0
RWKV_HEADS = 8
RWKV_HEAD = 64
RWKV_DIM = RWKV_HEADS * RWKV_HEAD
DECAY_LORA = 64
AAA_LORA = 64
GATE_LORA = 128
GN_EPS = 64e-5
N_GROUPS = 4
NORM_EPS = 1e-6

LANES = 128
VMEM_LIMIT_BYTES = 48 * 1024 * 1024
RWKV_CHUNK = 64
MOE_ROWS = 128
MOE_VMEM_LIMIT_BYTES = 56 * 1024 * 1024


def _cparams(*sem):
    return pltpu.CompilerParams(dimension_semantics=sem, vmem_limit_bytes=VMEM_LIMIT_BYTES)


def _sigmoid(z):
    return 1.0 / (1.0 + jnp.exp(-z))


def _bdot(a, b):
    return jnp.dot(a.astype(BF16), b.astype(BF16), preferred_element_type=F32)


def _dot_nt(a, b):
    return lax.dot_general(a.astype(BF16), b.astype(BF16), (((1,), (1,)), ((), ())),
                           preferred_element_type=F32)


def _dot_tn(a, b):
    return lax.dot_general(a.astype(BF16), b.astype(BF16), (((0,), (0,)), ((), ())),
                           preferred_element_type=F32)


def _split2(x):
    hi = x.astype(BF16)
    lo = (x - hi.astype(F32)).astype(BF16)
    return hi, lo


def _dot_exact_rhs(x, ones_bf16):
    hi, lo = _split2(x)
    return (jnp.dot(hi, ones_bf16, preferred_element_type=F32)
            + jnp.dot(lo, ones_bf16, preferred_element_type=F32))


def _dot_split(a, b):
    ah, al = _split2(a)
    bh, bl = _split2(b)
    return (jnp.dot(ah, bh, preferred_element_type=F32)
            + jnp.dot(ah, bl, preferred_element_type=F32)
            + jnp.dot(al, bh, preferred_element_type=F32))


def _ada_kernel(c_ref, w_ref, b_ref, o_ref):
    c = c_ref[...]
    s = c * _sigmoid(c)
    o_ref[...] = _dot_split(s, w_ref[...]) + b_ref[...]


def _ada_mod(c_rows, w_ada, b_ada):
    L, D, D6 = w_ada.shape
    R = c_rows.shape[0]
    tn = 1536 if D6 % 1536 == 0 else D6
    return pl.pallas_call(
        _ada_kernel,
        out_shape=jax.ShapeDtypeStruct((L, R, D6), F32),
        grid=(L, D6 // tn),
        in_specs=[pl.BlockSpec((R, D), lambda l, j: (0, 0)),
                  pl.BlockSpec((None, D, tn), lambda l, j: (l, 0, j)),
                  pl.BlockSpec((None, 1, tn), lambda l, j: (l, 0, j))],
        out_specs=pl.BlockSpec((None, R, tn), lambda l, j: (l, 0, j)),
        compiler_params=_cparams("arbitrary", "arbitrary"),
        name="ada_mod",
    )(c_rows, w_ada, b_ada.reshape(L, 1, D6))


def _norm_kernel(x_ref, g_ref, sc_ref, sh_ref, o_ref, *, modulate):
    x = x_ref[...].astype(F32)
    y = x * lax.rsqrt(jnp.mean(x * x, axis=-1, keepdims=True) + NORM_EPS) * g_ref[...]
    if modulate:
        y = y * (1.0 + sc_ref[...]) + sh_ref[...]
    o_ref[...] = y.astype(o_ref.dtype)


def _norm(x, g, sc, sh, out_dtype, modulate=True):
    N, D = x.shape
    Bm = sc.shape[0]
    rpm = N // Bm
    tm = min(rpm, 1024)
    assert rpm % tm == 0
    return pl.pallas_call(
        functools.partial(_norm_kernel, modulate=modulate),
        out_shape=jax.ShapeDtypeStruct((N, D), out_dtype),
        grid=(N // tm,),
        in_specs=[pl.BlockSpec((tm, D), lambda i: (i, 0)),
                  pl.BlockSpec((1, D), lambda i: (0, 0)),
                  pl.BlockSpec((None, 1, D), lambda i: (i * tm // rpm, 0, 0)),
                  pl.BlockSpec((None, 1, D), lambda i: (i * tm // rpm, 0, 0))],
        out_specs=pl.BlockSpec((tm, D), lambda i: (i, 0)),
        compiler_params=_cparams("arbitrary"),
        name="rmsnorm_mod",
    )(x, g.reshape(1, D), sc, sh)


def _in_proj_kernel(x_ref, g_ref, sc_ref, sh_ref, w_ref, o_ref, h_ref):
    @pl.when(pl.program_id(1) == 0)
    def _():
        x = x_ref[...]
        y = x * lax.rsqrt(jnp.mean(x * x, axis=-1, keepdims=True) + NORM_EPS) * g_ref[...]
        h_ref[...] = (y * (1.0 + sc_ref[...]) + sh_ref[...]).astype(BF16)

    o_ref[...] = jnp.dot(h_ref[...], w_ref[...], preferred_element_type=F32).astype(o_ref.dtype)


def _in_proj(x, g, sc, sh, w_all):
    N, D = x.shape
    Nout = w_all.shape[1]
    Bm = sc.shape[0]
    rpm = N // Bm
    tm = min(rpm, 1024)
    assert rpm % tm == 0
    tn = _pick(Nout, (896, 512, 256, 128))
    mod = pl.BlockSpec((None, 1, D), lambda i, j: (i * tm // rpm, 0, 0))
    return pl.pallas_call(
        _in_proj_kernel,
        out_shape=jax.ShapeDtypeStruct((N, Nout), BF16),
        grid=(N // tm, Nout // tn),
        in_specs=[pl.BlockSpec((tm, D), lambda i, j: (i, 0)),
                  pl.BlockSpec((1, D), lambda i, j: (0, 0)), mod, mod,
                  pl.BlockSpec((D, tn), lambda i, j: (0, j))],
        out_specs=pl.BlockSpec((tm, tn), lambda i, j: (i, j)),
        scratch_shapes=[pltpu.VMEM((tm, D), BF16)],
        compiler_params=_cparams("arbitrary", "arbitrary"),
        name="in_proj",
    )(x, g.reshape(1, D), sc, sh, w_all)


def _column_layout(widths):
    offs, cur = [], 0
    for w in widths:
        cur = -(-cur // w) * w
        offs.append(cur)
        cur += w
    return offs, -(-cur // LANES) * LANES


def _mm_kernel(a_ref, b_ref, o_ref):
    o_ref[...] = jnp.dot(a_ref[...], b_ref[...], preferred_element_type=F32).astype(o_ref.dtype)


def _mm_acc_kernel(a_ref, b_ref, o_ref, acc_ref, *, nk):
    k = pl.program_id(2)

    @pl.when(k == 0)
    def _():
        acc_ref[...] = jnp.zeros_like(acc_ref)

    acc_ref[...] += jnp.dot(a_ref[...], b_ref[...], preferred_element_type=F32)

    @pl.when(k == nk - 1)
    def _():
        o_ref[...] = acc_ref[...].astype(o_ref.dtype)


def _pick(n, prefs):
    for p in prefs:
        if n % p == 0:
            return p
    return n


def _mm(a, w, out_dtype, tn=None, tk=None):
    M, K = a.shape
    Nn = w.shape[1]
    tm = _pick(M, (1024, 512, 256))
    tn = tn or _pick(Nn, (1024, 896, 512))
    tk = tk or K
    if tk == K:
        return pl.pallas_call(
            _mm_kernel,
            out_shape=jax.ShapeDtypeStruct((M, Nn), out_dtype),
            grid=(M // tm, Nn // tn),
            in_specs=[pl.BlockSpec((tm, K), lambda i, j: (i, 0)),
                      pl.BlockSpec((K, tn), lambda i, j: (0, j))],
            out_specs=pl.BlockSpec((tm, tn), lambda i, j: (i, j)),
            compiler_params=_cparams("arbitrary", "arbitrary"),
            name="matmul",
        )(a, w)
    nk = K // tk
    return pl.pallas_call(
        functools.partial(_mm_acc_kernel, nk=nk),
        out_shape=jax.ShapeDtypeStruct((M, Nn), out_dtype),
        grid=(M // tm, Nn // tn, nk),
        in_specs=[pl.BlockSpec((tm, tk), lambda i, j, k: (i, k)),
                  pl.BlockSpec((tk, tn), lambda i, j, k: (k, j))],
        out_specs=pl.BlockSpec((tm, tn), lambda i, j, k: (i, j)),
        scratch_shapes=[pltpu.VMEM((tm, tn), F32)],
        compiler_params=_cparams("arbitrary", "arbitrary", "arbitrary"),
        name="matmul_kacc",
    )(a, w)


def _qkv_kernel(p_ref, gq_ref, gkv_ref, wqa_ref, wqb_ref, wk_ref, wv_ref,
                cq_ref, sq_ref, ck_ref, sk_ref, q_ref, k_ref, v_ref, *, ql, kvl):
    p = p_ref[...].astype(F32)
    pq = p[:, :ql]
    pkv = p[:, ql:ql + kvl]
    kra = p[:, ql + kvl:ql + kvl + LANES]
    krb = p[:, ql + kvl + LANES:ql + kvl + 2 * LANES]
    qn = (pq * lax.rsqrt(jnp.mean(pq * pq, axis=-1, keepdims=True) + NORM_EPS) * gq_ref[...]).astype(BF16)
    kvn = (pkv * lax.rsqrt(jnp.mean(pkv * pkv, axis=-1, keepdims=True) + NORM_EPS) * gkv_ref[...]).astype(BF16)
    kr = kra * ck_ref[...] + krb * sk_ref[...]
    cq = cq_ref[...] * QK_PRESCALE
    sq = sq_ref[...] * QK_PRESCALE
    for h in range(q_ref.shape[0]):
        qa = jnp.dot(qn, wqa_ref[h], preferred_element_type=F32)
        qb = jnp.dot(qn, wqb_ref[h], preferred_element_type=F32)
        q_ref[h] = (qa * cq + qb * sq).astype(q_ref.dtype)
        k_ref[h] = (jnp.dot(kvn, wk_ref[h], preferred_element_type=F32) + kr).astype(k_ref.dtype)
        v_ref[h] = jnp.dot(kvn, wv_ref[h], preferred_element_type=F32).astype(v_ref.dtype)


def _qkv_project(proj, col, gq, gkv, wqa, wqb, wk, wv, tabs, B, T):
    N = proj.shape[0]
    H = wqa.shape[0]
    ql, kvl = wqa.shape[1], wk.shape[1]
    W = ql + kvl + 2 * LANES
    tm = _pick(T, (512, 256))
    nt = T // tm
    cq, sq, ck, sk = tabs
    full = lambda shape: pl.BlockSpec(shape, lambda i: (0,) * len(shape))
    tab_spec = pl.BlockSpec((tm, LANES), lambda i: (i % nt, 0))
    out_spec = pl.BlockSpec((None, H, tm, LANES), lambda i: (i // nt, 0, i % nt, 0))
    out_sds = jax.ShapeDtypeStruct((B, H, T, LANES), BF16)
    return pl.pallas_call(
        functools.partial(_qkv_kernel, ql=ql, kvl=kvl),
        out_shape=(out_sds, out_sds, out_sds),
        grid=(N // tm,),
        in_specs=[pl.BlockSpec((tm, W), lambda i: (i, col)),
                  full((1, ql)), full((1, kvl)),
                  full(wqa.shape), full(wqb.shape), full(wk.shape), full(wv.shape),
                  tab_spec, tab_spec, tab_spec, tab_spec],
        out_specs=(out_spec, out_spec, out_spec),
        compiler_params=_cparams("arbitrary"),
        name="mla_qkv",
    )(proj, gq.reshape(1, ql), gkv.reshape(1, kvl), wqa, wqb, wk, wv, cq, sq, ck, sk)


def _attn_kernel(q_ref, *refs, tiles):
    o_ref = refs[-1]
    q = q_ref[...]
    m = l = acc = None
    for src, start, size in tiles:
        k = refs[2 * src][start:start + size, :]
        v = refs[2 * src + 1][start:start + size, :]
        s = _dot_nt(q, k)
        mt = jnp.max(s, axis=-1, keepdims=True)
        if m is None:
            m_new = mt
            p = jnp.exp2(s - m_new)
            l = jnp.sum(p, axis=-1, keepdims=True)
            acc = jnp.dot(p.astype(BF16), v, preferred_element_type=F32)
        else:
            m_new = jnp.maximum(m, mt)
            alpha = jnp.exp2(m - m_new)
            p = jnp.exp2(s - m_new)
            l = alpha * l + jnp.sum(p, axis=-1, keepdims=True)
            acc = alpha * acc + jnp.dot(p.astype(BF16), v, preferred_element_type=F32)
        m = m_new
    o_ref[...] = (acc / l).astype(o_ref.dtype)


def _attention(q, kvs):
    B, H, T, dq = q.shape
    tq = _pick(T, (512, 256))
    tiles = []
    for src, (k, _) in enumerate(kvs):
        Tk = k.shape[2]
        tk = _pick(Tk, (1024, 512, 256))
        tiles += [(src, st, tk) for st in range(0, Tk, tk)]
    in_specs = [pl.BlockSpec((None, None, tq, dq), lambda b, h, i: (b, h, i, 0))]
    args = [q]
    for k, v in kvs:
        Tk = k.shape[2]
        in_specs += [pl.BlockSpec((None, None, Tk, dq), lambda b, h, i: (b, h, 0, 0)),
                     pl.BlockSpec((None, None, Tk, LANES), lambda b, h, i: (b, h, 0, 0))]
        args += [k, v]
    return pl.pallas_call(
        functools.partial(_attn_kernel, tiles=tuple(tiles)),
        out_shape=jax.ShapeDtypeStruct((B, H, T, LANES), BF16),
        grid=(B, H, T // tq),
        in_specs=in_specs,
        out_specs=pl.BlockSpec((None, None, tq, LANES), lambda b, h, i: (b, h, i, 0)),
        compiler_params=_cparams("arbitrary", "arbitrary", "arbitrary"),
        name="attention",
    )(*args)


def _softplus(u):
    return jnp.maximum(u, 0.0) + jnp.log(1.0 + jnp.exp(-jnp.abs(u)))


def _rwkv_prep_kernel(p_ref, pp_ref, pn_ref, mu_ref, w0_ref, wup_ref, a0_ref, aup_ref, gup_ref,
                      kk_ref, ka_ref, rk_ref, bd_ref,
                      r_o, k_o, v_o, kn_o, lw_o, a_o, g_o, bonus_o, *, tps):
    i = pl.program_id(0)
    p = p_ref[...].astype(F32)
    tt = p.shape[0]
    C = RWKV_DIM
    halo = pp_ref.shape[0]
    prev_row = jnp.where(i % tps == 0, 0.0, pp_ref[...].astype(F32)[halo - 1:halo, :])
    next_row = jnp.where(i % tps == tps - 1, 0.0, pn_ref[...].astype(F32)[0:1, :])
    rows = lax.broadcasted_iota(jnp.int32, (tt, 1), 0)
    pprev = jnp.where(rows == 0, prev_row, pltpu.roll(p, 1, axis=0))
    pnext = jnp.where(rows == tt - 1, next_row, pltpu.roll(p, tt - 1, axis=0))
    xs = p + (0.5 * (pprev + pnext) - p) * mu_ref[...]
    r = xs[:, :C]
    k = xs[:, C:2 * C]
    v = xs[:, 2 * C:3 * C]
    xwa = xs[:, 3 * C:3 * C + DECAY_LORA + AAA_LORA]
    xg = xs[:, 3 * C + DECAY_LORA + AAA_LORA:]
    bd = bd_ref[...]
    g_o[...] = _bdot(_sigmoid(xg), gup_ref[...])
    kn = k * kk_ref[...]
    kn = kn * lax.rsqrt(_dot_exact_rhs(kn * kn, bd) + 1e-12)
    th = jnp.tanh(xwa)
    bonus = jnp.zeros_like(r)
    for d in range(2):
        w = -_softplus(-(w0_ref[d:d + 1, :] + _bdot(th, wup_ref[d]))) - 0.5
        lw_o[d] = -jnp.exp(w)
        a = _sigmoid(a0_ref[d:d + 1, :] + _bdot(xwa, aup_ref[d]))
        a_o[d] = a
        kd = k * (1.0 + (a - 1.0) * ka_ref[...])
        bonus = bonus + _dot_exact_rhs(r * kd * rk_ref[...], bd) * v
    r_o[...] = r
    k_o[...] = k
    v_o[...] = v
    kn_o[...] = kn
    bonus_o[...] = bonus


def _rwkv_prep(proj, col, prm, B, T):
    N = proj.shape[0]
    W = prm["mu"].shape[1]
    C = RWKV_DIM
    tt = _pick(T, (256,))
    tps = T // tt
    halo = 16
    nh = N // halo
    full = lambda shape: pl.BlockSpec(shape, lambda i: (0,) * len(shape))
    o_spec = pl.BlockSpec((tt, C), lambda i: (i, 0))
    o2_spec = pl.BlockSpec((2, tt, C), lambda i: (0, i, 0))
    sds = jax.ShapeDtypeStruct((N, C), F32)
    sds2 = jax.ShapeDtypeStruct((2, N, C), F32)
    return pl.pallas_call(
        functools.partial(_rwkv_prep_kernel, tps=tps),
        out_shape=(sds, sds, sds, sds, sds2, sds2, sds, sds),
        grid=(N // tt,),
        in_specs=[pl.BlockSpec((tt, W), lambda i: (i, col)),
                  pl.BlockSpec((halo, W), lambda i: (jnp.maximum(i * (tt // halo) - 1, 0), col)),
                  pl.BlockSpec((halo, W), lambda i: (jnp.minimum((i + 1) * (tt // halo), nh - 1), col)),
                  full((1, W)), full((2, C)), full((2, LANES, C)), full((2, C)), full((2, LANES, C)),
                  full((GATE_LORA, C)), full((1, C)), full((1, C)), full((1, C)), full((C, C))],
        out_specs=(o_spec, o_spec, o_spec, o_spec, o2_spec, o2_spec, o_spec, o_spec),
        compiler_params=_cparams("arbitrary"),
        name="rwkv_prep",
    )(proj, proj, proj, prm["mu"], prm["w0"], prm["wup"], prm["a0"], prm["aup"], prm["gup"],
      prm["k_k"], prm["k_a"], prm["r_k"], prm["bd"])


def _bmm(a, b):
    return lax.dot_general(a.astype(BF16), b.astype(BF16), (((2,), (1,)), ((0,), (0,))),
                           preferred_element_type=F32)


def _bmm_nt(a, b):
    return lax.dot_general(a.astype(BF16), b.astype(BF16), (((2,), (2,)), ((0,), (0,))),
                           preferred_element_type=F32)


def _bmm_tn(a, b):
    return lax.dot_general(a.astype(BF16), b.astype(BF16), (((1,), (1,)), ((0,), (0,))),
                           preferred_element_type=F32)


def _rwkv_scan_kernel(rf_ref, kf_ref, vf_ref, knf_ref, lwf_ref, af_ref,
                      rr_ref, kr_ref, vr_ref, knr_ref, lwr_ref, ar_ref, ka_ref, s0_ref,
                      yf_ref, yr_ref, sf_ref, S_ref, *, G):
    j = pl.program_id(1)

    @pl.when(j == 0)
    def _():
        S_ref[...] = s0_ref[...]

    Cc, N, H = RWKV_CHUNK, RWKV_HEAD, RWKV_HEADS
    R = G * Cc
    ti = lax.broadcasted_iota(jnp.int32, (R, R), 0)
    ii = lax.broadcasted_iota(jnp.int32, (R, R), 1)
    same = (ti // Cc) == (ii // Cc)
    m_all = same.astype(F32).astype(BF16)

    def heads(x):
        return jnp.stack([x[g * Cc:(g + 1) * Cc, h * N:(h + 1) * N] for g in range(G) for h in range(H)])

    per_dir = []
    for d, refs in enumerate(((rf_ref, kf_ref, vf_ref, knf_ref, lwf_ref, af_ref),
                              (rr_ref, kr_ref, vr_ref, knr_ref, lwr_ref, ar_ref))):
        r, k, v, kn, lw, a = [x[...] for x in refs]
        before = (ii <= ti) if d == 0 else (ii >= ti)
        m_incl = (same & before).astype(F32).astype(BF16)
        h1 = lw.astype(BF16)
        r1 = lw - h1.astype(F32)
        h2 = r1.astype(BF16)
        h3 = (r1 - h2.astype(F32)).astype(BF16)
        cum = (jnp.dot(m_incl, h1, preferred_element_type=F32)
               + jnp.dot(m_incl, h2, preferred_element_type=F32)
               + jnp.dot(m_incl, h3, preferred_element_type=F32))
        tot = (jnp.dot(m_all, h1, preferred_element_type=F32)
               + jnp.dot(m_all, h2, preferred_element_type=F32)
               + jnp.dot(m_all, h3, preferred_element_type=F32))
        e_neg = jnp.exp(-cum)
        e_rem = jnp.exp(tot - cum)
        b = kn * a
        kd = k * (1.0 + (a - 1.0) * ka_ref[...])
        per_dir.append([heads(x) for x in (-kn * jnp.exp(cum - lw), r * jnp.exp(cum), b * e_neg, kd * e_neg,
                                           b * e_rem, kd * e_rem, v, jnp.exp(tot))])

    A, Rt, Bt, Kt, Bh, Kh, V, Wc = [jnp.concatenate([per_dir[0][i], per_dir[1][i]], axis=0) for i in range(8)]
    nb = 2 * G * H
    bi = lax.broadcasted_iota(jnp.int32, (nb, Cc, Cc), 0)
    t3 = lax.broadcasted_iota(jnp.int32, (nb, Cc, Cc), 1)
    i3 = lax.broadcasted_iota(jnp.int32, (nb, Cc, Cc), 2)
    order = jnp.where(bi < G * H, t3 - i3, i3 - t3)
    strict = order > 0
    incl = order >= 0

    AR = jnp.concatenate([A, Rt], axis=1)
    ab = _bmm_nt(AR, Bt)
    ak = _bmm_nt(AR, Kt)
    Lab = jnp.where(strict, ab[:, :Cc], 0.0)
    Mrb = jnp.where(incl, ab[:, Cc:], 0.0)
    Lak = jnp.where(strict, ak[:, :Cc], 0.0)
    Mrk = jnp.where(incl, ak[:, Cc:], 0.0)
    lv = _bmm(jnp.concatenate([Lak, Mrk], axis=1), V)
    Z = jnp.concatenate([A, lv[:, :Cc]], axis=2)
    P = Lab
    steps = int(math.log2(Cc))
    for s in range(steps):
        if s < steps - 1:
            out = _bmm(P, jnp.concatenate([Z, P], axis=2))
            Z = Z + out[:, :, :2 * N]
            P = out[:, :, 2 * N:]
        else:
            Z = Z + _bmm(P, Z)
    mz = _bmm(Mrb, Z)
    Rh = Rt + mz[:, :, :N]
    Yh = mz[:, :, N:] + lv[:, Cc:]
    gh = _bmm_tn(Z, Bh)
    Gm = gh[:, :N]
    Hm = gh[:, N:] + _bmm_tn(V, Kh)

    S = S_ref[...].reshape(2 * H, N, N)
    for g in range(G):
        gr = G - 1 - g
        pick = lambda x: jnp.concatenate([x[g * H:(g + 1) * H], x[(G + gr) * H:(G + gr + 1) * H]], axis=0)
        y = _bmm_nt(pick(Rh), S) + pick(Yh)
        S = S * pick(Wc)[:, 0:1, :] + _bmm(S, pick(Gm)) + pick(Hm)
        yf_ref[g * Cc:(g + 1) * Cc, :] = jnp.concatenate([y[h] for h in range(H)], axis=1)
        yr_ref[gr * Cc:(gr + 1) * Cc, :] = jnp.concatenate([y[H + h] for h in range(H)], axis=1)
    S = S.reshape(2, H, N, N)
    S_ref[...] = S
    sf_ref[...] = S


def _rwkv_scan(r, k, v, kn, lw, a, k_a, s0, B, T):
    N, C = r.shape
    Cc = RWKV_CHUNK
    nc = T // Cc
    G = 2 if nc % 2 == 0 else 1
    ng = nc // G
    H, Nh = RWKV_HEADS, RWKV_HEAD
    fwd = pl.BlockSpec((G * Cc, C), lambda b, j: (b * ng + j, 0))
    rev = pl.BlockSpec((G * Cc, C), lambda b, j: (b * ng + ng - 1 - j, 0))
    fwd2 = pl.BlockSpec((None, G * Cc, C), lambda b, j: (0, b * ng + j, 0))
    rev2 = pl.BlockSpec((None, G * Cc, C), lambda b, j: (1, b * ng + ng - 1 - j, 0))
    st = pl.BlockSpec((2, None, H, Nh, Nh), lambda b, j: (0, b, 0, 0, 0))
    yf, yr, sf = pl.pallas_call(
        functools.partial(_rwkv_scan_kernel, G=G),
        out_shape=(jax.ShapeDtypeStruct((N, C), F32), jax.ShapeDtypeStruct((N, C), F32),
                   jax.ShapeDtypeStruct((2, B, H, Nh, Nh), F32)),
        grid=(B, ng),
        in_specs=[fwd, fwd, fwd, fwd, fwd2, fwd2, rev, rev, rev, rev, rev2, rev2,
                  pl.BlockSpec((1, C), lambda b, j: (0, 0)), st],
        out_specs=(fwd, rev, st),
        scratch_shapes=[pltpu.VMEM((2, H, Nh, Nh), F32)],
        compiler_params=_cparams("arbitrary", "arbitrary"),
        name="rwkv_scan",
    )(r, k, v, kn, lw, a, r, k, v, kn, lw, a, k_a, s0)
    return yf, yr, sf


def _rwkv_post_kernel(yf_ref, yr_ref, bonus_ref, g_ref, gg_ref, gb_ref, bd_ref, o_ref):
    y = yf_ref[...] + yr_ref[...]
    bd = bd_ref[...]
    inv_n = 1.0 / RWKV_HEAD
    mean = _dot_exact_rhs(y, bd) * inv_n
    yc = y - mean
    var = _dot_exact_rhs(yc * yc, bd) * inv_n
    yn = yc * lax.rsqrt(var + GN_EPS) * gg_ref[...] + gb_ref[...]
    o_ref[...] = ((yn + bonus_ref[...]) * g_ref[...]).astype(o_ref.dtype)


def _rwkv_post(yf, yr, bonus, g, gn_g, gn_b, bd):
    N, C = yf.shape
    tm = _pick(N, (1024, 512, 256))
    row = pl.BlockSpec((tm, C), lambda i: (i, 0))
    vec = pl.BlockSpec((1, C), lambda i: (0, 0))
    return pl.pallas_call(
        _rwkv_post_kernel,
        out_shape=jax.ShapeDtypeStruct((N, C), BF16),
        grid=(N // tm,),
        in_specs=[row, row, row, row, vec, vec, pl.BlockSpec((C, C), lambda i: (0, 0))],
        out_specs=row,
        compiler_params=_cparams("arbitrary"),
        name="rwkv_post",
    )(yf, yr, bonus, g, gn_g.reshape(1, C), gn_b.reshape(1, C), bd)


def _fourier_ch_kernel(p_ref, m_ref, o_ref):
    p = p_ref[...]
    mat = m_ref[...]
    G = FOURIER_GROUP_DIM
    for g in range(p.shape[1] // G):
        o_ref[:, g * G:(g + 1) * G] = jnp.dot(p[:, g * G:(g + 1) * G], mat,
                                              preferred_element_type=F32).astype(o_ref.dtype)


def _fourier_channel(proj, col, Fd, ch_mats, B, T):
    tm = _pick(T, (1024, 512, 256))
    nt = T // tm
    G = FOURIER_GROUP_DIM
    return pl.pallas_call(
        _fourier_ch_kernel,
        out_shape=jax.ShapeDtypeStruct((2, T, B * Fd), BF16),
        grid=(B, nt, 2),
        in_specs=[pl.BlockSpec((tm, Fd), lambda b, i, s: (b * nt + i, col)),
                  pl.BlockSpec((None, G, G), lambda b, i, s: (s, 0, 0))],
        out_specs=pl.BlockSpec((None, tm, Fd), lambda b, i, s: (s, i, b)),
        compiler_params=_cparams("arbitrary", "arbitrary", "arbitrary"),
        name="fourier_channel",
    )(proj, ch_mats)


def _dft_mats(n):
    idx = jnp.arange(n, dtype=jnp.int32)
    ph = (idx[:, None] * idx[None, :]) % n
    ang = ph.astype(F32) * (2.0 * math.pi / n)
    s = 1.0 / math.sqrt(n)
    return jnp.cos(ang) * s, jnp.sin(ang) * s


def _merge_kernel(fa_ref, at_ref, rw_ref, gt_ref, x_ref, g1_ref, wa_ref, wb_ref, wc_ref, wo_ref, o_ref):
    D = x_ref.shape[1]
    ya = jnp.dot(fa_ref[...], wa_ref[...], preferred_element_type=F32)
    yb = jnp.dot(at_ref[0], wb_ref[0], preferred_element_type=F32)
    for h in range(1, at_ref.shape[0]):
        yb = yb + jnp.dot(at_ref[h], wb_ref[h], preferred_element_type=F32)
    yc = jnp.dot(rw_ref[...], wc_ref[...], preferred_element_type=F32)
    gt = gt_ref[...].astype(F32)
    merged = (_sigmoid(gt[:, :D]) * ya + _sigmoid(gt[:, D:2 * D]) * yb + _sigmoid(gt[:, 2 * D:]) * yc)
    out = jnp.dot(merged.astype(BF16), wo_ref[...], preferred_element_type=F32)
    o_ref[...] = x_ref[...] + g1_ref[...] * out


def _merge(fa, attn, rw, proj, gcol, x, g1, wa, wb, wc, wo, B, T):
    N, D = x.shape
    H = attn.shape[1]
    Fd = rw.shape[1]
    tm = _pick(T, (512, 256))
    nt = T // tm
    Bm = g1.shape[0]
    full = lambda shape: pl.BlockSpec(shape, lambda b, i: (0,) * len(shape))
    return pl.pallas_call(
        _merge_kernel,
        out_shape=jax.ShapeDtypeStruct((N, D), F32),
        grid=(B, nt),
        in_specs=[pl.BlockSpec((tm, Fd), lambda b, i: (i, b)),
                  pl.BlockSpec((None, H, tm, LANES), lambda b, i: (b, 0, i, 0)),
                  pl.BlockSpec((tm, Fd), lambda b, i: (b * nt + i, 0)),
                  pl.BlockSpec((tm, 3 * D), lambda b, i: (b * nt + i, gcol)),
                  pl.BlockSpec((tm, D), lambda b, i: (b * nt + i, 0)),
                  pl.BlockSpec((None, 1, D), lambda b, i: (b % Bm, 0, 0)),
                  full(wa.shape), full(wb.shape), full(wc.shape), full(wo.shape)],
        out_specs=pl.BlockSpec((tm, D), lambda b, i: (b * nt + i, 0)),
        compiler_params=_cparams("arbitrary", "arbitrary"),
        name="merge_out",
    )(fa, attn, rw, proj, x, g1, wa, wb, wc, wo)


def _route(scores, sel, n_exp):
    tm = scores.shape[0]
    lane = lax.broadcasted_iota(jnp.int32, (tm, LANES), 1)
    neg = -jnp.inf
    sel = jnp.where(lane < n_exp, sel, neg)
    epg = n_exp // N_GROUPS
    assert epg == 4
    odd = (lane % 2) == 1
    p1 = jnp.where(odd, pltpu.roll(sel, 1, axis=1), pltpu.roll(sel, LANES - 1, axis=1))
    hi1, lo1 = jnp.maximum(sel, p1), jnp.minimum(sel, p1)
    upper = (lane % 4) >= 2
    hi2 = jnp.where(upper, pltpu.roll(hi1, 2, axis=1), pltpu.roll(hi1, LANES - 2, axis=1))
    lo2 = jnp.where(upper, pltpu.roll(lo1, 2, axis=1), pltpu.roll(lo1, LANES - 2, axis=1))
    gscore = jnp.maximum(hi1, hi2) + jnp.maximum(jnp.minimum(hi1, hi2), jnp.maximum(lo1, lo2))
    gscore = jnp.where(lane < n_exp, gscore, neg)
    gmax = jnp.max(gscore, axis=1, keepdims=True)
    lane_f = lane.astype(F32)
    grp = (lane // epg).astype(F32)
    big = float(LANES)
    best = jnp.min(jnp.where(gscore == gmax, grp, big), axis=1, keepdims=True)
    masked = jnp.where(grp == best, sel, neg)
    m1 = jnp.max(masked, axis=1, keepdims=True)
    i1 = jnp.min(jnp.where(masked == m1, lane_f, big), axis=1, keepdims=True)
    masked2 = jnp.where(lane_f == i1, neg, masked)
    m2 = jnp.max(masked2, axis=1, keepdims=True)
    i2 = jnp.min(jnp.where(masked2 == m2, lane_f, big), axis=1, keepdims=True)
    pick1 = lane_f == i1
    pick2 = lane_f == i2
    s1 = jnp.sum(jnp.where(pick1, scores, 0.0), axis=1, keepdims=True)
    s2 = jnp.sum(jnp.where(pick2, scores, 0.0), axis=1, keepdims=True)
    return pick1, pick2, s1 / (s1 + s2), s2 / (s1 + s2)


def _lane_to_row(col_vals):
    ones = jnp.ones((8, LANES), BF16)
    hi, lo = _split2(col_vals)
    return (_dot_nt(ones, hi) + _dot_nt(ones, lo))[0:1]


def _moe_kernel(x_ref, g_ref, sc_ref, sh_ref, g2_ref, rw_ref, rb_ref,
                wg_ref, wu_ref, wd_ref, swg_ref, swu_ref, swd_ref, o_ref,
                m_ref, xs_ref, ys_ref, pm_ref, ws_ref, oc_ref, *, n_exp):
    e = pl.program_id(1)
    tm = x_ref.shape[0]
    S = 2 * tm
    RB = MOE_ROWS

    @pl.when(e == 0)
    def _():
        x = x_ref[...]
        m = x * lax.rsqrt(jnp.mean(x * x, axis=-1, keepdims=True) + NORM_EPS) * g_ref[...]
        m = m * (1.0 + sc_ref[...]) + sh_ref[...]
        mb = m.astype(BF16)
        m_ref[...] = mb
        scores = _sigmoid(_dot_split(m, rw_ref[...]))
        pick1, pick2, w1, w2 = _route(scores, scores + rb_ref[...], n_exp)
        p1 = pick1.astype(F32)
        p2 = pick2.astype(F32)
        chosen = p1 + p2
        cnt = jnp.sum(chosen, axis=0, keepdims=True)
        li = lax.broadcasted_iota(jnp.int32, (LANES, LANES), 0)
        lj = lax.broadcasted_iota(jnp.int32, (LANES, LANES), 1)
        off = _dot_exact_rhs(jnp.broadcast_to(cnt, (8, LANES)), (li < lj).astype(F32).astype(BF16))[0:1]
        ti = lax.broadcasted_iota(jnp.int32, (tm, tm), 0)
        tj = lax.broadcasted_iota(jnp.int32, (tm, tm), 1)
        rank = jnp.dot((tj < ti).astype(F32).astype(BF16), chosen.astype(BF16),
                       preferred_element_type=F32)
        dest = off + rank
        d1 = _lane_to_row(p1 * dest)
        d2 = _lane_to_row(p2 * dest)
        w1r = _lane_to_row(jnp.broadcast_to(w1, (tm, LANES)) * p1)
        w2r = _lane_to_row(jnp.broadcast_to(w2, (tm, LANES)) * p2)
        row = lax.broadcasted_iota(jnp.int32, (S, tm), 0).astype(F32)
        is1 = row == d1
        is2 = row == d2
        perm = jnp.where(is1, 1.0, jnp.where(is2, 1.0, 0.0)).astype(BF16)
        pm_ref[...] = perm
        ws_ref[...] = jnp.sum(jnp.where(is1, w1r, jnp.where(is2, w2r, 0.0)), axis=1, keepdims=True)
        xs_ref[...] = jnp.dot(perm, mb, preferred_element_type=F32).astype(BF16)
        ys_ref[...] = jnp.zeros_like(ys_ref)
        oc_ref[0:1, :] = off
        oc_ref[1:2, :] = cnt

    @pl.when(e < n_exp)
    def _():
        lane = lax.broadcasted_iota(jnp.int32, (1, LANES), 1)
        o_e = jnp.sum(jnp.where(lane == e, oc_ref[0:1, :], 0.0)).astype(jnp.int32)
        c_e = jnp.sum(jnp.where(lane == e, oc_ref[1:2, :], 0.0)).astype(jnp.int32)
        wg = wg_ref[...].astype(BF16)
        wu = wu_ref[...].astype(BF16)
        wd = wd_ref[...].astype(BF16)
        b0 = o_e // RB
        b1 = jnp.where(c_e > 0, (o_e + c_e + RB - 1) // RB, b0)

        def body(b, carry):
            r0 = pl.multiple_of(b * RB, RB)
            xb = xs_ref[pl.ds(r0, RB), :]
            hg = jnp.dot(xb, wg, preferred_element_type=F32)
            hu = jnp.dot(xb, wu, preferred_element_type=F32)
            rows = r0 + lax.broadcasted_iota(jnp.int32, (RB, 1), 0)
            mine = (rows >= o_e) & (rows < o_e + c_e)
            wrow = jnp.where(mine, ws_ref[pl.ds(r0, RB), :], 0.0)
            hh = hg * _sigmoid(hg) * hu * wrow
            ys_ref[pl.ds(r0, RB), :] += jnp.dot(hh.astype(BF16), wd, preferred_element_type=F32)
            return carry

        lax.fori_loop(b0, b1, body, 0)

    @pl.when(e == n_exp)
    def _():
        m = m_ref[...]
        hg = jnp.dot(m, swg_ref[...], preferred_element_type=F32)
        hu = jnp.dot(m, swu_ref[...], preferred_element_type=F32)
        hh = hg * _sigmoid(hg) * hu
        y = jnp.dot(hh.astype(BF16), swd_ref[...], preferred_element_type=F32)
        y = y + _dot_tn(pm_ref[...], ys_ref[...])
        o_ref[...] = x_ref[...] + g2_ref[...] * y


def _moe(x, g, sc, sh, g2, rw_pad, rb_pad, wg, wu, wd, swg, swu, swd):
    N, D = x.shape
    E, _, Ff = wg.shape
    Bm = sc.shape[0]
    rpm = N // Bm
    tm = min(rpm, 1024)
    assert rpm % tm == 0 and (2 * tm) % MOE_ROWS == 0
    one = pl.Buffered(1)
    mod = pl.BlockSpec((None, 1, D), lambda i, e: (i * tm // rpm, 0, 0))
    full = lambda shape: pl.BlockSpec(shape, lambda i, e: (0,) * len(shape), pipeline_mode=one)
    return pl.pallas_call(
        functools.partial(_moe_kernel, n_exp=E),
        out_shape=jax.ShapeDtypeStruct((N, D), F32),
        grid=(N // tm, E + 1),
        in_specs=[pl.BlockSpec((tm, D), lambda i, e: (i, 0), pipeline_mode=one),
                  full((1, D)), mod, mod, mod,
                  full(rw_pad.shape), full(rb_pad.shape),
                  pl.BlockSpec((None, D, Ff), lambda i, e: (jnp.minimum(e, E - 1), 0, 0)),
                  pl.BlockSpec((None, D, Ff), lambda i, e: (jnp.minimum(e, E - 1), 0, 0)),
                  pl.BlockSpec((None, Ff, D), lambda i, e: (jnp.minimum(e, E - 1), 0, 0)),
                  full(swg.shape), full(swu.shape), full(swd.shape)],
        out_specs=pl.BlockSpec((tm, D), lambda i, e: (i, 0), pipeline_mode=one),
        scratch_shapes=[pltpu.VMEM((tm, D), BF16), pltpu.VMEM((2 * tm, D), BF16),
                        pltpu.VMEM((2 * tm, D), F32), pltpu.VMEM((2 * tm, tm), BF16),
                        pltpu.VMEM((2 * tm, 1), F32), pltpu.VMEM((8, LANES), F32)],
        compiler_params=pltpu.CompilerParams(dimension_semantics=("arbitrary", "arbitrary"),
                                             vmem_limit_bytes=MOE_VMEM_LIMIT_BYTES),
        name="moe",
    )(x, g.reshape(1, D), sc, sh, g2, rw_pad, rb_pad, wg, wu, wd, swg, swu, swd)


def _rope_partner(w):
    lead = w.shape[:-1]
    wr = w.reshape(lead + (2, 2, ROPE_PAIRS))
    return jnp.stack([-wr[..., 1, :], wr[..., 0, :]], axis=-2).reshape(w.shape)


def _rope_tables(T):
    rows = T // GRID_W
    row = jnp.repeat(jnp.arange(rows), GRID_W)
    col = jnp.tile(jnp.arange(GRID_W), rows)
    inv = ROPE_BASE ** (-jnp.arange(ROPE_PAIRS, dtype=F32) / ROPE_PAIRS)
    ang = jnp.stack([row, col], axis=-1).astype(F32)[..., None] * inv
    cos = jnp.broadcast_to(jnp.cos(ang)[:, :, None, :], (T, 2, 2, ROPE_PAIRS)).reshape(T, MLA_ROPE)
    sin = jnp.broadcast_to(jnp.sin(ang)[:, :, None, :], (T, 2, 2, ROPE_PAIRS)).reshape(T, MLA_ROPE)
    pad = LANES - MLA_QK
    z_n, z_p = jnp.zeros((T, MLA_NOPE), F32), jnp.zeros((T, pad), F32)
    cq = jnp.concatenate([jnp.ones((T, MLA_NOPE), F32), cos, z_p], axis=1)
    sq = jnp.concatenate([z_n, sin, z_p], axis=1)
    ck = jnp.concatenate([z_n, cos, z_p], axis=1)
    return cq, sq, ck, sq


def _identity_tables(T):
    pad = LANES - MLA_QK
    z_n, z_p = jnp.zeros((T, MLA_NOPE), F32), jnp.zeros((T, pad), F32)
    one_r = jnp.ones((T, MLA_ROPE), F32)
    cq = jnp.concatenate([jnp.ones((T, MLA_NOPE), F32), one_r, z_p], axis=1)
    ck = jnp.concatenate([z_n, one_r, z_p], axis=1)
    zero = jnp.zeros((T, LANES), F32)
    return cq, zero, ck, zero


def kernel(x, c, ctx, c_ctx, w_ada, b_ada, norm1_g, norm2_g, w_in, q_norm_g, kv_norm_g, w_uq, w_ukv, rwkv_mu, rwkv_w0, rwkv_w_up, rwkv_a0, rwkv_a_up, rwkv_g_up, rwkv_k_k, rwkv_k_a, rwkv_r_k, rwkv_gn_g, rwkv_gn_b, w_proj_a, w_proj_b, w_proj_c, w_out, router_w, router_b, exp_w_gate, exp_w_up, exp_w_down, sh_w_gate, sh_w_up, sh_w_down, final_norm_g):
    B, T, D = x.shape
    Tc = ctx.shape[1]
    L = w_ada.shape[0]
    H = MLA_HEADS
    QL, KVL = q_norm_g.shape[1], kv_norm_g.shape[1]
    Fd = w_proj_a.shape[1]
    C = RWKV_DIM
    E = router_w.shape[1]
    N, Nc = B * T, B * Tc

    sizes = (Fd, QL, KVL, MLA_ROPE, 3 * C + DECAY_LORA + AAA_LORA + GATE_LORA, D, D, D)
    offs = np.concatenate([[0], np.cumsum(sizes)])

    lat_tabs = _rope_tables(T)
    ctx_tabs = _identity_tables(Tc)
    cc, cs = _dft_mats(FOURIER_GROUP_DIM)
    ch_mats = jnp.stack([cc, cs]).astype(BF16)
    tcos, tsin = _dft_mats(T)
    dft_lat = jnp.concatenate([tcos, -tsin], axis=1).astype(BF16)
    ccos, csin = _dft_mats(Tc)
    dft_ctx = jnp.concatenate([ccos, -csin], axis=1).astype(BF16)
    hid = jnp.arange(C) // RWKV_HEAD
    bd = (hid[:, None] == hid[None, :]).astype(BF16)
    rw_pad = jnp.zeros((D, LANES), F32).at[:, :E].set(router_w)
    rb_pad = jnp.zeros((1, LANES), F32).at[0, :E].set(router_b)

    R = -(-(B + 1) // 8) * 8
    c_rows = jnp.zeros((R, D), F32).at[:B].set(c).at[B].set(c_ctx)
    mod = _ada_mod(c_rows, w_ada, b_ada)

    x = x.reshape(N, D)
    ctx = ctx.reshape(Nc, D)
    zero_state = jnp.zeros((2, B, RWKV_HEADS, RWKV_HEAD, RWKV_HEAD), F32)

    for l in range(L):
        last = l == L - 1
        ml = mod[l, :B].reshape(B, 1, 6, D)
        mc = mod[l, B:B + 1].reshape(1, 1, 6, D)
        sh1, sc1, g1, sh2, sc2, g2 = [ml[:, :, j] for j in range(6)]
        csh1, csc1, cg1, csh2, csc2, cg2 = [mc[:, :, j] for j in range(6)]

        wi = w_in[l]
        w_kr = wi[:, offs[3]:offs[4]]
        zl = jnp.zeros((D, MLA_NOPE), F32)
        zr = jnp.zeros((D, LANES - MLA_QK), F32)
        segs = [wi[:, offs[5]:offs[8]],
                wi[:, offs[0]:offs[1]],
                wi[:, offs[4]:offs[5]],
                jnp.concatenate([wi[:, offs[1]:offs[3]], zl, w_kr, zr, zl, _rope_partner(w_kr), zr], axis=1)]
        seg_off, n_all = _column_layout([s.shape[1] for s in segs])
        w_all = jnp.zeros((D, n_all), BF16)
        for s, o in zip(segs, seg_off):
            w_all = w_all.at[:, o:o + s.shape[1]].set(s.astype(BF16))
        gt_col, pf_col, rw_col, qkv_col = [o // s.shape[1] for s, o in zip(segs, seg_off)]
        uq = w_uq[l].reshape(QL, H, MLA_QK).transpose(1, 0, 2)
        padq = jnp.zeros((H, QL, LANES - MLA_QK), F32)
        wqa = jnp.concatenate([uq, padq], axis=2).astype(BF16)
        wqb = jnp.concatenate([jnp.zeros((H, QL, MLA_NOPE), F32), _rope_partner(uq[..., MLA_NOPE:]), padq],
                              axis=2).astype(BF16)
        ukv = w_ukv[l].reshape(KVL, H, MLA_NOPE + MLA_V).transpose(1, 0, 2)
        wk = jnp.concatenate([ukv[..., :MLA_NOPE], jnp.zeros((H, KVL, LANES - MLA_NOPE), F32)], axis=2).astype(BF16)
        wv = jnp.concatenate([ukv[..., MLA_NOPE:], jnp.zeros((H, KVL, LANES - MLA_V), F32)], axis=2).astype(BF16)
        wb = jnp.concatenate([w_proj_b[l].reshape(H, MLA_V, D), jnp.zeros((H, LANES - MLA_V, D), F32)],
                             axis=1).astype(BF16)
        wa = w_proj_a[l].astype(BF16)
        wc = w_proj_c[l].astype(BF16)
        wo = w_out[l].astype(BF16)
        zd = jnp.zeros((2, DECAY_LORA, C), F32)
        prm = dict(
            mu=rwkv_mu[l].reshape(1, -1), w0=rwkv_w0[l], a0=rwkv_a0[l],
            wup=jnp.concatenate([rwkv_w_up[l], zd], axis=1).astype(BF16),
            aup=jnp.concatenate([zd, rwkv_a_up[l]], axis=1).astype(BF16),
            gup=rwkv_g_up[l].astype(BF16), k_k=rwkv_k_k[l].reshape(1, C), k_a=rwkv_k_a[l].reshape(1, C),
            r_k=rwkv_r_k[l].reshape(1, C), bd=bd)

        def mixers(xx, sc, sh, tabs, Tx):
            proj = _in_proj(xx, norm1_g[l], sc, sh, w_all)
            q, k, v = _qkv_project(proj, qkv_col, q_norm_g[l], kv_norm_g[l], wqa, wqb, wk, wv, tabs, B, Tx)
            rp = _rwkv_prep(proj, rw_col, prm, B, Tx)
            return proj, q, k, v, rp

        proj_l, q_l, k_l, v_l, rp_l = mixers(x, sc1, sh1, lat_tabs, T)
        proj_c, q_c, k_c, v_c, rp_c = mixers(ctx, csc1, csh1, ctx_tabs, Tc)

        attn_l = _attention(q_l, [(k_l, v_l), (k_c, v_c)])

        yf_c, yr_c, s_ctx = _rwkv_scan(*rp_c[:6], prm["k_a"], zero_state, B, Tc)
        yf_l, yr_l, _ = _rwkv_scan(*rp_l[:6], prm["k_a"], s_ctx, B, T)
        rwo_l = _rwkv_post(yf_l, yr_l, rp_l[7], rp_l[6], rwkv_gn_g[l], rwkv_gn_b[l], bd)

        zcs_l = _fourier_channel(proj_l, pf_col, Fd, ch_mats, B, T).reshape(2 * T, B * Fd)
        fa_l = _mm(dft_lat, zcs_l, BF16, tn=B * Fd, tk=min(2 * T, 1024))

        x = _merge(fa_l, attn_l, rwo_l, proj_l, gt_col, x, g1, wa, wb, wc, wo, B, T)
        moe_w = (rw_pad, rb_pad, exp_w_gate[l], exp_w_up[l], exp_w_down[l],
                 sh_w_gate[l].astype(BF16), sh_w_up[l].astype(BF16), sh_w_down[l].astype(BF16))
        x = _moe(x, norm2_g[l], sc2, sh2, g2, *moe_w)

        if not last:
            attn_c = _attention(q_c, [(k_c, v_c)])
            rwo_c = _rwkv_post(yf_c, yr_c, rp_c[7], rp_c[6], rwkv_gn_g[l], rwkv_gn_b[l], bd)
            zcs_c = _fourier_channel(proj_c, pf_col, Fd, ch_mats, B, Tc).reshape(2 * Tc, B * Fd)
            fa_c = _mm(dft_ctx, zcs_c, BF16, tn=B * Fd, tk=min(2 * Tc, 1024))
            ctx = _merge(fa_c, attn_c, rwo_c, proj_c, gt_col, ctx, cg1, wa, wb, wc, wo, B, Tc)
            ctx = _moe(ctx, norm2_g[l], csc2, csh2, cg2, *moe_w)

    ones = jnp.ones((1, 1, D), F32)
    out = _norm(x, final_norm_g, ones, ones, F32, modulate=False)
    return out.reshape(B, T, D)
```

```python
import functools
import math

import numpy as np
import jax
import jax.numpy as jnp
from jax import lax
from jax.experimental import pallas as pl
from jax.experimental.pallas import tpu as pltpu

F32 = jnp.float32
BF16 = jnp.bfloat16

GRID_W = 64
FOURIER_GROUP_DIM = 128
MLA_HEADS = 8
MLA_NOPE = 64
MLA_ROPE = 32
MLA_QK = MLA_NOPE + MLA_ROPE
MLA_V = 64
MLA_SCALE = MLA_QK ** -0.5
QK_PRESCALE = MLA_SCALE * math.log2(math.e)
ROPE_PAIRS = MLA_ROPE // 4
ROPE_BASE = 10000.0
RWKV_HEADS = 8
RWKV_HEAD = 64
RWKV_DIM = RWKV_HEADS * RWKV_HEAD
DECAY_LORA = 64
AAA_LORA = 64
GATE_LORA = 128
GN_EPS = 64e-5
N_GROUPS = 4
NORM_EPS = 1e-6

LANES = 128
VMEM_LIMIT_BYTES = 48 * 1024 * 1024
RWKV_CHUNK = 64
MOE_ROWS = 128
MOE_VMEM_LIMIT_BYTES = 56 * 1024 * 1024


def _cparams(*sem):
    return pltpu.CompilerParams(dimension_semantics=sem, vmem_limit_bytes=VMEM_LIMIT_BYTES)


def _sigmoid(z):
    return 1.0 / (1.0 + jnp.exp(-z))


def _bdot(a, b):
    return jnp.dot(a.astype(BF16), b.astype(BF16), preferred_element_type=F32)


def _dot_nt(a, b):
    return lax.dot_general(a.astype(BF16), b.astype(BF16), (((1,), (1,)), ((), ())),
                           preferred_element_type=F32)


def _dot_tn(a, b):
    return lax.dot_general(a.astype(BF16), b.astype(BF16), (((0,), (0,)), ((), ())),
                           preferred_element_type=F32)


def _split2(x):
    hi = x.astype(BF16)
    lo = (x - hi.astype(F32)).astype(BF16)
    return hi, lo


def _dot_exact_rhs(x, ones_bf16):
    hi, lo = _split2(x)
    return (jnp.dot(hi, ones_bf16, preferred_element_type=F32)
            + jnp.dot(lo, ones_bf16, preferred_element_type=F32))


def _dot_split(a, b):
    ah, al = _split2(a)
    bh, bl = _split2(b)
    return (jnp.dot(ah, bh, preferred_element_type=F32)
            + jnp.dot(ah, bl, preferred_element_type=F32)
            + jnp.dot(al, bh, preferred_element_type=F32))


def _ada_kernel(c_ref, w_ref, b_ref, o_ref):
    c = c_ref[...]
    s = c * _sigmoid(c)
    o_ref[...] = _dot_split(s, w_ref[...]) + b_ref[...]


def _ada_mod(c_rows, w_ada, b_ada):
    L, D, D6 = w_ada.shape
    R = c_rows.shape[0]
    tn = 1536 if D6 % 1536 == 0 else D6
    return pl.pallas_call(
        _ada_kernel,
        out_shape=jax.ShapeDtypeStruct((L, R, D6), F32),
        grid=(L, D6 // tn),
        in_specs=[pl.BlockSpec((R, D), lambda l, j: (0, 0)),
                  pl.BlockSpec((None, D, tn), lambda l, j: (l, 0, j)),
                  pl.BlockSpec((None, 1, tn), lambda l, j: (l, 0, j))],
        out_specs=pl.BlockSpec((None, R, tn), lambda l, j: (l, 0, j)),
        compiler_params=_cparams("arbitrary", "arbitrary"),
        name="ada_mod",
    )(c_rows, w_ada, b_ada.reshape(L, 1, D6))


def _norm_kernel(x_ref, g_ref, sc_ref, sh_ref, o_ref, *, modulate):
    x = x_ref[...].astype(F32)
    y = x * lax.rsqrt(jnp.mean(x * x, axis=-1, keepdims=True) + NORM_EPS) * g_ref[...]
    if modulate:
        y = y * (1.0 + sc_ref[...]) + sh_ref[...]
    o_ref[...] = y.astype(o_ref.dtype)


def _norm(x, g, sc, sh, out_dtype, modulate=True):
    N, D = x.shape
    Bm = sc.shape[0]
    rpm = N // Bm
    tm = min(rpm, 1024)
    assert rpm % tm == 0
    return pl.pallas_call(
        functools.partial(_norm_kernel, modulate=modulate),
        out_shape=jax.ShapeDtypeStruct((N, D), out_dtype),
        grid=(N // tm,),
        in_specs=[pl.BlockSpec((tm, D), lambda i: (i, 0)),
                  pl.BlockSpec((1, D), lambda i: (0, 0)),
                  pl.BlockSpec((None, 1, D), lambda i: (i * tm // rpm, 0, 0)),
                  pl.BlockSpec((None, 1, D), lambda i: (i * tm // rpm, 0, 0))],
        out_specs=pl.BlockSpec((tm, D), lambda i: (i, 0)),
        compiler_params=_cparams("arbitrary"),
        name="rmsnorm_mod",
    )(x, g.reshape(1, D), sc, sh)


def _in_proj_kernel(x_ref, g_ref, sc_ref, sh_ref, w_ref, o_ref, h_ref):
    @pl.when(pl.program_id(1) == 0)
    def _():
        x = x_ref[...]
        y = x * lax.rsqrt(jnp.mean(x * x, axis=-1, keepdims=True) + NORM_EPS) * g_ref[...]
        h_ref[...] = (y * (1.0 + sc_ref[...]) + sh_ref[...]).astype(BF16)

    o_ref[...] = jnp.dot(h_ref[...], w_ref[...], preferred_element_type=F32).astype(o_ref.dtype)


def _in_proj(x, g, sc, sh, w_all):
    N, D = x.shape
    Nout = w_all.shape[1]
    Bm = sc.shape[0]
    rpm = N // Bm
    tm = min(rpm, 1024)
    assert rpm % tm == 0
    tn = _pick(Nout, (896, 512, 256, 128))
    mod = pl.BlockSpec((None, 1, D), lambda i, j: (i * tm // rpm, 0, 0))
    return pl.pallas_call(
        _in_proj_kernel,
        out_shape=jax.ShapeDtypeStruct((N, Nout), BF16),
        grid=(N // tm, Nout // tn),
        in_specs=[pl.BlockSpec((tm, D), lambda i, j: (i, 0)),
                  pl.BlockSpec((1, D), lambda i, j: (0, 0)), mod, mod,
                  pl.BlockSpec((D, tn), lambda i, j: (0, j))],
        out_specs=pl.BlockSpec((tm, tn), lambda i, j: (i, j)),
        scratch_shapes=[pltpu.VMEM((tm, D), BF16)],
        compiler_params=_cparams("arbitrary", "arbitrary"),
        name="in_proj",
    )(x, g.reshape(1, D), sc, sh, w_all)


def _column_layout(widths):
    offs, cur = [], 0
    for w in widths:
        cur = -(-cur // w) * w
        offs.append(cur)
        cur += w
    return offs, -(-cur // LANES) * LANES


def _mm_kernel(a_ref, b_ref, o_ref):
    o_ref[...] = jnp.dot(a_ref[...], b_ref[...], preferred_element_type=F32).astype(o_ref.dtype)


def _mm_acc_kernel(a_ref, b_ref, o_ref, acc_ref, *, nk):
    k = pl.program_id(2)

    @pl.when(k == 0)
    def _():
        acc_ref[...] = jnp.zeros_like(acc_ref)

    acc_ref[...] += jnp.dot(a_ref[...], b_ref[...], preferred_element_type=F32)

    @pl.when(k == nk - 1)
    def _():
        o_ref[...] = acc_ref[...].astype(o_ref.dtype)


def _pick(n, prefs):
    for p in prefs:
        if n % p == 0:
            return p
    return n


def _mm(a, w, out_dtype, tn=None, tk=None):
    M, K = a.shape
    Nn = w.shape[1]
    tm = _pick(M, (1024, 512, 256))
    tn = tn or _pick(Nn, (1024, 896, 512))
    tk = tk or K
    if tk == K:
        return pl.pallas_call(
            _mm_kernel,
            out_shape=jax.ShapeDtypeStruct((M, Nn), out_dtype),
            grid=(M // tm, Nn // tn),
            in_specs=[pl.BlockSpec((tm, K), lambda i, j: (i, 0)),
                      pl.BlockSpec((K, tn), lambda i, j: (0, j))],
            out_specs=pl.BlockSpec((tm, tn), lambda i, j: (i, j)),
            compiler_params=_cparams("arbitrary", "arbitrary"),
            name="matmul",
        )(a, w)
    nk = K // tk
    return pl.pallas_call(
        functools.partial(_mm_acc_kernel, nk=nk),
        out_shape=jax.ShapeDtypeStruct((M, Nn), out_dtype),
        grid=(M // tm, Nn // tn, nk),
        in_specs=[pl.BlockSpec((tm, tk), lambda i, j, k: (i, k)),
                  pl.BlockSpec((tk, tn), lambda i, j, k: (k, j))],
        out_specs=pl.BlockSpec((tm, tn), lambda i, j, k: (i, j)),
        scratch_shapes=[pltpu.VMEM((tm, tn), F32)],
        compiler_params=_cparams("arbitrary", "arbitrary", "arbitrary"),
        name="matmul_kacc",
    )(a, w)


def _qkv_kernel(p_ref, gq_ref, gkv_ref, wqa_ref, wqb_ref, wk_ref, wv_ref,
                cq_ref, sq_ref, ck_ref, sk_ref, q_ref, k_ref, v_ref, *, ql, kvl):
    p = p_ref[...].astype(F32)
    pq = p[:, :ql]
    pkv = p[:, ql:ql + kvl]
    kra = p[:, ql + kvl:ql + kvl + LANES]
    krb = p[:, ql + kvl + LANES:ql + kvl + 2 * LANES]
    qn = (pq * lax.rsqrt(jnp.mean(pq * pq, axis=-1, keepdims=True) + NORM_EPS) * gq_ref[...]).astype(BF16)
    kvn = (pkv * lax.rsqrt(jnp.mean(pkv * pkv, axis=-1, keepdims=True) + NORM_EPS) * gkv_ref[...]).astype(BF16)
    kr = kra * ck_ref[...] + krb * sk_ref[...]
    cq = cq_ref[...] * QK_PRESCALE
    sq = sq_ref[...] * QK_PRESCALE
    for h in range(q_ref.shape[0]):
        qa = jnp.dot(qn, wqa_ref[h], preferred_element_type=F32)
        qb = jnp.dot(qn, wqb_ref[h], preferred_element_type=F32)
        q_ref[h] = (qa * cq + qb * sq).astype(q_ref.dtype)
        k_ref[h] = (jnp.dot(kvn, wk_ref[h], preferred_element_type=F32) + kr).astype(k_ref.dtype)
        v_ref[h] = jnp.dot(kvn, wv_ref[h], preferred_element_type=F32).astype(v_ref.dtype)


def _qkv_project(proj, col, gq, gkv, wqa, wqb, wk, wv, tabs, B, T):
    N = proj.shape[0]
    H = wqa.shape[0]
    ql, kvl = wqa.shape[1], wk.shape[1]
    W = ql + kvl + 2 * LANES
    tm = _pick(T, (512, 256))
    nt = T // tm
    cq, sq, ck, sk = tabs
    full = lambda shape: pl.BlockSpec(shape, lambda i: (0,) * len(shape))
    tab_spec = pl.BlockSpec((tm, LANES), lambda i: (i % nt, 0))
    out_spec = pl.BlockSpec((None, H, tm, LANES), lambda i: (i // nt, 0, i % nt, 0))
    out_sds = jax.ShapeDtypeStruct((B, H, T, LANES), BF16)
    return pl.pallas_call(
        functools.partial(_qkv_kernel, ql=ql, kvl=kvl),
        out_shape=(out_sds, out_sds, out_sds),
        grid=(N // tm,),
        in_specs=[pl.BlockSpec((tm, W), lambda i: (i, col)),
                  full((1, ql)), full((1, kvl)),
                  full(wqa.shape), full(wqb.shape), full(wk.shape), full(wv.shape),
                  tab_spec, tab_spec, tab_spec, tab_spec],
        out_specs=(out_spec, out_spec, out_spec),
        compiler_params=_cparams("arbitrary"),
        name="mla_qkv",
    )(proj, gq.reshape(1, ql), gkv.reshape(1, kvl), wqa, wqb, wk, wv, cq, sq, ck, sk)


def _attn_kernel(q_ref, *refs, tiles):
    o_ref = refs[-1]
    q = q_ref[...]
    m = l = acc = None
    for src, start, size in tiles:
        k = refs[2 * src][start:start + size, :]
        v = refs[2 * src + 1][start:start + size, :]
        s = _dot_nt(q, k)
        mt = jnp.max(s, axis=-1, keepdims=True)
        if m is None:
            m_new = mt
            p = jnp.exp2(s - m_new)
            l = jnp.sum(p, axis=-1, keepdims=True)
            acc = jnp.dot(p.astype(BF16), v, preferred_element_type=F32)
        else:
            m_new = jnp.maximum(m, mt)
            alpha = jnp.exp2(m - m_new)
            p = jnp.exp2(s - m_new)
            l = alpha * l + jnp.sum(p, axis=-1, keepdims=True)
            acc = alpha * acc + jnp.dot(p.astype(BF16), v, preferred_element_type=F32)
        m = m_new
    o_ref[...] = (acc / l).astype(o_ref.dtype)


def _attention(q, kvs):
    B, H, T, dq = q.shape
    tq = _pick(T, (512, 256))
    tiles = []
    for src, (k, _) in enumerate(kvs):
        Tk = k.shape[2]
        tk = _pick(Tk, (1024, 512, 256))
        tiles += [(src, st, tk) for st in range(0, Tk, tk)]
    in_specs = [pl.BlockSpec((None, None, tq, dq), lambda b, h, i: (b, h, i, 0))]
    args = [q]
    for k, v in kvs:
        Tk = k.shape[2]
        in_specs += [pl.BlockSpec((None, None, Tk, dq), lambda b, h, i: (b, h, 0, 0)),
                     pl.BlockSpec((None, None, Tk, LANES), lambda b, h, i: (b, h, 0, 0))]
        args += [k, v]
    return pl.pallas_call(
        functools.partial(_attn_kernel, tiles=tuple(tiles)),
        out_shape=jax.ShapeDtypeStruct((B, H, T, LANES), BF16),
        grid=(B, H, T // tq),
        in_specs=in_specs,
        out_specs=pl.BlockSpec((None, None, tq, LANES), lambda b, h, i: (b, h, i, 0)),
        compiler_params=_cparams("arbitrary", "arbitrary", "arbitrary"),
        name="attention",
    )(*args)


def _softplus(u):
    return jnp.maximum(u, 0.0) + jnp.log(1.0 + jnp.exp(-jnp.abs(u)))


def _rwkv_prep_kernel(p_ref, pp_ref, pn_ref, mu_ref, w0_ref, wup_ref, a0_ref, aup_ref, gup_ref,
                      kk_ref, ka_ref, rk_ref, bd_ref,
                      r_o, k_o, v_o, kn_o, lw_o, a_o, g_o, bonus_o, *, tps):
    i = pl.program_id(0)
    p = p_ref[...].astype(F32)
    tt = p.shape[0]
    C = RWKV_DIM
    halo = pp_ref.shape[0]
    prev_row = jnp.where(i % tps == 0, 0.0, pp_ref[...].astype(F32)[halo - 1:halo, :])
    next_row = jnp.where(i % tps == tps - 1, 0.0, pn_ref[...].astype(F32)[0:1, :])
    rows = lax.broadcasted_iota(jnp.int32, (tt, 1), 0)
    pprev = jnp.where(rows == 0, prev_row, pltpu.roll(p, 1, axis=0))
    pnext = jnp.where(rows == tt - 1, next_row, pltpu.roll(p, tt - 1, axis=0))
    xs = p + (0.5 * (pprev + pnext) - p) * mu_ref[...]
    r = xs[:, :C]
    k = xs[:, C:2 * C]
    v = xs[:, 2 * C:3 * C]
    xwa = xs[:, 3 * C:3 * C + DECAY_LORA + AAA_LORA]
    xg = xs[:, 3 * C + DECAY_LORA + AAA_LORA:]
    bd = bd_ref[...]
    g_o[...] = _bdot(_sigmoid(xg), gup_ref[...]).astype(g_o.dtype)
    kn = k * kk_ref[...]
    kn = kn * lax.rsqrt(_dot_exact_rhs(kn * kn, bd) + 1e-12)
    th = jnp.tanh(xwa)
    bonus = jnp.zeros_like(r)
    for d in range(2):
        w = -_softplus(-(w0_ref[d:d + 1, :] + _bdot(th, wup_ref[d]))) - 0.5
        lw_o[d] = -jnp.exp(w)
        a = _sigmoid(a0_ref[d:d + 1, :] + _bdot(xwa, aup_ref[d]))
        a_o[d] = a.astype(a_o.dtype)
        kd = k * (1.0 + (a - 1.0) * ka_ref[...])
        bonus = bonus + _dot_exact_rhs(r * kd * rk_ref[...], bd) * v
    r_o[...] = r.astype(r_o.dtype)
    k_o[...] = k.astype(k_o.dtype)
    v_o[...] = v.astype(v_o.dtype)
    kn_o[...] = kn.astype(kn_o.dtype)
    bonus_o[...] = bonus.astype(bonus_o.dtype)


def _rwkv_prep(proj, col, prm, B, T):
    N = proj.shape[0]
    W = prm["mu"].shape[1]
    C = RWKV_DIM
    tt = _pick(T, (256,))
    tps = T // tt
    halo = 16
    nh = N // halo
    full = lambda shape: pl.BlockSpec(shape, lambda i: (0,) * len(shape))
    o_spec = pl.BlockSpec((tt, C), lambda i: (i, 0))
    o2_spec = pl.BlockSpec((2, tt, C), lambda i: (0, i, 0))
    sds = jax.ShapeDtypeStruct((N, C), BF16)
    lw_sds = jax.ShapeDtypeStruct((2, N, C), F32)
    a_sds = jax.ShapeDtypeStruct((2, N, C), BF16)
    return pl.pallas_call(
        functools.partial(_rwkv_prep_kernel, tps=tps),
        out_shape=(sds, sds, sds, sds, lw_sds, a_sds, sds, sds),
        grid=(N // tt,),
        in_specs=[pl.BlockSpec((tt, W), lambda i: (i, col)),
                  pl.BlockSpec((halo, W), lambda i: (jnp.maximum(i * (tt // halo) - 1, 0), col)),
                  pl.BlockSpec((halo, W), lambda i: (jnp.minimum((i + 1) * (tt // halo), nh - 1), col)),
                  full((1, W)), full((2, C)), full((2, LANES, C)), full((2, C)), full((2, LANES, C)),
                  full((GATE_LORA, C)), full((1, C)), full((1, C)), full((1, C)), full((C, C))],
        out_specs=(o_spec, o_spec, o_spec, o_spec, o2_spec, o2_spec, o_spec, o_spec),
        compiler_params=_cparams("arbitrary"),
        name="rwkv_prep",
    )(proj, proj, proj, prm["mu"], prm["w0"], prm["wup"], prm["a0"], prm["aup"], prm["gup"],
      prm["k_k"], prm["k_a"], prm["r_k"], prm["bd"])


def _bmm(a, b):
    return lax.dot_general(a.astype(BF16), b.astype(BF16), (((2,), (1,)), ((0,), (0,))),
                           preferred_element_type=F32)


def _bmm_nt(a, b):
    return lax.dot_general(a.astype(BF16), b.astype(BF16), (((2,), (2,)), ((0,), (0,))),
                           preferred_element_type=F32)


def _bmm_tn(a, b):
    return lax.dot_general(a.astype(BF16), b.astype(BF16), (((1,), (1,)), ((0,), (0,))),
                           preferred_element_type=F32)


def _rwkv_scan_kernel(rf_ref, kf_ref, vf_ref, knf_ref, lwf_ref, af_ref,
                      rr_ref, kr_ref, vr_ref, knr_ref, lwr_ref, ar_ref, ka_ref, s0_ref,
                      yf_ref, yr_ref, sf_ref, S_ref, *, G):
    j = pl.program_id(1)

    @pl.when(j == 0)
    def _():
        S_ref[...] = s0_ref[...]

    Cc, N, H = RWKV_CHUNK, RWKV_HEAD, RWKV_HEADS
    R = G * Cc
    ti = lax.broadcasted_iota(jnp.int32, (R, R), 0)
    ii = lax.broadcasted_iota(jnp.int32, (R, R), 1)
    same = (ti // Cc) == (ii // Cc)
    m_all = same.astype(F32).astype(BF16)

    def heads(x):
        return jnp.stack([x[g * Cc:(g + 1) * Cc, h * N:(h + 1) * N] for g in range(G) for h in range(H)])

    per_dir = []
    for d, refs in enumerate(((rf_ref, kf_ref, vf_ref, knf_ref, lwf_ref, af_ref),
                              (rr_ref, kr_ref, vr_ref, knr_ref, lwr_ref, ar_ref))):
        r, k, v, kn, lw, a = [x[...].astype(F32) for x in refs]
        before = (ii <= ti) if d == 0 else (ii >= ti)
        m_incl = (same & before).astype(F32).astype(BF16)
        h1 = lw.astype(BF16)
        r1 = lw - h1.astype(F32)
        h2 = r1.astype(BF16)
        h3 = (r1 - h2.astype(F32)).astype(BF16)
        cum = (jnp.dot(m_incl, h1, preferred_element_type=F32)
               + jnp.dot(m_incl, h2, preferred_element_type=F32)
               + jnp.dot(m_incl, h3, preferred_element_type=F32))
        tot = (jnp.dot(m_all, h1, preferred_element_type=F32)
               + jnp.dot(m_all, h2, preferred_element_type=F32)
               + jnp.dot(m_all, h3, preferred_element_type=F32))
        e_neg = jnp.exp(-cum)
        e_rem = jnp.exp(tot - cum)
        b = kn * a
        kd = k * (1.0 + (a - 1.0) * ka_ref[...])
        per_dir.append([heads(x) for x in (-kn * jnp.exp(cum - lw), r * jnp.exp(cum), b * e_neg, kd * e_neg,
                                           b * e_rem, kd * e_rem, v, jnp.exp(tot))])

    A, Rt, Bt, Kt, Bh, Kh, V, Wc = [jnp.concatenate([per_dir[0][i], per_dir[1][i]], axis=0) for i in range(8)]
    nb = 2 * G * H
    bi = lax.broadcasted_iota(jnp.int32, (nb, Cc, Cc), 0)
    t3 = lax.broadcasted_iota(jnp.int32, (nb, Cc, Cc), 1)
    i3 = lax.broadcasted_iota(jnp.int32, (nb, Cc, Cc), 2)
    order = jnp.where(bi < G * H, t3 - i3, i3 - t3)
    strict = order > 0
    incl = order >= 0

    AR = jnp.concatenate([A, Rt], axis=1)
    ab = _bmm_nt(AR, Bt)
    ak = _bmm_nt(AR, Kt)
    Lab = jnp.where(strict, ab[:, :Cc], 0.0)
    Mrb = jnp.where(incl, ab[:, Cc:], 0.0)
    Lak = jnp.where(strict, ak[:, :Cc], 0.0)
    Mrk = jnp.where(incl, ak[:, Cc:], 0.0)
    lv = _bmm(jnp.concatenate([Lak, Mrk], axis=1), V)
    Z = jnp.concatenate([A, lv[:, :Cc]], axis=2)
    P = Lab
    steps = int(math.log2(Cc))
    for s in range(steps):
        if s < steps - 1:
            out = _bmm(P, jnp.concatenate([Z, P], axis=2))
            Z = Z + out[:, :, :2 * N]
            P = out[:, :, 2 * N:]
        else:
            Z = Z + _bmm(P, Z)
    mz = _bmm(Mrb, Z)
    Rh = Rt + mz[:, :, :N]
    Yh = mz[:, :, N:] + lv[:, Cc:]
    gh = _bmm_tn(Z, Bh)
    Gm = gh[:, :N]
    Hm = gh[:, N:] + _bmm_tn(V, Kh)

    S = S_ref[...].reshape(2 * H, N, N)
    for g in range(G):
        gr = G - 1 - g
        pick = lambda x: jnp.concatenate([x[g * H:(g + 1) * H], x[(G + gr) * H:(G + gr + 1) * H]], axis=0)
        y = _bmm_nt(pick(Rh), S) + pick(Yh)
        S = S * pick(Wc)[:, 0:1, :] + _bmm(S, pick(Gm)) + pick(Hm)
        yf_ref[g * Cc:(g + 1) * Cc, :] = jnp.concatenate([y[h] for h in range(H)], axis=1)
        yr_ref[gr * Cc:(gr + 1) * Cc, :] = jnp.concatenate([y[H + h] for h in range(H)], axis=1)
    S = S.reshape(2, H, N, N)
    S_ref[...] = S
    sf_ref[...] = S


def _rwkv_scan(r, k, v, kn, lw, a, k_a, s0, B, T):
    N, C = r.shape
    Cc = RWKV_CHUNK
    nc = T // Cc
    G = 2 if nc % 2 == 0 else 1
    ng = nc // G
    H, Nh = RWKV_HEADS, RWKV_HEAD
    fwd = pl.BlockSpec((G * Cc, C), lambda b, j: (b * ng + j, 0))
    rev = pl.BlockSpec((G * Cc, C), lambda b, j: (b * ng + ng - 1 - j, 0))
    fwd2 = pl.BlockSpec((None, G * Cc, C), lambda b, j: (0, b * ng + j, 0))
    rev2 = pl.BlockSpec((None, G * Cc, C), lambda b, j: (1, b * ng + ng - 1 - j, 0))
    st = pl.BlockSpec((2, None, H, Nh, Nh), lambda b, j: (0, b, 0, 0, 0))
    yf, yr, sf = pl.pallas_call(
        functools.partial(_rwkv_scan_kernel, G=G),
        out_shape=(jax.ShapeDtypeStruct((N, C), F32), jax.ShapeDtypeStruct((N, C), F32),
                   jax.ShapeDtypeStruct((2, B, H, Nh, Nh), F32)),
        grid=(B, ng),
        in_specs=[fwd, fwd, fwd, fwd, fwd2, fwd2, rev, rev, rev, rev, rev2, rev2,
                  pl.BlockSpec((1, C), lambda b, j: (0, 0)), st],
        out_specs=(fwd, rev, st),
        scratch_shapes=[pltpu.VMEM((2, H, Nh, Nh), F32)],
        compiler_params=_cparams("arbitrary", "arbitrary"),
        name="rwkv_scan",
    )(r, k, v, kn, lw, a, r, k, v, kn, lw, a, k_a, s0)
    return yf, yr, sf


def _rwkv_post_kernel(yf_ref, yr_ref, bonus_ref, g_ref, gg_ref, gb_ref, bd_ref, o_ref):
    y = yf_ref[...] + yr_ref[...]
    bd = bd_ref[...]
    inv_n = 1.0 / RWKV_HEAD
    mean = _dot_exact_rhs(y, bd) * inv_n
    yc = y - mean
    var = _dot_exact_rhs(yc * yc, bd) * inv_n
    yn = yc * lax.rsqrt(var + GN_EPS) * gg_ref[...] + gb_ref[...]
    o_ref[...] = ((yn + bonus_ref[...].astype(F32)) * g_ref[...].astype(F32)).astype(o_ref.dtype)


def _rwkv_post(yf, yr, bonus, g, gn_g, gn_b, bd):
    N, C = yf.shape
    tm = _pick(N, (1024, 512, 256))
    row = pl.BlockSpec((tm, C), lambda i: (i, 0))
    vec = pl.BlockSpec((1, C), lambda i: (0, 0))
    return pl.pallas_call(
        _rwkv_post_kernel,
        out_shape=jax.ShapeDtypeStruct((N, C), BF16),
        grid=(N // tm,),
        in_specs=[row, row, row, row, vec, vec, pl.BlockSpec((C, C), lambda i: (0, 0))],
        out_specs=row,
        compiler_params=_cparams("arbitrary"),
        name="rwkv_post",
    )(yf, yr, bonus, g, gn_g.reshape(1, C), gn_b.reshape(1, C), bd)


def _fourier_ch_kernel(p_ref, m_ref, o_ref):
    p = p_ref[...]
    mat = m_ref[...]
    G = FOURIER_GROUP_DIM
    for g in range(p.shape[1] // G):
        o_ref[:, g * G:(g + 1) * G] = jnp.dot(p[:, g * G:(g + 1) * G], mat,
                                              preferred_element_type=F32).astype(o_ref.dtype)


def _fourier_channel(proj, col, Fd, ch_mats, B, T):
    tm = _pick(T, (1024, 512, 256))
    nt = T // tm
    G = FOURIER_GROUP_DIM
    return pl.pallas_call(
        _fourier_ch_kernel,
        out_shape=jax.ShapeDtypeStruct((B, 2, T, Fd), BF16),
        grid=(B, nt, 2),
        in_specs=[pl.BlockSpec((tm, Fd), lambda b, i, s: (b * nt + i, col)),
                  pl.BlockSpec((None, G, G), lambda b, i, s: (s, 0, 0))],
        out_specs=pl.BlockSpec((None, None, tm, Fd), lambda b, i, s: (b, s, i, 0)),
        compiler_params=_cparams("arbitrary", "arbitrary", "arbitrary"),
        name="fourier_channel",
    )(proj, ch_mats)


def _dft_mats(n):
    idx = jnp.arange(n, dtype=jnp.int32)
    ph = (idx[:, None] * idx[None, :]) % n
    ang = ph.astype(F32) * (2.0 * math.pi / n)
    s = 1.0 / math.sqrt(n)
    return jnp.cos(ang) * s, jnp.sin(ang) * s


DFT_COLS = 8


def _dft1_kernel(z_ref, m1_ref, tc_ref, ts_ref, o_ref, *, T1, Fd):
    a = jnp.dot(m1_ref[...], z_ref[...], preferred_element_type=F32)
    tc = tc_ref[...]
    ts = ts_ref[...]
    for j in range(DFT_COLS):
        ar = a[:T1, j * Fd:(j + 1) * Fd]
        ai = a[T1:, j * Fd:(j + 1) * Fd]
        c = tc[:, j:j + 1]
        s = ts[:, j:j + 1]
        o_ref[:T1, j * Fd:(j + 1) * Fd] = (ar * c + ai * s).astype(o_ref.dtype)
        o_ref[T1:, j * Fd:(j + 1) * Fd] = (ai * c - ar * s).astype(o_ref.dtype)


def _dft2_kernel(b_ref, m2_ref, o_ref, *, Fd):
    m2 = m2_ref[...]
    for j in range(DFT_COLS):
        rhs = jnp.concatenate([b_ref[0, j], b_ref[1, j]], axis=0)
        o_ref[:, j * Fd:(j + 1) * Fd] = jnp.dot(m2, rhs, preferred_element_type=F32).astype(o_ref.dtype)


def _fourier_time(z, B, T, Fd):
    T2 = GRID_W
    T1 = T // T2
    if T1 % DFT_COLS or T2 % DFT_COLS:
        cos, sin = _dft_mats(T)
        mat = jnp.concatenate([cos, sin], axis=1).astype(BF16)
        zz = z.reshape(B, 2 * T, Fd)
        return jnp.concatenate([_mm(mat, zz[b], BF16) for b in range(B)], axis=0)
    c1, s1 = _dft_mats(T1)
    m1 = jnp.concatenate([jnp.concatenate([c1, s1], axis=1),
                          jnp.concatenate([-s1, c1], axis=1)], axis=0).astype(BF16)
    c2, s2 = _dft_mats(T2)
    m2 = jnp.concatenate([c2, s2], axis=1).astype(BF16)
    ph = (jnp.arange(T1, dtype=jnp.int32)[:, None] * jnp.arange(T2, dtype=jnp.int32)[None, :]) % T
    ang = ph.astype(F32) * (2.0 * math.pi / T)
    nt2 = T2 // DFT_COLS
    twc = jnp.cos(ang).reshape(T1, nt2, DFT_COLS).transpose(1, 0, 2)
    tws = jnp.sin(ang).reshape(T1, nt2, DFT_COLS).transpose(1, 0, 2)
    tn = DFT_COLS * Fd
    stage1 = pl.pallas_call(
        functools.partial(_dft1_kernel, T1=T1, Fd=Fd),
        out_shape=jax.ShapeDtypeStruct((B, 2 * T1, T2 * Fd), BF16),
        grid=(B, nt2),
        in_specs=[pl.BlockSpec((None, 2 * T1, tn), lambda b, i: (b, 0, i)),
                  pl.BlockSpec((2 * T1, 2 * T1), lambda b, i: (0, 0)),
                  pl.BlockSpec((None, T1, DFT_COLS), lambda b, i: (i, 0, 0)),
                  pl.BlockSpec((None, T1, DFT_COLS), lambda b, i: (i, 0, 0))],
        out_specs=pl.BlockSpec((None, 2 * T1, tn), lambda b, i: (b, 0, i)),
        compiler_params=_cparams("arbitrary", "arbitrary"),
        name="dft_stage1",
    )(z.reshape(B, 2 * T1, T2 * Fd), m1, twc, tws)
    out = pl.pallas_call(
        functools.partial(_dft2_kernel, Fd=Fd),
        out_shape=jax.ShapeDtypeStruct((B, T2, T1 * Fd), BF16),
        grid=(B, T1 // DFT_COLS),
        in_specs=[pl.BlockSpec((None, 2, DFT_COLS, T2, Fd), lambda b, i: (b, 0, i, 0, 0)),
                  pl.BlockSpec((T2, 2 * T2), lambda b, i: (0, 0))],
        out_specs=pl.BlockSpec((None, T2, tn), lambda b, i: (b, 0, i)),
        compiler_params=_cparams("arbitrary", "arbitrary"),
        name="dft_stage2",
    )(stage1.reshape(B, 2, T1, T2, Fd), m2)
    return out.reshape(B * T, Fd)


def _merge_kernel(fa_ref, at_ref, rw_ref, gt_ref, x_ref, g1_ref, wa_ref, wb_ref, wc_ref, wo_ref, o_ref):
    D = x_ref.shape[1]
    ya = jnp.dot(fa_ref[...], wa_ref[...], preferred_element_type=F32)
    yb = jnp.dot(at_ref[0], wb_ref[0], preferred_element_type=F32)
    for h in range(1, at_ref.shape[0]):
        yb = yb + jnp.dot(at_ref[h], wb_ref[h], preferred_element_type=F32)
    yc = jnp.dot(rw_ref[...], wc_ref[...], preferred_element_type=F32)
    gt = gt_ref[...].astype(F32)
    merged = (_sigmoid(gt[:, :D]) * ya + _sigmoid(gt[:, D:2 * D]) * yb + _sigmoid(gt[:, 2 * D:]) * yc)
    out = jnp.dot(merged.astype(BF16), wo_ref[...], preferred_element_type=F32)
    o_ref[...] = x_ref[...] + g1_ref[...] * out


def _merge(fa, attn, rw, proj, gcol, x, g1, wa, wb, wc, wo, B, T):
    N, D = x.shape
    H = attn.shape[1]
    Fd = rw.shape[1]
    tm = _pick(T, (512, 256))
    nt = T // tm
    Bm = g1.shape[0]
    full = lambda shape: pl.BlockSpec(shape, lambda b, i: (0,) * len(shape))
    return pl.pallas_call(
        _merge_kernel,
        out_shape=jax.ShapeDtypeStruct((N, D), F32),
        grid=(B, nt),
        in_specs=[pl.BlockSpec((tm, Fd), lambda b, i: (b * nt + i, 0)),
                  pl.BlockSpec((None, H, tm, LANES), lambda b, i: (b, 0, i, 0)),
                  pl.BlockSpec((tm, Fd), lambda b, i: (b * nt + i, 0)),
                  pl.BlockSpec((tm, 3 * D), lambda b, i: (b * nt + i, gcol)),
                  pl.BlockSpec((tm, D), lambda b, i: (b * nt + i, 0)),
                  pl.BlockSpec((None, 1, D), lambda b, i: (b % Bm, 0, 0)),
                  full(wa.shape), full(wb.shape), full(wc.shape), full(wo.shape)],
        out_specs=pl.BlockSpec((tm, D), lambda b, i: (b * nt + i, 0)),
        compiler_params=_cparams("arbitrary", "arbitrary"),
        name="merge_out",
    )(fa, attn, rw, proj, x, g1, wa, wb, wc, wo)


def _route(scores, sel, n_exp):
    tm = scores.shape[0]
    lane = lax.broadcasted_iota(jnp.int32, (tm, LANES), 1)
    neg = -jnp.inf
    sel = jnp.where(lane < n_exp, sel, neg)
    epg = n_exp // N_GROUPS
    assert epg == 4
    odd = (lane % 2) == 1
    p1 = jnp.where(odd, pltpu.roll(sel, 1, axis=1), pltpu.roll(sel, LANES - 1, axis=1))
    hi1, lo1 = jnp.maximum(sel, p1), jnp.minimum(sel, p1)
    upper = (lane % 4) >= 2
    hi2 = jnp.where(upper, pltpu.roll(hi1, 2, axis=1), pltpu.roll(hi1, LANES - 2, axis=1))
    lo2 = jnp.where(upper, pltpu.roll(lo1, 2, axis=1), pltpu.roll(lo1, LANES - 2, axis=1))
    gscore = jnp.maximum(hi1, hi2) + jnp.maximum(jnp.minimum(hi1, hi2), jnp.maximum(lo1, lo2))
    gscore = jnp.where(lane < n_exp, gscore, neg)
    gmax = jnp.max(gscore, axis=1, keepdims=True)
    lane_f = lane.astype(F32)
    grp = (lane // epg).astype(F32)
    big = float(LANES)
    best = jnp.min(jnp.where(gscore == gmax, grp, big), axis=1, keepdims=True)
    masked = jnp.where(grp == best, sel, neg)
    m1 = jnp.max(masked, axis=1, keepdims=True)
    i1 = jnp.min(jnp.where(masked == m1, lane_f, big), axis=1, keepdims=True)
    masked2 = jnp.where(lane_f == i1, neg, masked)
    m2 = jnp.max(masked2, axis=1, keepdims=True)
    i2 = jnp.min(jnp.where(masked2 == m2, lane_f, big), axis=1, keepdims=True)
    pick1 = lane_f == i1
    pick2 = lane_f == i2
    s1 = jnp.sum(jnp.where(pick1, scores, 0.0), axis=1, keepdims=True)
    s2 = jnp.sum(jnp.where(pick2, scores, 0.0), axis=1, keepdims=True)
    return pick1, pick2, s1 / (s1 + s2), s2 / (s1 + s2)


def _lane_to_row(col_vals):
    ones = jnp.ones((8, LANES), BF16)
    hi, lo = _split2(col_vals)
    return (_dot_nt(ones, hi) + _dot_nt(ones, lo))[0:1]


def _moe_kernel(x_ref, g_ref, sc_ref, sh_ref, g2_ref, rw_ref, rb_ref,
                wg_ref, wu_ref, wd_ref, swg_ref, swu_ref, swd_ref, o_ref,
                m_ref, xs_ref, ys_ref, pm_ref, ws_ref, oc_ref, *, n_exp):
    e = pl.program_id(1)
    tm = x_ref.shape[0]
    S = 2 * tm
    RB = MOE_ROWS

    @pl.when(e == 0)
    def _():
        x = x_ref[...]
        m = x * lax.rsqrt(jnp.mean(x * x, axis=-1, keepdims=True) + NORM_EPS) * g_ref[...]
        m = m * (1.0 + sc_ref[...]) + sh_ref[...]
        mb = m.astype(BF16)
        m_ref[...] = mb
        scores = _sigmoid(_dot_split(m, rw_ref[...]))
        pick1, pick2, w1, w2 = _route(scores, scores + rb_ref[...], n_exp)
        p1 = pick1.astype(F32)
        p2 = pick2.astype(F32)
        chosen = p1 + p2
        cnt = jnp.sum(chosen, axis=0, keepdims=True)
        li = lax.broadcasted_iota(jnp.int32, (LANES, LANES), 0)
        lj = lax.broadcasted_iota(jnp.int32, (LANES, LANES), 1)
        off = _dot_exact_rhs(jnp.broadcast_to(cnt, (8, LANES)), (li < lj).astype(F32).astype(BF16))[0:1]
        ti = lax.broadcasted_iota(jnp.int32, (tm, tm), 0)
        tj = lax.broadcasted_iota(jnp.int32, (tm, tm), 1)
        rank = jnp.dot((tj < ti).astype(F32).astype(BF16), chosen.astype(BF16),
                       preferred_element_type=F32)
        dest = off + rank
        d1 = _lane_to_row(p1 * dest)
        d2 = _lane_to_row(p2 * dest)
        w1r = _lane_to_row(jnp.broadcast_to(w1, (tm, LANES)) * p1)
        w2r = _lane_to_row(jnp.broadcast_to(w2, (tm, LANES)) * p2)
        row = lax.broadcasted_iota(jnp.int32, (S, tm), 0).astype(F32)
        is1 = row == d1
        is2 = row == d2
        perm = jnp.where(is1, 1.0, jnp.where(is2, 1.0, 0.0)).astype(BF16)
        pm_ref[...] = perm
        ws_ref[...] = jnp.sum(jnp.where(is1, w1r, jnp.where(is2, w2r, 0.0)), axis=1, keepdims=True)
        xs_ref[...] = jnp.dot(perm, mb, preferred_element_type=F32).astype(BF16)
        ys_ref[...] = jnp.zeros_like(ys_ref)
        oc_ref[0:1, :] = off
        oc_ref[1:2, :] = cnt

    @pl.when(e < n_exp)
    def _():
        lane = lax.broadcasted_iota(jnp.int32, (1, LANES), 1)
        o_e = jnp.sum(jnp.where(lane == e, oc_ref[0:1, :], 0.0)).astype(jnp.int32)
        c_e = jnp.sum(jnp.where(lane == e, oc_ref[1:2, :], 0.0)).astype(jnp.int32)
        wg = wg_ref[...].astype(BF16)
        wu = wu_ref[...].astype(BF16)
        wd = wd_ref[...].astype(BF16)
        b0 = o_e // RB
        b1 = jnp.where(c_e > 0, (o_e + c_e + RB - 1) // RB, b0)

        def body(b, carry):
            r0 = pl.multiple_of(b * RB, RB)
            xb = xs_ref[pl.ds(r0, RB), :]
            hg = jnp.dot(xb, wg, preferred_element_type=F32)
            hu = jnp.dot(xb, wu, preferred_element_type=F32)
            rows = r0 + lax.broadcasted_iota(jnp.int32, (RB, 1), 0)
            mine = (rows >= o_e) & (rows < o_e + c_e)
            wrow = jnp.where(mine, ws_ref[pl.ds(r0, RB), :], 0.0)
            hh = hg * _sigmoid(hg) * hu * wrow
            ys_ref[pl.ds(r0, RB), :] += jnp.dot(hh.astype(BF16), wd, preferred_element_type=F32)
            return carry

        lax.fori_loop(b0, b1, body, 0)

    @pl.when(e == n_exp)
    def _():
        m = m_ref[...]
        hg = jnp.dot(m, swg_ref[...], preferred_element_type=F32)
        hu = jnp.dot(m, swu_ref[...], preferred_element_type=F32)
        hh = hg * _sigmoid(hg) * hu
        y = jnp.dot(hh.astype(BF16), swd_ref[...], preferred_element_type=F32)
        y = y + _dot_tn(pm_ref[...], ys_ref[...])
        o_ref[...] = x_ref[...] + g2_ref[...] * y


def _moe(x, g, sc, sh, g2, rw_pad, rb_pad, wg, wu, wd, swg, swu, swd):
    N, D = x.shape
    E, _, Ff = wg.shape
    Bm = sc.shape[0]
    rpm = N // Bm
    tm = min(rpm, 1024)
    assert rpm % tm == 0 and (2 * tm) % MOE_ROWS == 0
    one = pl.Buffered(1)
    mod = pl.BlockSpec((None, 1, D), lambda i, e: (i * tm // rpm, 0, 0))
    full = lambda shape: pl.BlockSpec(shape, lambda i, e: (0,) * len(shape), pipeline_mode=one)
    return pl.pallas_call(
        functools.partial(_moe_kernel, n_exp=E),
        out_shape=jax.ShapeDtypeStruct((N, D), F32),
        grid=(N // tm, E + 1),
        in_specs=[pl.BlockSpec((tm, D), lambda i, e: (i, 0), pipeline_mode=one),
                  full((1, D)), mod, mod, mod,
                  full(rw_pad.shape), full(rb_pad.shape),
                  pl.BlockSpec((None, D, Ff), lambda i, e: (jnp.minimum(e, E - 1), 0, 0)),
                  pl.BlockSpec((None, D, Ff), lambda i, e: (jnp.minimum(e, E - 1), 0, 0)),
                  pl.BlockSpec((None, Ff, D), lambda i, e: (jnp.minimum(e, E - 1), 0, 0)),
                  full(swg.shape), full(swu.shape), full(swd.shape)],
        out_specs=pl.BlockSpec((tm, D), lambda i, e: (i, 0), pipeline_mode=one),
        scratch_shapes=[pltpu.VMEM((tm, D), BF16), pltpu.VMEM((2 * tm, D), BF16),
                        pltpu.VMEM((2 * tm, D), F32), pltpu.VMEM((2 * tm, tm), BF16),
                        pltpu.VMEM((2 * tm, 1), F32), pltpu.VMEM((8, LANES), F32)],
        compiler_params=pltpu.CompilerParams(dimension_semantics=("arbitrary", "arbitrary"),
                                             vmem_limit_bytes=MOE_VMEM_LIMIT_BYTES),
        name="moe",
    )(x, g.reshape(1, D), sc, sh, g2, rw_pad, rb_pad, wg, wu, wd, swg, swu, swd)


def _rope_partner(w):
    lead = w.shape[:-1]
    wr = w.reshape(lead + (2, 2, ROPE_PAIRS))
    return jnp.stack([-wr[..., 1, :], wr[..., 0, :]], axis=-2).reshape(w.shape)


def _rope_tables(T):
    rows = T // GRID_W
    row = jnp.repeat(jnp.arange(rows), GRID_W)
    col = jnp.tile(jnp.arange(GRID_W), rows)
    inv = ROPE_BASE ** (-jnp.arange(ROPE_PAIRS, dtype=F32) / ROPE_PAIRS)
    ang = jnp.stack([row, col], axis=-1).astype(F32)[..., None] * inv
    cos = jnp.broadcast_to(jnp.cos(ang)[:, :, None, :], (T, 2, 2, ROPE_PAIRS)).reshape(T, MLA_ROPE)
    sin = jnp.broadcast_to(jnp.sin(ang)[:, :, None, :], (T, 2, 2, ROPE_PAIRS)).reshape(T, MLA_ROPE)
    pad = LANES - MLA_QK
    z_n, z_p = jnp.zeros((T, MLA_NOPE), F32), jnp.zeros((T, pad), F32)
    cq = jnp.concatenate([jnp.ones((T, MLA_NOPE), F32), cos, z_p], axis=1)
    sq = jnp.concatenate([z_n, sin, z_p], axis=1)
    ck = jnp.concatenate([z_n, cos, z_p], axis=1)
    return cq, sq, ck, sq


def _identity_tables(T):
    pad = LANES - MLA_QK
    z_n, z_p = jnp.zeros((T, MLA_NOPE), F32), jnp.zeros((T, pad), F32)
    one_r = jnp.ones((T, MLA_ROPE), F32)
    cq = jnp.concatenate([jnp.ones((T, MLA_NOPE), F32), one_r, z_p], axis=1)
    ck = jnp.concatenate([z_n, one_r, z_p], axis=1)
    zero = jnp.zeros((T, LANES), F32)
    return cq, zero, ck, zero


def kernel(x, c, ctx, c_ctx, w_ada, b_ada, norm1_g, norm2_g, w_in, q_norm_g, kv_norm_g, w_uq, w_ukv, rwkv_mu, rwkv_w0, rwkv_w_up, rwkv_a0, rwkv_a_up, rwkv_g_up, rwkv_k_k, rwkv_k_a, rwkv_r_k, rwkv_gn_g, rwkv_gn_b, w_proj_a, w_proj_b, w_proj_c, w_out, router_w, router_b, exp_w_gate, exp_w_up, exp_w_down, sh_w_gate, sh_w_up, sh_w_down, final_norm_g):
    B, T, D = x.shape
    Tc = ctx.shape[1]
    L = w_ada.shape[0]
    H = MLA_HEADS
    QL, KVL = q_norm_g.shape[1], kv_norm_g.shape[1]
    Fd = w_proj_a.shape[1]
    C = RWKV_DIM
    E = router_w.shape[1]
    N, Nc = B * T, B * Tc

    sizes = (Fd, QL, KVL, MLA_ROPE, 3 * C + DECAY_LORA + AAA_LORA + GATE_LORA, D, D, D)
    offs = np.concatenate([[0], np.cumsum(sizes)])

    lat_tabs = _rope_tables(T)
    ctx_tabs = _identity_tables(Tc)
    cc, cs = _dft_mats(FOURIER_GROUP_DIM)
    ch_mats = jnp.stack([cc, -cs]).astype(BF16)
    hid = jnp.arange(C) // RWKV_HEAD
    bd = (hid[:, None] == hid[None, :]).astype(BF16)
    rw_pad = jnp.zeros((D, LANES), F32).at[:, :E].set(router_w)
    rb_pad = jnp.zeros((1, LANES), F32).at[0, :E].set(router_b)

    R = -(-(B + 1) // 8) * 8
    c_rows = jnp.zeros((R, D), F32).at[:B].set(c).at[B].set(c_ctx)
    mod = _ada_mod(c_rows, w_ada, b_ada)

    x = x.reshape(N, D)
    ctx = ctx.reshape(Nc, D)
    zero_state = jnp.zeros((2, B, RWKV_HEADS, RWKV_HEAD, RWKV_HEAD), F32)

    for l in range(L):
        last = l == L - 1
        ml = mod[l, :B].reshape(B, 1, 6, D)
        mc = mod[l, B:B + 1].reshape(1, 1, 6, D)
        sh1, sc1, g1, sh2, sc2, g2 = [ml[:, :, j] for j in range(6)]
        csh1, csc1, cg1, csh2, csc2, cg2 = [mc[:, :, j] for j in range(6)]

        wi = w_in[l]
        w_kr = wi[:, offs[3]:offs[4]]
        zl = jnp.zeros((D, MLA_NOPE), F32)
        zr = jnp.zeros((D, LANES - MLA_QK), F32)
        segs = [wi[:, offs[5]:offs[8]],
                wi[:, offs[0]:offs[1]],
                wi[:, offs[4]:offs[5]],
                jnp.concatenate([wi[:, offs[1]:offs[3]], zl, w_kr, zr, zl, _rope_partner(w_kr), zr], axis=1)]
        seg_off, n_all = _column_layout([s.shape[1] for s in segs])
        w_all = jnp.zeros((D, n_all), BF16)
        for s, o in zip(segs, seg_off):
            w_all = w_all.at[:, o:o + s.shape[1]].set(s.astype(BF16))
        gt_col, pf_col, rw_col, qkv_col = [o // s.shape[1] for s, o in zip(segs, seg_off)]
        uq = w_uq[l].reshape(QL, H, MLA_QK).transpose(1, 0, 2)
        padq = jnp.zeros((H, QL, LANES - MLA_QK), F32)
        wqa = jnp.concatenate([uq, padq], axis=2).astype(BF16)
        wqb = jnp.concatenate([jnp.zeros((H, QL, MLA_NOPE), F32), _rope_partner(uq[..., MLA_NOPE:]), padq],
                              axis=2).astype(BF16)
        ukv = w_ukv[l].reshape(KVL, H, MLA_NOPE + MLA_V).transpose(1, 0, 2)
        wk = jnp.concatenate([ukv[..., :MLA_NOPE], jnp.zeros((H, KVL, LANES - MLA_NOPE), F32)], axis=2).astype(BF16)
        wv = jnp.concatenate([ukv[..., MLA_NOPE:], jnp.zeros((H, KVL, LANES - MLA_V), F32)], axis=2).astype(BF16)
        wb = jnp.concatenate([w_proj_b[l].reshape(H, MLA_V, D), jnp.zeros((H, LANES - MLA_V, D), F32)],
                             axis=1).astype(BF16)
        wa = w_proj_a[l].astype(BF16)
        wc = w_proj_c[l].astype(BF16)
        wo = w_out[l].astype(BF16)
        zd = jnp.zeros((2, DECAY_LORA, C), F32)
        prm = dict(
            mu=rwkv_mu[l].reshape(1, -1), w0=rwkv_w0[l], a0=rwkv_a0[l],
            wup=jnp.concatenate([rwkv_w_up[l], zd], axis=1).astype(BF16),
            aup=jnp.concatenate([zd, rwkv_a_up[l]], axis=1).astype(BF16),
            gup=rwkv_g_up[l].astype(BF16), k_k=rwkv_k_k[l].reshape(1, C), k_a=rwkv_k_a[l].reshape(1, C),
            r_k=rwkv_r_k[l].reshape(1, C), bd=bd)

        def mixers(xx, sc, sh, tabs, Tx):
            proj = _in_proj(xx, norm1_g[l], sc, sh, w_all)
            q, k, v = _qkv_project(proj, qkv_col, q_norm_g[l], kv_norm_g[l], wqa, wqb, wk, wv, tabs, B, Tx)
            rp = _rwkv_prep(proj, rw_col, prm, B, Tx)
            return proj, q, k, v, rp

        proj_l, q_l, k_l, v_l, rp_l = mixers(x, sc1, sh1, lat_tabs, T)
        proj_c, q_c, k_c, v_c, rp_c = mixers(ctx, csc1, csh1, ctx_tabs, Tc)

        attn_l = _attention(q_l, [(k_l, v_l), (k_c, v_c)])

        yf_c, yr_c, s_ctx = _rwkv_scan(*rp_c[:6], prm["k_a"], zero_state, B, Tc)
        yf_l, yr_l, _ = _rwkv_scan(*rp_l[:6], prm["k_a"], s_ctx, B, T)
        rwo_l = _rwkv_post(yf_l, yr_l, rp_l[7], rp_l[6], rwkv_gn_g[l], rwkv_gn_b[l], bd)

        fa_l = _fourier_time(_fourier_channel(proj_l, pf_col, Fd, ch_mats, B, T), B, T, Fd)

        x = _merge(fa_l, attn_l, rwo_l, proj_l, gt_col, x, g1, wa, wb, wc, wo, B, T)
        moe_w = (rw_pad, rb_pad, exp_w_gate[l], exp_w_up[l], exp_w_down[l],
                 sh_w_gate[l].astype(BF16), sh_w_up[l].astype(BF16), sh_w_down[l].astype(BF16))
        x = _moe(x, norm2_g[l], sc2, sh2, g2, *moe_w)

        if not last:
            attn_c = _attention(q_c, [(k_c, v_c)])
            rwo_c = _rwkv_post(yf_c, yr_c, rp_c[7], rp_c[6], rwkv_gn_g[l], rwkv_gn_b[l], bd)
            fa_c = _fourier_time(_fourier_channel(proj_c, pf_col, Fd, ch_mats, B, Tc), B, Tc, Fd)
            ctx = _merge(fa_c, attn_c, rwo_c, proj_c, gt_col, ctx, cg1, wa, wb, wc, wo, B, Tc)
            ctx = _moe(ctx, norm2_g[l], csc2, csh2, cg2, *moe_w)

    ones = jnp.ones((1, 1, D), F32)
    out = _norm(x, final_norm_g, ones, ones, F32, modulate=False)
    return out.reshape(B, T, D)
```

```python
import functools
import math

import numpy as np
import jax
import jax.numpy as jnp
from jax import lax
from jax.experimental import pallas as pl
from jax.experimental.pallas import tpu as pltpu

F32 = jnp.float32
BF16 = jnp.bfloat16

GRID_W = 64
FOURIER_GROUP_DIM = 128
MLA_HEADS = 8
MLA_NOPE = 64
MLA_ROPE = 32
MLA_QK = MLA_NOPE + MLA_ROPE
MLA_V = 64
MLA_SCALE = MLA_QK ** -0.5
QK_PRESCALE = MLA_SCALE * math.log2(math.e)
ROPE_PAIRS = MLA_ROPE // 4
ROPE_BASE = 10000.0
RWKV_HEADS = 8
RWKV_HEAD = 64
RWKV_DIM = RWKV_HEADS * RWKV_HEAD
DECAY_LORA = 64
AAA_LORA = 64
GATE_LORA = 128
GN_EPS = 64e-5
N_GROUPS = 4
NORM_EPS = 1e-6

LANES = 128
VMEM_LIMIT_BYTES = 48 * 1024 * 1024
RWKV_CHUNK = 64
MOE_ROWS = 128
MOE_VMEM_LIMIT_BYTES = 56 * 1024 * 1024


def _cparams(*sem):
    return pltpu.CompilerParams(dimension_semantics=sem, vmem_limit_bytes=VMEM_LIMIT_BYTES)


def _sigmoid(z):
    return 1.0 / (1.0 + jnp.exp(-z))


def _bdot(a, b):
    return jnp.dot(a.astype(BF16), b.astype(BF16), preferred_element_type=F32)


def _dot_nt(a, b):
    return lax.dot_general(a.astype(BF16), b.astype(BF16), (((1,), (1,)), ((), ())),
                           preferred_element_type=F32)


def _dot_tn(a, b):
    return lax.dot_general(a.astype(BF16), b.astype(BF16), (((0,), (0,)), ((), ())),
                           preferred_element_type=F32)


def _split2(x):
    hi = x.astype(BF16)
    lo = (x - hi.astype(F32)).astype(BF16)
    return hi, lo


def _dot_exact_rhs(x, ones_bf16):
    hi, lo = _split2(x)
    return (jnp.dot(hi, ones_bf16, preferred_element_type=F32)
            + jnp.dot(lo, ones_bf16, preferred_element_type=F32))


def _dot_split(a, b):
    ah, al = _split2(a)
    bh, bl = _split2(b)
    return (jnp.dot(ah, bh, preferred_element_type=F32)
            + jnp.dot(ah, bl, preferred_element_type=F32)
            + jnp.dot(al, bh, preferred_element_type=F32))


def _ada_kernel(c_ref, w_ref, b_ref, o_ref):
    c = c_ref[...]
    s = c * _sigmoid(c)
    o_ref[...] = _dot_split(s, w_ref[...]) + b_ref[...]


def _ada_mod(c_rows, w_ada, b_ada):
    L, D, D6 = w_ada.shape
    R = c_rows.shape[0]
    tn = 1536 if D6 % 1536 == 0 else D6
    return pl.pallas_call(
        _ada_kernel,
        out_shape=jax.ShapeDtypeStruct((L, R, D6), F32),
        grid=(L, D6 // tn),
        in_specs=[pl.BlockSpec((R, D), lambda l, j: (0, 0)),
                  pl.BlockSpec((None, D, tn), lambda l, j: (l, 0, j)),
                  pl.BlockSpec((None, 1, tn), lambda l, j: (l, 0, j))],
        out_specs=pl.BlockSpec((None, R, tn), lambda l, j: (l, 0, j)),
        compiler_params=_cparams("arbitrary", "arbitrary"),
        name="ada_mod",
    )(c_rows, w_ada, b_ada.reshape(L, 1, D6))


def _norm_kernel(x_ref, g_ref, sc_ref, sh_ref, o_ref, *, modulate):
    x = x_ref[...].astype(F32)
    y = x * lax.rsqrt(jnp.mean(x * x, axis=-1, keepdims=True) + NORM_EPS) * g_ref[...]
    if modulate:
        y = y * (1.0 + sc_ref[...]) + sh_ref[...]
    o_ref[...] = y.astype(o_ref.dtype)


def _norm(x, g, sc, sh, out_dtype, modulate=True):
    N, D = x.shape
    Bm = sc.shape[0]
    rpm = N // Bm
    tm = min(rpm, 1024)
    assert rpm % tm == 0
    return pl.pallas_call(
        functools.partial(_norm_kernel, modulate=modulate),
        out_shape=jax.ShapeDtypeStruct((N, D), out_dtype),
        grid=(N // tm,),
        in_specs=[pl.BlockSpec((tm, D), lambda i: (i, 0)),
                  pl.BlockSpec((1, D), lambda i: (0, 0)),
                  pl.BlockSpec((None, 1, D), lambda i: (i * tm // rpm, 0, 0)),
                  pl.BlockSpec((None, 1, D), lambda i: (i * tm // rpm, 0, 0))],
        out_specs=pl.BlockSpec((tm, D), lambda i: (i, 0)),
        compiler_params=_cparams("arbitrary"),
        name="rmsnorm_mod",
    )(x, g.reshape(1, D), sc, sh)


def _in_proj_kernel(x_ref, g_ref, sc_ref, sh_ref, w_ref, o_ref, h_ref):
    @pl.when(pl.program_id(1) == 0)
    def _():
        x = x_ref[...]
        y = x * lax.rsqrt(jnp.mean(x * x, axis=-1, keepdims=True) + NORM_EPS) * g_ref[...]
        h_ref[...] = (y * (1.0 + sc_ref[...]) + sh_ref[...]).astype(BF16)

    o_ref[...] = jnp.dot(h_ref[...], w_ref[...], preferred_element_type=F32).astype(o_ref.dtype)


def _in_proj(x, g, sc, sh, w_all):
    N, D = x.shape
    Nout = w_all.shape[1]
    Bm = sc.shape[0]
    rpm = N // Bm
    tm = min(rpm, 1024)
    assert rpm % tm == 0
    tn = _pick(Nout, (896, 512, 256, 128))
    mod = pl.BlockSpec((None, 1, D), lambda i, j: (i * tm // rpm, 0, 0))
    return pl.pallas_call(
        _in_proj_kernel,
        out_shape=jax.ShapeDtypeStruct((N, Nout), BF16),
        grid=(N // tm, Nout // tn),
        in_specs=[pl.BlockSpec((tm, D), lambda i, j: (i, 0)),
                  pl.BlockSpec((1, D), lambda i, j: (0, 0)), mod, mod,
                  pl.BlockSpec((D, tn), lambda i, j: (0, j))],
        out_specs=pl.BlockSpec((tm, tn), lambda i, j: (i, j)),
        scratch_shapes=[pltpu.VMEM((tm, D), BF16)],
        compiler_params=_cparams("arbitrary", "arbitrary"),
        name="in_proj",
    )(x, g.reshape(1, D), sc, sh, w_all)


def _column_layout(widths):
    offs, cur = [], 0
    for w in widths:
        cur = -(-cur // w) * w
        offs.append(cur)
        cur += w
    return offs, -(-cur // LANES) * LANES


def _mm_kernel(a_ref, b_ref, o_ref):
    o_ref[...] = jnp.dot(a_ref[...], b_ref[...], preferred_element_type=F32).astype(o_ref.dtype)


def _mm_acc_kernel(a_ref, b_ref, o_ref, acc_ref, *, nk):
    k = pl.program_id(2)

    @pl.when(k == 0)
    def _():
        acc_ref[...] = jnp.zeros_like(acc_ref)

    acc_ref[...] += jnp.dot(a_ref[...], b_ref[...], preferred_element_type=F32)

    @pl.when(k == nk - 1)
    def _():
        o_ref[...] = acc_ref[...].astype(o_ref.dtype)


def _pick(n, prefs):
    for p in prefs:
        if n % p == 0:
            return p
    return n


def _mm(a, w, out_dtype, tn=None, tk=None):
    M, K = a.shape
    Nn = w.shape[1]
    tm = _pick(M, (1024, 512, 256))
    tn = tn or _pick(Nn, (1024, 896, 512))
    tk = tk or K
    if tk == K:
        return pl.pallas_call(
            _mm_kernel,
            out_shape=jax.ShapeDtypeStruct((M, Nn), out_dtype),
            grid=(M // tm, Nn // tn),
            in_specs=[pl.BlockSpec((tm, K), lambda i, j: (i, 0)),
                      pl.BlockSpec((K, tn), lambda i, j: (0, j))],
            out_specs=pl.BlockSpec((tm, tn), lambda i, j: (i, j)),
            compiler_params=_cparams("arbitrary", "arbitrary"),
            name="matmul",
        )(a, w)
    nk = K // tk
    return pl.pallas_call(
        functools.partial(_mm_acc_kernel, nk=nk),
        out_shape=jax.ShapeDtypeStruct((M, Nn), out_dtype),
        grid=(M // tm, Nn // tn, nk),
        in_specs=[pl.BlockSpec((tm, tk), lambda i, j, k: (i, k)),
                  pl.BlockSpec((tk, tn), lambda i, j, k: (k, j))],
        out_specs=pl.BlockSpec((tm, tn), lambda i, j, k: (i, j)),
        scratch_shapes=[pltpu.VMEM((tm, tn), F32)],
        compiler_params=_cparams("arbitrary", "arbitrary", "arbitrary"),
        name="matmul_kacc",
    )(a, w)


def _qkv_kernel(p_ref, gq_ref, gkv_ref, wqa_ref, wqb_ref, wk_ref, wv_ref,
                cq_ref, sq_ref, ck_ref, sk_ref, q_ref, k_ref, v_ref, *, ql, kvl):
    p = p_ref[...].astype(F32)
    pq = p[:, :ql]
    pkv = p[:, ql:ql + kvl]
    kra = p[:, ql + kvl:ql + kvl + LANES]
    krb = p[:, ql + kvl + LANES:ql + kvl + 2 * LANES]
    qn = (pq * lax.rsqrt(jnp.mean(pq * pq, axis=-1, keepdims=True) + NORM_EPS) * gq_ref[...]).astype(BF16)
    kvn = (pkv * lax.rsqrt(jnp.mean(pkv * pkv, axis=-1, keepdims=True) + NORM_EPS) * gkv_ref[...]).astype(BF16)
    kr = kra * ck_ref[...] + krb * sk_ref[...]
    cq = cq_ref[...] * QK_PRESCALE
    sq = sq_ref[...] * QK_PRESCALE
    for h in range(q_ref.shape[0]):
        qa = jnp.dot(qn, wqa_ref[h], preferred_element_type=F32)
        qb = jnp.dot(qn, wqb_ref[h], preferred_element_type=F32)
        q_ref[h] = (qa * cq + qb * sq).astype(q_ref.dtype)
        k_ref[h] = (jnp.dot(kvn, wk_ref[h], preferred_element_type=F32) + kr).astype(k_ref.dtype)
        v_ref[h] = jnp.dot(kvn, wv_ref[h], preferred_element_type=F32).astype(v_ref.dtype)


def _qkv_project(proj, col, gq, gkv, wqa, wqb, wk, wv, tabs, B, T):
    N = proj.shape[0]
    H = wqa.shape[0]
    ql, kvl = wqa.shape[1], wk.shape[1]
    W = ql + kvl + 2 * LANES
    tm = _pick(T, (512, 256))
    nt = T // tm
    cq, sq, ck, sk = tabs
    full = lambda shape: pl.BlockSpec(shape, lambda i: (0,) * len(shape))
    tab_spec = pl.BlockSpec((tm, LANES), lambda i: (i % nt, 0))
    out_spec = pl.BlockSpec((None, H, tm, LANES), lambda i: (i // nt, 0, i % nt, 0))
    out_sds = jax.ShapeDtypeStruct((B, H, T, LANES), BF16)
    return pl.pallas_call(
        functools.partial(_qkv_kernel, ql=ql, kvl=kvl),
        out_shape=(out_sds, out_sds, out_sds),
        grid=(N // tm,),
        in_specs=[pl.BlockSpec((tm, W), lambda i: (i, col)),
                  full((1, ql)), full((1, kvl)),
                  full(wqa.shape), full(wqb.shape), full(wk.shape), full(wv.shape),
                  tab_spec, tab_spec, tab_spec, tab_spec],
        out_specs=(out_spec, out_spec, out_spec),
        compiler_params=_cparams("arbitrary"),
        name="mla_qkv",
    )(proj, gq.reshape(1, ql), gkv.reshape(1, kvl), wqa, wqb, wk, wv, cq, sq, ck, sk)


def _attn_kernel(q_ref, *refs, tiles):
    o_ref = refs[-1]
    q = q_ref[...]
    m = l = acc = None
    for src, start, size in tiles:
        k = refs[2 * src][start:start + size, :]
        v = refs[2 * src + 1][start:start + size, :]
        s = _dot_nt(q, k)
        mt = jnp.max(s, axis=-1, keepdims=True)
        if m is None:
            m_new = mt
            p = jnp.exp2(s - m_new)
            l = jnp.sum(p, axis=-1, keepdims=True)
            acc = jnp.dot(p.astype(BF16), v, preferred_element_type=F32)
        else:
            m_new = jnp.maximum(m, mt)
            alpha = jnp.exp2(m - m_new)
            p = jnp.exp2(s - m_new)
            l = alpha * l + jnp.sum(p, axis=-1, keepdims=True)
            acc = alpha * acc + jnp.dot(p.astype(BF16), v, preferred_element_type=F32)
        m = m_new
    o_ref[...] = (acc / l).astype(o_ref.dtype)


def _attention(q, kvs):
    B, H, T, dq = q.shape
    tq = _pick(T, (1024, 512, 256))
    tiles = []
    for src, (k, _) in enumerate(kvs):
        Tk = k.shape[2]
        tk = _pick(Tk, (1024, 512, 256))
        tiles += [(src, st, tk) for st in range(0, Tk, tk)]
    in_specs = [pl.BlockSpec((None, None, tq, dq), lambda b, h, i: (b, h, i, 0))]
    args = [q]
    for k, v in kvs:
        Tk = k.shape[2]
        in_specs += [pl.BlockSpec((None, None, Tk, dq), lambda b, h, i: (b, h, 0, 0)),
                     pl.BlockSpec((None, None, Tk, LANES), lambda b, h, i: (b, h, 0, 0))]
        args += [k, v]
    return pl.pallas_call(
        functools.partial(_attn_kernel, tiles=tuple(tiles)),
        out_shape=jax.ShapeDtypeStruct((B, H, T, LANES), BF16),
        grid=(B, H, T // tq),
        in_specs=in_specs,
        out_specs=pl.BlockSpec((None, None, tq, LANES), lambda b, h, i: (b, h, i, 0)),
        compiler_params=_cparams("arbitrary", "arbitrary", "arbitrary"),
        name="attention",
    )(*args)


def _softplus(u):
    return jnp.maximum(u, 0.0) + jnp.log(1.0 + jnp.exp(-jnp.abs(u)))


def _rwkv_prep_kernel(p_ref, pp_ref, pn_ref, mu_ref, w0_ref, wup_ref, a0_ref, aup_ref, gup_ref,
                      kk_ref, ka_ref, rk_ref, bd_ref,
                      r_o, k_o, v_o, kn_o, lw_o, a_o, g_o, bonus_o, *, tps):
    i = pl.program_id(0)
    p = p_ref[...].astype(F32)
    tt = p.shape[0]
    C = RWKV_DIM
    halo = pp_ref.shape[0]
    prev_row = jnp.where(i % tps == 0, 0.0, pp_ref[...].astype(F32)[halo - 1:halo, :])
    next_row = jnp.where(i % tps == tps - 1, 0.0, pn_ref[...].astype(F32)[0:1, :])
    rows = lax.broadcasted_iota(jnp.int32, (tt, 1), 0)
    pprev = jnp.where(rows == 0, prev_row, pltpu.roll(p, 1, axis=0))
    pnext = jnp.where(rows == tt - 1, next_row, pltpu.roll(p, tt - 1, axis=0))
    xs = p + (0.5 * (pprev + pnext) - p) * mu_ref[...]
    r = xs[:, :C]
    k = xs[:, C:2 * C]
    v = xs[:, 2 * C:3 * C]
    xwa = xs[:, 3 * C:3 * C + DECAY_LORA + AAA_LORA]
    xg = xs[:, 3 * C + DECAY_LORA + AAA_LORA:]
    bd = bd_ref[...]
    g_o[...] = _bdot(_sigmoid(xg), gup_ref[...]).astype(g_o.dtype)
    kn = k * kk_ref[...]
    kn = kn * lax.rsqrt(_dot_exact_rhs(kn * kn, bd) + 1e-12)
    th = jnp.tanh(xwa)
    bonus = jnp.zeros_like(r)
    for d in range(2):
        w = -_softplus(-(w0_ref[d:d + 1, :] + _bdot(th, wup_ref[d]))) - 0.5
        lw_o[d] = -jnp.exp(w)
        a = _sigmoid(a0_ref[d:d + 1, :] + _bdot(xwa, aup_ref[d]))
        a_o[d] = a.astype(a_o.dtype)
        kd = k * (1.0 + (a - 1.0) * ka_ref[...])
        bonus = bonus + _dot_exact_rhs(r * kd * rk_ref[...], bd) * v
    r_o[...] = r.astype(r_o.dtype)
    k_o[...] = k.astype(k_o.dtype)
    v_o[...] = v.astype(v_o.dtype)
    kn_o[...] = kn.astype(kn_o.dtype)
    bonus_o[...] = bonus.astype(bonus_o.dtype)


def _rwkv_prep(proj, col, prm, B, T):
    N = proj.shape[0]
    W = prm["mu"].shape[1]
    C = RWKV_DIM
    tt = _pick(T, (256,))
    tps = T // tt
    halo = 16
    nh = N // halo
    full = lambda shape: pl.BlockSpec(shape, lambda i: (0,) * len(shape))
    o_spec = pl.BlockSpec((tt, C), lambda i: (i, 0))
    o2_spec = pl.BlockSpec((2, tt, C), lambda i: (0, i, 0))
    sds = jax.ShapeDtypeStruct((N, C), BF16)
    lw_sds = jax.ShapeDtypeStruct((2, N, C), F32)
    a_sds = jax.ShapeDtypeStruct((2, N, C), BF16)
    return pl.pallas_call(
        functools.partial(_rwkv_prep_kernel, tps=tps),
        out_shape=(sds, sds, sds, sds, lw_sds, a_sds, sds, sds),
        grid=(N // tt,),
        in_specs=[pl.BlockSpec((tt, W), lambda i: (i, col)),
                  pl.BlockSpec((halo, W), lambda i: (jnp.maximum(i * (tt // halo) - 1, 0), col)),
                  pl.BlockSpec((halo, W), lambda i: (jnp.minimum((i + 1) * (tt // halo), nh - 1), col)),
                  full((1, W)), full((2, C)), full((2, LANES, C)), full((2, C)), full((2, LANES, C)),
                  full((GATE_LORA, C)), full((1, C)), full((1, C)), full((1, C)), full((C, C))],
        out_specs=(o_spec, o_spec, o_spec, o_spec, o2_spec, o2_spec, o_spec, o_spec),
        compiler_params=_cparams("arbitrary"),
        name="rwkv_prep",
    )(proj, proj, proj, prm["mu"], prm["w0"], prm["wup"], prm["a0"], prm["aup"], prm["gup"],
      prm["k_k"], prm["k_a"], prm["r_k"], prm["bd"])


def _bmm(a, b):
    return lax.dot_general(a.astype(BF16), b.astype(BF16), (((2,), (1,)), ((0,), (0,))),
                           preferred_element_type=F32)


def _bmm_nt(a, b):
    return lax.dot_general(a.astype(BF16), b.astype(BF16), (((2,), (2,)), ((0,), (0,))),
                           preferred_element_type=F32)


def _bmm_tn(a, b):
    return lax.dot_general(a.astype(BF16), b.astype(BF16), (((1,), (1,)), ((0,), (0,))),
                           preferred_element_type=F32)


def _rwkv_scan_kernel(rf_ref, kf_ref, vf_ref, knf_ref, lwf_ref, af_ref,
                      rr_ref, kr_ref, vr_ref, knr_ref, lwr_ref, ar_ref, ka_ref, s0_ref,
                      yf_ref, yr_ref, sf_ref, S_ref, *, G):
    j = pl.program_id(1)

    @pl.when(j == 0)
    def _():
        S_ref[...] = s0_ref[...]

    Cc, N, H = RWKV_CHUNK, RWKV_HEAD, RWKV_HEADS
    R = G * Cc
    ti = lax.broadcasted_iota(jnp.int32, (R, R), 0)
    ii = lax.broadcasted_iota(jnp.int32, (R, R), 1)
    same = (ti // Cc) == (ii // Cc)
    m_all = same.astype(F32).astype(BF16)

    def heads(x):
        return jnp.stack([x[g * Cc:(g + 1) * Cc, h * N:(h + 1) * N] for g in range(G) for h in range(H)])

    per_dir = []
    for d, refs in enumerate(((rf_ref, kf_ref, vf_ref, knf_ref, lwf_ref, af_ref),
                              (rr_ref, kr_ref, vr_ref, knr_ref, lwr_ref, ar_ref))):
        r, k, v, kn, lw, a = [x[...].astype(F32) for x in refs]
        before = (ii <= ti) if d == 0 else (ii >= ti)
        m_incl = (same & before).astype(F32).astype(BF16)
        h1 = lw.astype(BF16)
        r1 = lw - h1.astype(F32)
        h2 = r1.astype(BF16)
        h3 = (r1 - h2.astype(F32)).astype(BF16)
        cum = (jnp.dot(m_incl, h1, preferred_element_type=F32)
               + jnp.dot(m_incl, h2, preferred_element_type=F32)
               + jnp.dot(m_incl, h3, preferred_element_type=F32))
        tot = (jnp.dot(m_all, h1, preferred_element_type=F32)
               + jnp.dot(m_all, h2, preferred_element_type=F32)
               + jnp.dot(m_all, h3, preferred_element_type=F32))
        e_neg = jnp.exp(-cum)
        e_rem = jnp.exp(tot - cum)
        b = kn * a
        kd = k * (1.0 + (a - 1.0) * ka_ref[...])
        per_dir.append([heads(x) for x in (-kn * jnp.exp(cum - lw), r * jnp.exp(cum), b * e_neg, kd * e_neg,
                                           b * e_rem, kd * e_rem, v, jnp.exp(tot))])

    A, Rt, Bt, Kt, Bh, Kh, V, Wc = [jnp.concatenate([per_dir[0][i], per_dir[1][i]], axis=0) for i in range(8)]
    nb = 2 * G * H
    bi = lax.broadcasted_iota(jnp.int32, (nb, Cc, Cc), 0)
    t3 = lax.broadcasted_iota(jnp.int32, (nb, Cc, Cc), 1)
    i3 = lax.broadcasted_iota(jnp.int32, (nb, Cc, Cc), 2)
    order = jnp.where(bi < G * H, t3 - i3, i3 - t3)
    strict = order > 0
    incl = order >= 0

    AR = jnp.concatenate([A, Rt], axis=1)
    ab = _bmm_nt(AR, Bt)
    ak = _bmm_nt(AR, Kt)
    Lab = jnp.where(strict, ab[:, :Cc], 0.0)
    Mrb = jnp.where(incl, ab[:, Cc:], 0.0)
    Lak = jnp.where(strict, ak[:, :Cc], 0.0)
    Mrk = jnp.where(incl, ak[:, Cc:], 0.0)
    lv = _bmm(jnp.concatenate([Lak, Mrk], axis=1), V)
    Z = jnp.concatenate([A, lv[:, :Cc]], axis=2)
    P = Lab
    steps = int(math.log2(Cc))
    for s in range(steps):
        if s < steps - 1:
            out = _bmm(P, jnp.concatenate([Z, P], axis=2))
            Z = Z + out[:, :, :2 * N]
            P = out[:, :, 2 * N:]
        else:
            Z = Z + _bmm(P, Z)
    mz = _bmm(Mrb, Z)
    Rh = Rt + mz[:, :, :N]
    Yh = mz[:, :, N:] + lv[:, Cc:]
    gh = _bmm_tn(Z, Bh)
    Gm = gh[:, :N]
    Hm = gh[:, N:] + _bmm_tn(V, Kh)

    S = S_ref[...].reshape(2 * H, N, N)
    for g in range(G):
        gr = G - 1 - g
        pick = lambda x: jnp.concatenate([x[g * H:(g + 1) * H], x[(G + gr) * H:(G + gr + 1) * H]], axis=0)
        y = _bmm_nt(pick(Rh), S) + pick(Yh)
        S = S * pick(Wc)[:, 0:1, :] + _bmm(S, pick(Gm)) + pick(Hm)
        yf_ref[g * Cc:(g + 1) * Cc, :] = jnp.concatenate([y[h] for h in range(H)], axis=1)
        yr_ref[gr * Cc:(gr + 1) * Cc, :] = jnp.concatenate([y[H + h] for h in range(H)], axis=1)
    S = S.reshape(2, H, N, N)
    S_ref[...] = S
    sf_ref[...] = S


def _rwkv_scan(r, k, v, kn, lw, a, k_a, s0, B, T):
    N, C = r.shape
    Cc = RWKV_CHUNK
    nc = T // Cc
    G = 2 if nc % 2 == 0 else 1
    ng = nc // G
    H, Nh = RWKV_HEADS, RWKV_HEAD
    fwd = pl.BlockSpec((G * Cc, C), lambda b, j: (b * ng + j, 0))
    rev = pl.BlockSpec((G * Cc, C), lambda b, j: (b * ng + ng - 1 - j, 0))
    fwd2 = pl.BlockSpec((None, G * Cc, C), lambda b, j: (0, b * ng + j, 0))
    rev2 = pl.BlockSpec((None, G * Cc, C), lambda b, j: (1, b * ng + ng - 1 - j, 0))
    st = pl.BlockSpec((2, None, H, Nh, Nh), lambda b, j: (0, b, 0, 0, 0))
    yf, yr, sf = pl.pallas_call(
        functools.partial(_rwkv_scan_kernel, G=G),
        out_shape=(jax.ShapeDtypeStruct((N, C), F32), jax.ShapeDtypeStruct((N, C), F32),
                   jax.ShapeDtypeStruct((2, B, H, Nh, Nh), F32)),
        grid=(B, ng),
        in_specs=[fwd, fwd, fwd, fwd, fwd2, fwd2, rev, rev, rev, rev, rev2, rev2,
                  pl.BlockSpec((1, C), lambda b, j: (0, 0)), st],
        out_specs=(fwd, rev, st),
        scratch_shapes=[pltpu.VMEM((2, H, Nh, Nh), F32)],
        compiler_params=_cparams("arbitrary", "arbitrary"),
        name="rwkv_scan",
    )(r, k, v, kn, lw, a, r, k, v, kn, lw, a, k_a, s0)
    return yf, yr, sf


def _rwkv_post_kernel(yf_ref, yr_ref, bonus_ref, g_ref, gg_ref, gb_ref, bd_ref, o_ref):
    y = yf_ref[...] + yr_ref[...]
    bd = bd_ref[...]
    inv_n = 1.0 / RWKV_HEAD
    mean = _dot_exact_rhs(y, bd) * inv_n
    yc = y - mean
    var = _dot_exact_rhs(yc * yc, bd) * inv_n
    yn = yc * lax.rsqrt(var + GN_EPS) * gg_ref[...] + gb_ref[...]
    o_ref[...] = ((yn + bonus_ref[...].astype(F32)) * g_ref[...].astype(F32)).astype(o_ref.dtype)


def _rwkv_post(yf, yr, bonus, g, gn_g, gn_b, bd):
    N, C = yf.shape
    tm = _pick(N, (1024, 512, 256))
    row = pl.BlockSpec((tm, C), lambda i: (i, 0))
    vec = pl.BlockSpec((1, C), lambda i: (0, 0))
    return pl.pallas_call(
        _rwkv_post_kernel,
        out_shape=jax.ShapeDtypeStruct((N, C), BF16),
        grid=(N // tm,),
        in_specs=[row, row, row, row, vec, vec, pl.BlockSpec((C, C), lambda i: (0, 0))],
        out_specs=row,
        compiler_params=_cparams("arbitrary"),
        name="rwkv_post",
    )(yf, yr, bonus, g, gn_g.reshape(1, C), gn_b.reshape(1, C), bd)


def _fourier_ch_kernel(p_ref, m_ref, o_ref):
    p = p_ref[...]
    mat = m_ref[...]
    G = FOURIER_GROUP_DIM
    for g in range(p.shape[1] // G):
        o_ref[:, g * G:(g + 1) * G] = jnp.dot(p[:, g * G:(g + 1) * G], mat,
                                              preferred_element_type=F32).astype(o_ref.dtype)


def _fourier_channel(proj, col, Fd, ch_mats, B, T):
    tm = _pick(T, (1024, 512, 256))
    nt = T // tm
    G = FOURIER_GROUP_DIM
    return pl.pallas_call(
        _fourier_ch_kernel,
        out_shape=jax.ShapeDtypeStruct((B, 2, T, Fd), BF16),
        grid=(B, nt, 2),
        in_specs=[pl.BlockSpec((tm, Fd), lambda b, i, s: (b * nt + i, col)),
                  pl.BlockSpec((None, G, G), lambda b, i, s: (s, 0, 0))],
        out_specs=pl.BlockSpec((None, None, tm, Fd), lambda b, i, s: (b, s, i, 0)),
        compiler_params=_cparams("arbitrary", "arbitrary", "arbitrary"),
        name="fourier_channel",
    )(proj, ch_mats)


def _dft_mats(n):
    idx = jnp.arange(n, dtype=jnp.int32)
    ph = (idx[:, None] * idx[None, :]) % n
    ang = ph.astype(F32) * (2.0 * math.pi / n)
    s = 1.0 / math.sqrt(n)
    return jnp.cos(ang) * s, jnp.sin(ang) * s


DFT_COLS = 8


def _dft1_kernel(z_ref, m1_ref, tc_ref, ts_ref, o_ref, *, T1):
    m1 = m1_ref[...]
    tc = tc_ref[...]
    ts = ts_ref[...]
    for j in range(DFT_COLS):
        zj = jnp.concatenate([z_ref[0, :, j, :], z_ref[1, :, j, :]], axis=0)
        a = jnp.dot(m1, zj, preferred_element_type=F32)
        ar = a[:T1]
        ai = a[T1:]
        c = tc[:, j:j + 1]
        s = ts[:, j:j + 1]
        o_ref[0, :, j, :] = (ar * c + ai * s).astype(o_ref.dtype)
        o_ref[1, :, j, :] = (ai * c - ar * s).astype(o_ref.dtype)


def _dft2_kernel(b_ref, m2_ref, o_ref):
    m2 = m2_ref[...]
    for j in range(DFT_COLS):
        rhs = jnp.concatenate([b_ref[0, j], b_ref[1, j]], axis=0)
        o_ref[:, j, :] = jnp.dot(m2, rhs, preferred_element_type=F32).astype(o_ref.dtype)


def _fourier_time(z, B, T, Fd):
    T2 = GRID_W
    T1 = T // T2
    if T1 % DFT_COLS or T2 % DFT_COLS:
        cos, sin = _dft_mats(T)
        mat = jnp.concatenate([cos, sin], axis=1).astype(BF16)
        zz = z.reshape(B, 2 * T, Fd)
        return jnp.concatenate([_mm(mat, zz[b], BF16) for b in range(B)], axis=0)
    c1, s1 = _dft_mats(T1)
    m1 = jnp.concatenate([jnp.concatenate([c1, s1], axis=1),
                          jnp.concatenate([-s1, c1], axis=1)], axis=0).astype(BF16)
    c2, s2 = _dft_mats(T2)
    m2 = jnp.concatenate([c2, s2], axis=1).astype(BF16)
    ph = (jnp.arange(T1, dtype=jnp.int32)[:, None] * jnp.arange(T2, dtype=jnp.int32)[None, :]) % T
    ang = ph.astype(F32) * (2.0 * math.pi / T)
    nt2 = T2 // DFT_COLS
    twc = jnp.cos(ang).reshape(T1, nt2, DFT_COLS).transpose(1, 0, 2)
    tws = jnp.sin(ang).reshape(T1, nt2, DFT_COLS).transpose(1, 0, 2)
    blk1 = pl.BlockSpec((None, 2, T1, DFT_COLS, Fd), lambda b, i: (b, 0, 0, i, 0))
    stage1 = pl.pallas_call(
        functools.partial(_dft1_kernel, T1=T1),
        out_shape=jax.ShapeDtypeStruct((B, 2, T1, T2, Fd), BF16),
        grid=(B, nt2),
        in_specs=[blk1,
                  pl.BlockSpec((2 * T1, 2 * T1), lambda b, i: (0, 0)),
                  pl.BlockSpec((None, T1, DFT_COLS), lambda b, i: (i, 0, 0)),
                  pl.BlockSpec((None, T1, DFT_COLS), lambda b, i: (i, 0, 0))],
        out_specs=blk1,
        compiler_params=_cparams("arbitrary", "arbitrary"),
        name="dft_stage1",
    )(z.reshape(B, 2, T1, T2, Fd), m1, twc, tws)
    out = pl.pallas_call(
        _dft2_kernel,
        out_shape=jax.ShapeDtypeStruct((B, T2, T1, Fd), BF16),
        grid=(B, T1 // DFT_COLS),
        in_specs=[pl.BlockSpec((None, 2, DFT_COLS, T2, Fd), lambda b, i: (b, 0, i, 0, 0)),
                  pl.BlockSpec((T2, 2 * T2), lambda b, i: (0, 0))],
        out_specs=pl.BlockSpec((None, T2, DFT_COLS, Fd), lambda b, i: (b, 0, i, 0)),
        compiler_params=_cparams("arbitrary", "arbitrary"),
        name="dft_stage2",
    )(stage1, m2)
    return out.reshape(B * T, Fd)


def _merge_kernel(fa_ref, at_ref, rw_ref, gt_ref, x_ref, g1_ref, wa_ref, wb_ref, wc_ref, wo_ref, o_ref):
    D = x_ref.shape[1]
    ya = jnp.dot(fa_ref[...], wa_ref[...], preferred_element_type=F32)
    yb = jnp.dot(at_ref[0], wb_ref[0], preferred_element_type=F32)
    for h in range(1, at_ref.shape[0]):
        yb = yb + jnp.dot(at_ref[h], wb_ref[h], preferred_element_type=F32)
    yc = jnp.dot(rw_ref[...], wc_ref[...], preferred_element_type=F32)
    gt = gt_ref[...].astype(F32)
    merged = (_sigmoid(gt[:, :D]) * ya + _sigmoid(gt[:, D:2 * D]) * yb + _sigmoid(gt[:, 2 * D:]) * yc)
    out = jnp.dot(merged.astype(BF16), wo_ref[...], preferred_element_type=F32)
    o_ref[...] = x_ref[...] + g1_ref[...] * out


def _merge(fa, attn, rw, proj, gcol, x, g1, wa, wb, wc, wo, B, T):
    N, D = x.shape
    H = attn.shape[1]
    Fd = rw.shape[1]
    tm = _pick(T, (512, 256))
    nt = T // tm
    Bm = g1.shape[0]
    full = lambda shape: pl.BlockSpec(shape, lambda b, i: (0,) * len(shape))
    return pl.pallas_call(
        _merge_kernel,
        out_shape=jax.ShapeDtypeStruct((N, D), F32),
        grid=(B, nt),
        in_specs=[pl.BlockSpec((tm, Fd), lambda b, i: (b * nt + i, 0)),
                  pl.BlockSpec((None, H, tm, LANES), lambda b, i: (b, 0, i, 0)),
                  pl.BlockSpec((tm, Fd), lambda b, i: (b * nt + i, 0)),
                  pl.BlockSpec((tm, 3 * D), lambda b, i: (b * nt + i, gcol)),
                  pl.BlockSpec((tm, D), lambda b, i: (b * nt + i, 0)),
                  pl.BlockSpec((None, 1, D), lambda b, i: (b % Bm, 0, 0)),
                  full(wa.shape), full(wb.shape), full(wc.shape), full(wo.shape)],
        out_specs=pl.BlockSpec((tm, D), lambda b, i: (b * nt + i, 0)),
        compiler_params=_cparams("arbitrary", "arbitrary"),
        name="merge_out",
    )(fa, attn, rw, proj, x, g1, wa, wb, wc, wo)


def _route(scores, sel, n_exp):
    tm = scores.shape[0]
    lane = lax.broadcasted_iota(jnp.int32, (tm, LANES), 1)
    neg = -jnp.inf
    sel = jnp.where(lane < n_exp, sel, neg)
    epg = n_exp // N_GROUPS
    assert epg == 4
    odd = (lane % 2) == 1
    p1 = jnp.where(odd, pltpu.roll(sel, 1, axis=1), pltpu.roll(sel, LANES - 1, axis=1))
    hi1, lo1 = jnp.maximum(sel, p1), jnp.minimum(sel, p1)
    upper = (lane % 4) >= 2
    hi2 = jnp.where(upper, pltpu.roll(hi1, 2, axis=1), pltpu.roll(hi1, LANES - 2, axis=1))
    lo2 = jnp.where(upper, pltpu.roll(lo1, 2, axis=1), pltpu.roll(lo1, LANES - 2, axis=1))
    gscore = jnp.maximum(hi1, hi2) + jnp.maximum(jnp.minimum(hi1, hi2), jnp.maximum(lo1, lo2))
    gscore = jnp.where(lane < n_exp, gscore, neg)
    gmax = jnp.max(gscore, axis=1, keepdims=True)
    lane_f = lane.astype(F32)
    grp = (lane // epg).astype(F32)
    big = float(LANES)
    best = jnp.min(jnp.where(gscore == gmax, grp, big), axis=1, keepdims=True)
    masked = jnp.where(grp == best, sel, neg)
    m1 = jnp.max(masked, axis=1, keepdims=True)
    i1 = jnp.min(jnp.where(masked == m1, lane_f, big), axis=1, keepdims=True)
    masked2 = jnp.where(lane_f == i1, neg, masked)
    m2 = jnp.max(masked2, axis=1, keepdims=True)
    i2 = jnp.min(jnp.where(masked2 == m2, lane_f, big), axis=1, keepdims=True)
    pick1 = lane_f == i1
    pick2 = lane_f == i2
    s1 = jnp.sum(jnp.where(pick1, scores, 0.0), axis=1, keepdims=True)
    s2 = jnp.sum(jnp.where(pick2, scores, 0.0), axis=1, keepdims=True)
    return pick1, pick2, s1 / (s1 + s2), s2 / (s1 + s2)


def _lane_to_row(col_vals):
    ones = jnp.ones((8, LANES), BF16)
    hi, lo = _split2(col_vals)
    return (_dot_nt(ones, hi) + _dot_nt(ones, lo))[0:1]


def _moe_kernel(x_ref, g_ref, sc_ref, sh_ref, g2_ref, rw_ref, rb_ref,
                wg_ref, wu_ref, wd_ref, swg_ref, swu_ref, swd_ref, o_ref,
                m_ref, xs_ref, ys_ref, pm_ref, ws_ref, oc_ref, *, n_exp):
    e = pl.program_id(1)
    tm = x_ref.shape[0]
    S = 2 * tm
    RB = MOE_ROWS

    @pl.when(e == 0)
    def _():
        x = x_ref[...]
        m = x * lax.rsqrt(jnp.mean(x * x, axis=-1, keepdims=True) + NORM_EPS) * g_ref[...]
        m = m * (1.0 + sc_ref[...]) + sh_ref[...]
        mb = m.astype(BF16)
        m_ref[...] = mb
        scores = _sigmoid(_dot_split(m, rw_ref[...]))
        pick1, pick2, w1, w2 = _route(scores, scores + rb_ref[...], n_exp)
        p1 = pick1.astype(F32)
        p2 = pick2.astype(F32)
        chosen = p1 + p2
        cnt = jnp.sum(chosen, axis=0, keepdims=True)
        li = lax.broadcasted_iota(jnp.int32, (LANES, LANES), 0)
        lj = lax.broadcasted_iota(jnp.int32, (LANES, LANES), 1)
        off = _dot_exact_rhs(jnp.broadcast_to(cnt, (8, LANES)), (li < lj).astype(F32).astype(BF16))[0:1]
        ti = lax.broadcasted_iota(jnp.int32, (tm, tm), 0)
        tj = lax.broadcasted_iota(jnp.int32, (tm, tm), 1)
        rank = jnp.dot((tj < ti).astype(F32).astype(BF16), chosen.astype(BF16),
                       preferred_element_type=F32)
        dest = off + rank
        d1 = _lane_to_row(p1 * dest)
        d2 = _lane_to_row(p2 * dest)
        w1r = _lane_to_row(jnp.broadcast_to(w1, (tm, LANES)) * p1)
        w2r = _lane_to_row(jnp.broadcast_to(w2, (tm, LANES)) * p2)
        row = lax.broadcasted_iota(jnp.int32, (S, tm), 0).astype(F32)
        is1 = row == d1
        is2 = row == d2
        perm = jnp.where(is1, 1.0, jnp.where(is2, 1.0, 0.0)).astype(BF16)
        pm_ref[...] = perm
        ws_ref[...] = jnp.sum(jnp.where(is1, w1r, jnp.where(is2, w2r, 0.0)), axis=1, keepdims=True)
        xs_ref[...] = jnp.dot(perm, mb, preferred_element_type=F32).astype(BF16)
        ys_ref[...] = jnp.zeros_like(ys_ref)
        oc_ref[0:1, :] = off
        oc_ref[1:2, :] = cnt

    @pl.when(e < n_exp)
    def _():
        lane = lax.broadcasted_iota(jnp.int32, (1, LANES), 1)
        o_e = jnp.sum(jnp.where(lane == e, oc_ref[0:1, :], 0.0)).astype(jnp.int32)
        c_e = jnp.sum(jnp.where(lane == e, oc_ref[1:2, :], 0.0)).astype(jnp.int32)
        wg = wg_ref[...].astype(BF16)
        wu = wu_ref[...].astype(BF16)
        wd = wd_ref[...].astype(BF16)
        b0 = o_e // RB
        b1 = jnp.where(c_e > 0, (o_e + c_e + RB - 1) // RB, b0)

        def body(b, carry):
            r0 = pl.multiple_of(b * RB, RB)
            xb = xs_ref[pl.ds(r0, RB), :]
            hg = jnp.dot(xb, wg, preferred_element_type=F32)
            hu = jnp.dot(xb, wu, preferred_element_type=F32)
            rows = r0 + lax.broadcasted_iota(jnp.int32, (RB, 1), 0)
            mine = (rows >= o_e) & (rows < o_e + c_e)
            wrow = jnp.where(mine, ws_ref[pl.ds(r0, RB), :], 0.0)
            hh = hg * _sigmoid(hg) * hu * wrow
            ys_ref[pl.ds(r0, RB), :] += jnp.dot(hh.astype(BF16), wd, preferred_element_type=F32)
            return carry

        lax.fori_loop(b0, b1, body, 0)

    @pl.when(e == n_exp)
    def _():
        m = m_ref[...]
        hg = jnp.dot(m, swg_ref[...], preferred_element_type=F32)
        hu = jnp.dot(m, swu_ref[...], preferred_element_type=F32)
        hh = hg * _sigmoid(hg) * hu
        y = jnp.dot(hh.astype(BF16), swd_ref[...], preferred_element_type=F32)
        y = y + _dot_tn(pm_ref[...], ys_ref[...])
        o_ref[...] = x_ref[...] + g2_ref[...] * y


def _moe(x, g, sc, sh, g2, rw_pad, rb_pad, wg, wu, wd, swg, swu, swd):
    N, D = x.shape
    E, _, Ff = wg.shape
    Bm = sc.shape[0]
    rpm = N // Bm
    tm = min(rpm, 1024)
    assert rpm % tm == 0 and (2 * tm) % MOE_ROWS == 0
    one = pl.Buffered(1)
    mod = pl.BlockSpec((None, 1, D), lambda i, e: (i * tm // rpm, 0, 0))
    full = lambda shape: pl.BlockSpec(shape, lambda i, e: (0,) * len(shape), pipeline_mode=one)
    return pl.pallas_call(
        functools.partial(_moe_kernel, n_exp=E),
        out_shape=jax.ShapeDtypeStruct((N, D), F32),
        grid=(N // tm, E + 1),
        in_specs=[pl.BlockSpec((tm, D), lambda i, e: (i, 0), pipeline_mode=one),
                  full((1, D)), mod, mod, mod,
                  full(rw_pad.shape), full(rb_pad.shape),
                  pl.BlockSpec((None, D, Ff), lambda i, e: (jnp.minimum(e, E - 1), 0, 0)),
                  pl.BlockSpec((None, D, Ff), lambda i, e: (jnp.minimum(e, E - 1), 0, 0)),
                  pl.BlockSpec((None, Ff, D), lambda i, e: (jnp.minimum(e, E - 1), 0, 0)),
                  full(swg.shape), full(swu.shape), full(swd.shape)],
        out_specs=pl.BlockSpec((tm, D), lambda i, e: (i, 0), pipeline_mode=one),
        scratch_shapes=[pltpu.VMEM((tm, D), BF16), pltpu.VMEM((2 * tm, D), BF16),
                        pltpu.VMEM((2 * tm, D), F32), pltpu.VMEM((2 * tm, tm), BF16),
                        pltpu.VMEM((2 * tm, 1), F32), pltpu.VMEM((8, LANES), F32)],
        compiler_params=pltpu.CompilerParams(dimension_semantics=("arbitrary", "arbitrary"),
                                             vmem_limit_bytes=MOE_VMEM_LIMIT_BYTES),
        name="moe",
    )(x, g.reshape(1, D), sc, sh, g2, rw_pad, rb_pad, wg, wu, wd, swg, swu, swd)


def _rope_partner(w):
    lead = w.shape[:-1]
    wr = w.reshape(lead + (2, 2, ROPE_PAIRS))
    return jnp.stack([-wr[..., 1, :], wr[..., 0, :]], axis=-2).reshape(w.shape)


def _rope_tables(T):
    rows = T // GRID_W
    row = jnp.repeat(jnp.arange(rows), GRID_W)
    col = jnp.tile(jnp.arange(GRID_W), rows)
    inv = ROPE_BASE ** (-jnp.arange(ROPE_PAIRS, dtype=F32) / ROPE_PAIRS)
    ang = jnp.stack([row, col], axis=-1).astype(F32)[..., None] * inv
    cos = jnp.broadcast_to(jnp.cos(ang)[:, :, None, :], (T, 2, 2, ROPE_PAIRS)).reshape(T, MLA_ROPE)
    sin = jnp.broadcast_to(jnp.sin(ang)[:, :, None, :], (T, 2, 2, ROPE_PAIRS)).reshape(T, MLA_ROPE)
    pad = LANES - MLA_QK
    z_n, z_p = jnp.zeros((T, MLA_NOPE), F32), jnp.zeros((T, pad), F32)
    cq = jnp.concatenate([jnp.ones((T, MLA_NOPE), F32), cos, z_p], axis=1)
    sq = jnp.concatenate([z_n, sin, z_p], axis=1)
    ck = jnp.concatenate([z_n, cos, z_p], axis=1)
    return cq, sq, ck, sq


def _identity_tables(T):
    pad = LANES - MLA_QK
    z_n, z_p = jnp.zeros((T, MLA_NOPE), F32), jnp.zeros((T, pad), F32)
    one_r = jnp.ones((T, MLA_ROPE), F32)
    cq = jnp.concatenate([jnp.ones((T, MLA_NOPE), F32), one_r, z_p], axis=1)
    ck = jnp.concatenate([z_n, one_r, z_p], axis=1)
    zero = jnp.zeros((T, LANES), F32)
    return cq, zero, ck, zero


def kernel(x, c, ctx, c_ctx, w_ada, b_ada, norm1_g, norm2_g, w_in, q_norm_g, kv_norm_g, w_uq, w_ukv, rwkv_mu, rwkv_w0, rwkv_w_up, rwkv_a0, rwkv_a_up, rwkv_g_up, rwkv_k_k, rwkv_k_a, rwkv_r_k, rwkv_gn_g, rwkv_gn_b, w_proj_a, w_proj_b, w_proj_c, w_out, router_w, router_b, exp_w_gate, exp_w_up, exp_w_down, sh_w_gate, sh_w_up, sh_w_down, final_norm_g):
    B, T, D = x.shape
    Tc = ctx.shape[1]
    L = w_ada.shape[0]
    H = MLA_HEADS
    QL, KVL = q_norm_g.shape[1], kv_norm_g.shape[1]
    Fd = w_proj_a.shape[1]
    C = RWKV_DIM
    E = router_w.shape[1]
    N, Nc = B * T, B * Tc

    sizes = (Fd, QL, KVL, MLA_ROPE, 3 * C + DECAY_LORA + AAA_LORA + GATE_LORA, D, D, D)
    offs = np.concatenate([[0], np.cumsum(sizes)])

    lat_tabs = _rope_tables(T)
    ctx_tabs = _identity_tables(Tc)
    cc, cs = _dft_mats(FOURIER_GROUP_DIM)
    ch_mats = jnp.stack([cc, -cs]).astype(BF16)
    hid = jnp.arange(C) // RWKV_HEAD
    bd = (hid[:, None] == hid[None, :]).astype(BF16)
    rw_pad = jnp.zeros((D, LANES), F32).at[:, :E].set(router_w)
    rb_pad = jnp.zeros((1, LANES), F32).at[0, :E].set(router_b)

    R = -(-(B + 1) // 8) * 8
    c_rows = jnp.zeros((R, D), F32).at[:B].set(c).at[B].set(c_ctx)
    mod = _ada_mod(c_rows, w_ada, b_ada)

    x = x.reshape(N, D)
    ctx = ctx.reshape(Nc, D)
    zero_state = jnp.zeros((2, B, RWKV_HEADS, RWKV_HEAD, RWKV_HEAD), F32)

    for l in range(L):
        last = l == L - 1
        ml = mod[l, :B].reshape(B, 1, 6, D)
        mc = mod[l, B:B + 1].reshape(1, 1, 6, D)
        sh1, sc1, g1, sh2, sc2, g2 = [ml[:, :, j] for j in range(6)]
        csh1, csc1, cg1, csh2, csc2, cg2 = [mc[:, :, j] for j in range(6)]

        wi = w_in[l]
        w_kr = wi[:, offs[3]:offs[4]]
        zl = jnp.zeros((D, MLA_NOPE), F32)
        zr = jnp.zeros((D, LANES - MLA_QK), F32)
        segs = [wi[:, offs[5]:offs[8]],
                wi[:, offs[0]:offs[1]],
                wi[:, offs[4]:offs[5]],
                jnp.concatenate([wi[:, offs[1]:offs[3]], zl, w_kr, zr, zl, _rope_partner(w_kr), zr], axis=1)]
        seg_off, n_all = _column_layout([s.shape[1] for s in segs])
        pieces, cur = [], 0
        for s, o in zip(segs, seg_off):
            if o > cur:
                pieces.append(jnp.zeros((D, o - cur), F32))
            pieces.append(s)
            cur = o + s.shape[1]
        if n_all > cur:
            pieces.append(jnp.zeros((D, n_all - cur), F32))
        w_all = jnp.concatenate(pieces, axis=1).astype(BF16)
        gt_col, pf_col, rw_col, qkv_col = [o // s.shape[1] for s, o in zip(segs, seg_off)]
        uq = w_uq[l].reshape(QL, H, MLA_QK).transpose(1, 0, 2)
        padq = jnp.zeros((H, QL, LANES - MLA_QK), F32)
        wqa = jnp.concatenate([uq, padq], axis=2).astype(BF16)
        wqb = jnp.concatenate([jnp.zeros((H, QL, MLA_NOPE), F32), _rope_partner(uq[..., MLA_NOPE:]), padq],
                              axis=2).astype(BF16)
        ukv = w_ukv[l].reshape(KVL, H, MLA_NOPE + MLA_V).transpose(1, 0, 2)
        wk = jnp.concatenate([ukv[..., :MLA_NOPE], jnp.zeros((H, KVL, LANES - MLA_NOPE), F32)], axis=2).astype(BF16)
        wv = jnp.concatenate([ukv[..., MLA_NOPE:], jnp.zeros((H, KVL, LANES - MLA_V), F32)], axis=2).astype(BF16)
        wb = jnp.concatenate([w_proj_b[l].reshape(H, MLA_V, D), jnp.zeros((H, LANES - MLA_V, D), F32)],
                             axis=1).astype(BF16)
        wa = w_proj_a[l].astype(BF16)
        wc = w_proj_c[l].astype(BF16)
        wo = w_out[l].astype(BF16)
        zd = jnp.zeros((2, DECAY_LORA, C), F32)
        prm = dict(
            mu=rwkv_mu[l].reshape(1, -1), w0=rwkv_w0[l], a0=rwkv_a0[l],
            wup=jnp.concatenate([rwkv_w_up[l], zd], axis=1).astype(BF16),
            aup=jnp.concatenate([zd, rwkv_a_up[l]], axis=1).astype(BF16),
            gup=rwkv_g_up[l].astype(BF16), k_k=rwkv_k_k[l].reshape(1, C), k_a=rwkv_k_a[l].reshape(1, C),
            r_k=rwkv_r_k[l].reshape(1, C), bd=bd)

        def mixers(xx, sc, sh, tabs, Tx):
            proj = _in_proj(xx, norm1_g[l], sc, sh, w_all)
            q, k, v = _qkv_project(proj, qkv_col, q_norm_g[l], kv_norm_g[l], wqa, wqb, wk, wv, tabs, B, Tx)
            rp = _rwkv_prep(proj, rw_col, prm, B, Tx)
            return proj, q, k, v, rp

        proj_l, q_l, k_l, v_l, rp_l = mixers(x, sc1, sh1, lat_tabs, T)
        proj_c, q_c, k_c, v_c, rp_c = mixers(ctx, csc1, csh1, ctx_tabs, Tc)

        attn_l = _attention(q_l, [(k_l, v_l), (k_c, v_c)])

        yf_c, yr_c, s_ctx = _rwkv_scan(*rp_c[:6], prm["k_a"], zero_state, B, Tc)
        yf_l, yr_l, _ = _rwkv_scan(*rp_l[:6], prm["k_a"], s_ctx, B, T)
        rwo_l = _rwkv_post(yf_l, yr_l, rp_l[7], rp_l[6], rwkv_gn_g[l], rwkv_gn_b[l], bd)

        fa_l = _fourier_time(_fourier_channel(proj_l, pf_col, Fd, ch_mats, B, T), B, T, Fd)

        x = _merge(fa_l, attn_l, rwo_l, proj_l, gt_col, x, g1, wa, wb, wc, wo, B, T)
        moe_w = (rw_pad, rb_pad, exp_w_gate[l], exp_w_up[l], exp_w_down[l],
                 sh_w_gate[l].astype(BF16), sh_w_up[l].astype(BF16), sh_w_down[l].astype(BF16))
        x = _moe(x, norm2_g[l], sc2, sh2, g2, *moe_w)

        if not last:
            attn_c = _attention(q_c, [(k_c, v_c)])
            rwo_c = _rwkv_post(yf_c, yr_c, rp_c[7], rp_c[6], rwkv_gn_g[l], rwkv_gn_b[l], bd)
            fa_c = _fourier_time(_fourier_channel(proj_c, pf_col, Fd, ch_mats, B, Tc), B, Tc, Fd)
            ctx = _merge(fa_c, attn_c, rwo_c, proj_c, gt_col, ctx, cg1, wa, wb, wc, wo, B, Tc)
            ctx = _moe(ctx, norm2_g[l], csc2, csh2, cg2, *moe_w)

    ones = jnp.ones((1, 1, D), F32)
    out = _norm(x, final_norm_g, ones, ones, F32, modulate=False)
    return out.reshape(B, T, D)
```

```python
import functools
import math

import numpy as np
import jax
import jax.numpy as jnp
from jax import lax
from jax.experimental import pallas as pl
from jax.experimental.pallas import tpu as pltpu

F32 = jnp.float32
BF16 = jnp.bfloat16

GRID_W = 64
FOURIER_GROUP_DIM = 128
MLA_HEADS = 8
MLA_NOPE = 64
MLA_ROPE = 32
MLA_QK = MLA_NOPE + MLA_ROPE
MLA_V = 64
MLA_SCALE = MLA_QK ** -0.5
QK_PRESCALE = MLA_SCALE * math.log2(math.e)
ROPE_PAIRS = MLA_ROPE // 4
ROPE_BASE = 10000.0
RWKV_HEADS = 8
RWKV_HEAD = 64
RWKV_DIM = RWKV_HEADS * RWKV_HEAD
DECAY_LORA = 64
AAA_LORA = 64
GATE_LORA = 128
GN_EPS = 64e-5
N_GROUPS = 4
NORM_EPS = 1e-6

LANES = 128
VMEM_LIMIT_BYTES = 48 * 1024 * 1024
RWKV_CHUNK = 64
MOE_ROWS = 128
MOE_VMEM_LIMIT_BYTES = 56 * 1024 * 1024


def _cparams(*sem):
    return pltpu.CompilerParams(dimension_semantics=sem, vmem_limit_bytes=VMEM_LIMIT_BYTES)


def _sigmoid(z):
    return 1.0 / (1.0 + jnp.exp(-z))


def _bdot(a, b):
    return jnp.dot(a.astype(BF16), b.astype(BF16), preferred_element_type=F32)


def _dot_nt(a, b):
    return lax.dot_general(a.astype(BF16), b.astype(BF16), (((1,), (1,)), ((), ())),
                           preferred_element_type=F32)


def _dot_tn(a, b):
    return lax.dot_general(a.astype(BF16), b.astype(BF16), (((0,), (0,)), ((), ())),
                           preferred_element_type=F32)


def _split2(x):
    hi = x.astype(BF16)
    lo = (x - hi.astype(F32)).astype(BF16)
    return hi, lo


def _dot_exact_rhs(x, ones_bf16):
    hi, lo = _split2(x)
    return (jnp.dot(hi, ones_bf16, preferred_element_type=F32)
            + jnp.dot(lo, ones_bf16, preferred_element_type=F32))


def _dot_split(a, b):
    ah, al = _split2(a)
    bh, bl = _split2(b)
    return (jnp.dot(ah, bh, preferred_element_type=F32)
            + jnp.dot(ah, bl, preferred_element_type=F32)
            + jnp.dot(al, bh, preferred_element_type=F32))


def _ada_kernel(c_ref, w_ref, b_ref, o_ref):
    c = c_ref[...]
    s = c * _sigmoid(c)
    o_ref[...] = _dot_split(s, w_ref[...]) + b_ref[...]


def _ada_mod(c_rows, w_ada, b_ada):
    L, D, D6 = w_ada.shape
    R = c_rows.shape[0]
    tn = 1536 if D6 % 1536 == 0 else D6
    return pl.pallas_call(
        _ada_kernel,
        out_shape=jax.ShapeDtypeStruct((L, R, D6), F32),
        grid=(L, D6 // tn),
        in_specs=[pl.BlockSpec((R, D), lambda l, j: (0, 0)),
                  pl.BlockSpec((None, D, tn), lambda l, j: (l, 0, j)),
                  pl.BlockSpec((None, 1, tn), lambda l, j: (l, 0, j))],
        out_specs=pl.BlockSpec((None, R, tn), lambda l, j: (l, 0, j)),
        compiler_params=_cparams("arbitrary", "arbitrary"),
        name="ada_mod",
    )(c_rows, w_ada, b_ada.reshape(L, 1, D6))


def _norm_kernel(x_ref, g_ref, sc_ref, sh_ref, o_ref, *, modulate):
    x = x_ref[...].astype(F32)
    y = x * lax.rsqrt(jnp.mean(x * x, axis=-1, keepdims=True) + NORM_EPS) * g_ref[...]
    if modulate:
        y = y * (1.0 + sc_ref[...]) + sh_ref[...]
    o_ref[...] = y.astype(o_ref.dtype)


def _norm(x, g, sc, sh, out_dtype, modulate=True):
    N, D = x.shape
    Bm = sc.shape[0]
    rpm = N // Bm
    tm = min(rpm, 1024)
    assert rpm % tm == 0
    return pl.pallas_call(
        functools.partial(_norm_kernel, modulate=modulate),
        out_shape=jax.ShapeDtypeStruct((N, D), out_dtype),
        grid=(N // tm,),
        in_specs=[pl.BlockSpec((tm, D), lambda i: (i, 0)),
                  pl.BlockSpec((1, D), lambda i: (0, 0)),
                  pl.BlockSpec((None, 1, D), lambda i: (i * tm // rpm, 0, 0)),
                  pl.BlockSpec((None, 1, D), lambda i: (i * tm // rpm, 0, 0))],
        out_specs=pl.BlockSpec((tm, D), lambda i: (i, 0)),
        compiler_params=_cparams("arbitrary"),
        name="rmsnorm_mod",
    )(x, g.reshape(1, D), sc, sh)


def _in_proj_kernel(x_ref, g_ref, sc_ref, sh_ref, w_ref, o_ref, h_ref):
    @pl.when(pl.program_id(1) == 0)
    def _():
        x = x_ref[...]
        y = x * lax.rsqrt(jnp.mean(x * x, axis=-1, keepdims=True) + NORM_EPS) * g_ref[...]
        h_ref[...] = (y * (1.0 + sc_ref[...]) + sh_ref[...]).astype(BF16)

    o_ref[...] = jnp.dot(h_ref[...], w_ref[...], preferred_element_type=F32).astype(o_ref.dtype)


def _in_proj(x, g, sc, sh, w_all):
    N, D = x.shape
    Nout = w_all.shape[1]
    Bm = sc.shape[0]
    rpm = N // Bm
    tm = min(rpm, 512)
    assert rpm % tm == 0
    tn = Nout
    mod = pl.BlockSpec((None, 1, D), lambda i, j: (i * tm // rpm, 0, 0))
    return pl.pallas_call(
        _in_proj_kernel,
        out_shape=jax.ShapeDtypeStruct((N, Nout), BF16),
        grid=(N // tm, Nout // tn),
        in_specs=[pl.BlockSpec((tm, D), lambda i, j: (i, 0)),
                  pl.BlockSpec((1, D), lambda i, j: (0, 0)), mod, mod,
                  pl.BlockSpec((D, tn), lambda i, j: (0, j), pipeline_mode=pl.Buffered(1))],
        out_specs=pl.BlockSpec((tm, tn), lambda i, j: (i, j)),
        scratch_shapes=[pltpu.VMEM((tm, D), BF16)],
        compiler_params=_cparams("arbitrary", "arbitrary"),
        name="in_proj",
    )(x, g.reshape(1, D), sc, sh, w_all)


def _column_layout(widths):
    offs, cur = [], 0
    for w in widths:
        cur = -(-cur // w) * w
        offs.append(cur)
        cur += w
    return offs, -(-cur // LANES) * LANES


def _mm_kernel(a_ref, b_ref, o_ref):
    o_ref[...] = jnp.dot(a_ref[...], b_ref[...], preferred_element_type=F32).astype(o_ref.dtype)


def _mm_acc_kernel(a_ref, b_ref, o_ref, acc_ref, *, nk):
    k = pl.program_id(2)

    @pl.when(k == 0)
    def _():
        acc_ref[...] = jnp.zeros_like(acc_ref)

    acc_ref[...] += jnp.dot(a_ref[...], b_ref[...], preferred_element_type=F32)

    @pl.when(k == nk - 1)
    def _():
        o_ref[...] = acc_ref[...].astype(o_ref.dtype)


def _pick(n, prefs):
    for p in prefs:
        if n % p == 0:
            return p
    return n


def _mm(a, w, out_dtype, tn=None, tk=None):
    M, K = a.shape
    Nn = w.shape[1]
    tm = _pick(M, (1024, 512, 256))
    tn = tn or _pick(Nn, (1024, 896, 512))
    tk = tk or K
    if tk == K:
        return pl.pallas_call(
            _mm_kernel,
            out_shape=jax.ShapeDtypeStruct((M, Nn), out_dtype),
            grid=(M // tm, Nn // tn),
            in_specs=[pl.BlockSpec((tm, K), lambda i, j: (i, 0)),
                      pl.BlockSpec((K, tn), lambda i, j: (0, j))],
            out_specs=pl.BlockSpec((tm, tn), lambda i, j: (i, j)),
            compiler_params=_cparams("arbitrary", "arbitrary"),
            name="matmul",
        )(a, w)
    nk = K // tk
    return pl.pallas_call(
        functools.partial(_mm_acc_kernel, nk=nk),
        out_shape=jax.ShapeDtypeStruct((M, Nn), out_dtype),
        grid=(M // tm, Nn // tn, nk),
        in_specs=[pl.BlockSpec((tm, tk), lambda i, j, k: (i, k)),
                  pl.BlockSpec((tk, tn), lambda i, j, k: (k, j))],
        out_specs=pl.BlockSpec((tm, tn), lambda i, j, k: (i, j)),
        scratch_shapes=[pltpu.VMEM((tm, tn), F32)],
        compiler_params=_cparams("arbitrary", "arbitrary", "arbitrary"),
        name="matmul_kacc",
    )(a, w)


def _qkv_kernel(p_ref, gq_ref, gkv_ref, wqa_ref, wqb_ref, wk_ref, wv_ref,
                cq_ref, sq_ref, ck_ref, sk_ref, q_ref, k_ref, v_ref, *, ql, kvl):
    p = p_ref[...].astype(F32)
    pq = p[:, :ql]
    pkv = p[:, ql:ql + kvl]
    kra = p[:, ql + kvl:ql + kvl + LANES]
    krb = p[:, ql + kvl + LANES:ql + kvl + 2 * LANES]
    qn = (pq * lax.rsqrt(jnp.mean(pq * pq, axis=-1, keepdims=True) + NORM_EPS) * gq_ref[...]).astype(BF16)
    kvn = (pkv * lax.rsqrt(jnp.mean(pkv * pkv, axis=-1, keepdims=True) + NORM_EPS) * gkv_ref[...]).astype(BF16)
    kr = kra * ck_ref[...] + krb * sk_ref[...]
    cq = cq_ref[...] * QK_PRESCALE
    sq = sq_ref[...] * QK_PRESCALE
    for h in range(q_ref.shape[0]):
        qa = jnp.dot(qn, wqa_ref[h], preferred_element_type=F32)
        qb = jnp.dot(qn, wqb_ref[h], preferred_element_type=F32)
        q_ref[h] = (qa * cq + qb * sq).astype(q_ref.dtype)
        k_ref[h] = (jnp.dot(kvn, wk_ref[h], preferred_element_type=F32) + kr).astype(k_ref.dtype)
        v_ref[h] = jnp.dot(kvn, wv_ref[h], preferred_element_type=F32).astype(v_ref.dtype)


def _qkv_project(proj, col, gq, gkv, wqa, wqb, wk, wv, tabs, B, T):
    N = proj.shape[0]
    H = wqa.shape[0]
    ql, kvl = wqa.shape[1], wk.shape[1]
    W = ql + kvl + 2 * LANES
    tm = _pick(T, (512, 256))
    nt = T // tm
    cq, sq, ck, sk = tabs
    full = lambda shape: pl.BlockSpec(shape, lambda i: (0,) * len(shape))
    tab_spec = pl.BlockSpec((tm, LANES), lambda i: (i % nt, 0))
    out_spec = pl.BlockSpec((None, H, tm, LANES), lambda i: (i // nt, 0, i % nt, 0))
    out_sds = jax.ShapeDtypeStruct((B, H, T, LANES), BF16)
    return pl.pallas_call(
        functools.partial(_qkv_kernel, ql=ql, kvl=kvl),
        out_shape=(out_sds, out_sds, out_sds),
        grid=(N // tm,),
        in_specs=[pl.BlockSpec((tm, W), lambda i: (i, col)),
                  full((1, ql)), full((1, kvl)),
                  full(wqa.shape), full(wqb.shape), full(wk.shape), full(wv.shape),
                  tab_spec, tab_spec, tab_spec, tab_spec],
        out_specs=(out_spec, out_spec, out_spec),
        compiler_params=_cparams("arbitrary"),
        name="mla_qkv",
    )(proj, gq.reshape(1, ql), gkv.reshape(1, kvl), wqa, wqb, wk, wv, cq, sq, ck, sk)


def _attn_kernel(q_ref, *refs, tiles):
    o_ref = refs[-1]
    out = None
    for hh in range(q_ref.shape[0]):
        q = q_ref[hh]
        m = l = acc = None
        for src, start, size in tiles:
            k = refs[2 * src][hh, start:start + size, :]
            v = refs[2 * src + 1][hh, start:start + size, :]
            s = _dot_nt(q, k)
            mt = jnp.max(s, axis=-1, keepdims=True)
            if m is None:
                m_new = mt
                p = jnp.exp2(s - m_new)
                l = jnp.sum(p, axis=-1, keepdims=True)
                acc = jnp.dot(p.astype(BF16), v, preferred_element_type=F32)
            else:
                m_new = jnp.maximum(m, mt)
                alpha = jnp.exp2(m - m_new)
                p = jnp.exp2(s - m_new)
                l = alpha * l + jnp.sum(p, axis=-1, keepdims=True)
                acc = alpha * acc + jnp.dot(p.astype(BF16), v, preferred_element_type=F32)
            m = m_new
        o = acc / l
        out = o if out is None else out + o
    o_ref[...] = out.astype(o_ref.dtype)


def _attention(q, kvs):
    B, H, T, dq = q.shape
    HP = 2
    tq = _pick(T, (1024, 512, 256))
    tiles = []
    for src, (k, _) in enumerate(kvs):
        Tk = k.shape[2]
        tk = _pick(Tk, (1024, 512, 256))
        tiles += [(src, st, tk) for st in range(0, Tk, tk)]
    in_specs = [pl.BlockSpec((None, HP, tq, dq), lambda b, h, i: (b, h, i, 0))]
    args = [q]
    for k, v in kvs:
        Tk = k.shape[2]
        in_specs += [pl.BlockSpec((None, HP, Tk, dq), lambda b, h, i: (b, h, 0, 0)),
                     pl.BlockSpec((None, HP, Tk, LANES), lambda b, h, i: (b, h, 0, 0))]
        args += [k, v]
    return pl.pallas_call(
        functools.partial(_attn_kernel, tiles=tuple(tiles)),
        out_shape=jax.ShapeDtypeStruct((B, H // HP, T, LANES), BF16),
        grid=(B, H // HP, T // tq),
        in_specs=in_specs,
        out_specs=pl.BlockSpec((None, None, tq, LANES), lambda b, h, i: (b, h, i, 0)),
        compiler_params=_cparams("arbitrary", "arbitrary", "arbitrary"),
        name="attention",
    )(*args)


def _softplus(u):
    return jnp.maximum(u, 0.0) + jnp.log(1.0 + jnp.exp(-jnp.abs(u)))


def _rwkv_prep_kernel(p_ref, pp_ref, pn_ref, mu_ref, w0_ref, wup_ref, a0_ref, aup_ref, gup_ref,
                      kk_ref, ka_ref, rk_ref, bd_ref,
                      r_o, k_o, v_o, kn_o, lw_o, a_o, g_o, bonus_o, *, tps):
    i = pl.program_id(0)
    p = p_ref[...].astype(F32)
    tt = p.shape[0]
    C = RWKV_DIM
    halo = pp_ref.shape[0]
    prev_row = jnp.where(i % tps == 0, 0.0, pp_ref[...].astype(F32)[halo - 1:halo, :])
    next_row = jnp.where(i % tps == tps - 1, 0.0, pn_ref[...].astype(F32)[0:1, :])
    rows = lax.broadcasted_iota(jnp.int32, (tt, 1), 0)
    pprev = jnp.where(rows == 0, prev_row, pltpu.roll(p, 1, axis=0))
    pnext = jnp.where(rows == tt - 1, next_row, pltpu.roll(p, tt - 1, axis=0))
    xs = p + (0.5 * (pprev + pnext) - p) * mu_ref[...]
    r = xs[:, :C]
    k = xs[:, C:2 * C]
    v = xs[:, 2 * C:3 * C]
    xwa = xs[:, 3 * C:3 * C + DECAY_LORA + AAA_LORA]
    xg = xs[:, 3 * C + DECAY_LORA + AAA_LORA:]
    bd = bd_ref[...]
    g_o[...] = _bdot(_sigmoid(xg), gup_ref[...]).astype(g_o.dtype)
    kn = k * kk_ref[...]
    kn = kn * lax.rsqrt(_dot_exact_rhs(kn * kn, bd) + 1e-12)
    th = jnp.tanh(xwa)
    bonus = jnp.zeros_like(r)
    for d in range(2):
        w = -_softplus(-(w0_ref[d:d + 1, :] + _bdot(th, wup_ref[d]))) - 0.5
        lw_o[d] = -jnp.exp(w)
        a = _sigmoid(a0_ref[d:d + 1, :] + _bdot(xwa, aup_ref[d]))
        a_o[d] = a.astype(a_o.dtype)
        kd = k * (1.0 + (a - 1.0) * ka_ref[...])
        bonus = bonus + _dot_exact_rhs(r * kd * rk_ref[...], bd) * v
    r_o[...] = r.astype(r_o.dtype)
    k_o[...] = k.astype(k_o.dtype)
    v_o[...] = v.astype(v_o.dtype)
    kn_o[...] = kn.astype(kn_o.dtype)
    bonus_o[...] = bonus.astype(bonus_o.dtype)


def _rwkv_prep(proj, col, prm, B, T):
    N = proj.shape[0]
    W = prm["mu"].shape[1]
    C = RWKV_DIM
    tt = _pick(T, (256,))
    tps = T // tt
    halo = 16
    nh = N // halo
    full = lambda shape: pl.BlockSpec(shape, lambda i: (0,) * len(shape))
    o_spec = pl.BlockSpec((tt, C), lambda i: (i, 0))
    o2_spec = pl.BlockSpec((2, tt, C), lambda i: (0, i, 0))
    sds = jax.ShapeDtypeStruct((N, C), BF16)
    lw_sds = jax.ShapeDtypeStruct((2, N, C), F32)
    a_sds = jax.ShapeDtypeStruct((2, N, C), BF16)
    return pl.pallas_call(
        functools.partial(_rwkv_prep_kernel, tps=tps),
        out_shape=(sds, sds, sds, sds, lw_sds, a_sds, sds, sds),
        grid=(N // tt,),
        in_specs=[pl.BlockSpec((tt, W), lambda i: (i, col)),
                  pl.BlockSpec((halo, W), lambda i: (jnp.maximum(i * (tt // halo) - 1, 0), col)),
                  pl.BlockSpec((halo, W), lambda i: (jnp.minimum((i + 1) * (tt // halo), nh - 1), col)),
                  full((1, W)), full((2, C)), full((2, LANES, C)), full((2, C)), full((2, LANES, C)),
                  full((GATE_LORA, C)), full((1, C)), full((1, C)), full((1, C)), full((C, C))],
        out_specs=(o_spec, o_spec, o_spec, o_spec, o2_spec, o2_spec, o_spec, o_spec),
        compiler_params=_cparams("arbitrary"),
        name="rwkv_prep",
    )(proj, proj, proj, prm["mu"], prm["w0"], prm["wup"], prm["a0"], prm["aup"], prm["gup"],
      prm["k_k"], prm["k_a"], prm["r_k"], prm["bd"])


def _bmm(a, b):
    return lax.dot_general(a.astype(BF16), b.astype(BF16), (((2,), (1,)), ((0,), (0,))),
                           preferred_element_type=F32)


def _bmm_nt(a, b):
    return lax.dot_general(a.astype(BF16), b.astype(BF16), (((2,), (2,)), ((0,), (0,))),
                           preferred_element_type=F32)


def _bmm_tn(a, b):
    return lax.dot_general(a.astype(BF16), b.astype(BF16), (((1,), (1,)), ((0,), (0,))),
                           preferred_element_type=F32)


def _rwkv_scan_kernel(rf_ref, kf_ref, vf_ref, knf_ref, lwf_ref, af_ref,
                      rr_ref, kr_ref, vr_ref, knr_ref, lwr_ref, ar_ref, ka_ref, s0_ref,
                      yf_ref, yr_ref, sf_ref, S_ref, *, G):
    j = pl.program_id(1)

    @pl.when(j == 0)
    def _():
        S_ref[...] = s0_ref[...]

    Cc, N, H = RWKV_CHUNK, RWKV_HEAD, RWKV_HEADS
    R = G * Cc
    ti = lax.broadcasted_iota(jnp.int32, (R, R), 0)
    ii = lax.broadcasted_iota(jnp.int32, (R, R), 1)
    same = (ti // Cc) == (ii // Cc)

    def heads(x):
        return jnp.stack([x[g * Cc:(g + 1) * Cc, h * N:(h + 1) * N] for g in range(G) for h in range(H)])

    per_dir = []
    for d, refs in enumerate(((rf_ref, kf_ref, vf_ref, knf_ref, lwf_ref, af_ref),
                              (rr_ref, kr_ref, vr_ref, knr_ref, lwr_ref, ar_ref))):
        r, k, v, kn, lw, a = [x[...].astype(F32) for x in refs]
        before = (ii <= ti) if d == 0 else (ii >= ti)
        m_incl = (same & before).astype(F32).astype(BF16)
        h1 = lw.astype(BF16)
        r1 = lw - h1.astype(F32)
        h2 = r1.astype(BF16)
        h3 = (r1 - h2.astype(F32)).astype(BF16)
        cum = (jnp.dot(m_incl, h1, preferred_element_type=F32)
               + jnp.dot(m_incl, h2, preferred_element_type=F32)
               + jnp.dot(m_incl, h3, preferred_element_type=F32))
        last = [(g + 1) * Cc - 1 if d == 0 else g * Cc for g in range(G)]
        tot = jnp.concatenate([jnp.broadcast_to(cum[t:t + 1, :], (Cc, cum.shape[1])) for t in last],
                              axis=0)
        e_neg = jnp.exp(-cum)
        e_rem = jnp.exp(tot - cum)
        b = kn * a
        kd = k * (1.0 + (a - 1.0) * ka_ref[...])
        per_dir.append([heads(x) for x in (-kn * jnp.exp(cum - lw), r * jnp.exp(cum), b * e_neg, kd * e_neg,
                                           b * e_rem, kd * e_rem, v, jnp.exp(tot))])

    A, Rt, Bt, Kt, Bh, Kh, V, Wc = [jnp.concatenate([per_dir[0][i], per_dir[1][i]], axis=0) for i in range(8)]
    nb = 2 * G * H
    bi = lax.broadcasted_iota(jnp.int32, (nb, Cc, Cc), 0)
    t3 = lax.broadcasted_iota(jnp.int32, (nb, Cc, Cc), 1)
    i3 = lax.broadcasted_iota(jnp.int32, (nb, Cc, Cc), 2)
    order = jnp.where(bi < G * H, t3 - i3, i3 - t3)
    strict = order > 0
    incl = order >= 0

    AR = jnp.concatenate([A, Rt], axis=1)
    ab = _bmm_nt(AR, Bt)
    ak = _bmm_nt(AR, Kt)
    Lab = jnp.where(strict, ab[:, :Cc], 0.0)
    Mrb = jnp.where(incl, ab[:, Cc:], 0.0)
    Lak = jnp.where(strict, ak[:, :Cc], 0.0)
    Mrk = jnp.where(incl, ak[:, Cc:], 0.0)
    lv = _bmm(jnp.concatenate([Lak, Mrk], axis=1), V)
    Z = jnp.concatenate([A, lv[:, :Cc]], axis=2)
    P = Lab
    steps = int(math.log2(Cc))
    for s in range(steps):
        if s < steps - 1:
            out = _bmm(P, jnp.concatenate([Z, P], axis=2))
            Z = Z + out[:, :, :2 * N]
            P = out[:, :, 2 * N:]
        else:
            Z = Z + _bmm(P, Z)
    mz = _bmm(Mrb, Z)
    Rh = Rt + mz[:, :, :N]
    Yh = mz[:, :, N:] + lv[:, Cc:]
    gh = _bmm_tn(Z, Bh)
    Gm = gh[:, :N]
    Hm = gh[:, N:] + _bmm_tn(V, Kh)

    S = S_ref[...].reshape(2 * H, N, N)
    for g in range(G):
        gr = G - 1 - g
        pick = lambda x: jnp.concatenate([x[g * H:(g + 1) * H], x[(G + gr) * H:(G + gr + 1) * H]], axis=0)
        y = _bmm_nt(pick(Rh), S) + pick(Yh)
        S = S * pick(Wc)[:, 0:1, :] + _bmm(S, pick(Gm)) + pick(Hm)
        yf_ref[g * Cc:(g + 1) * Cc, :] = jnp.concatenate([y[h] for h in range(H)], axis=1)
        yr_ref[gr * Cc:(gr + 1) * Cc, :] = jnp.concatenate([y[H + h] for h in range(H)], axis=1)
    S = S.reshape(2, H, N, N)
    S_ref[...] = S
    sf_ref[...] = S


def _rwkv_scan(r, k, v, kn, lw, a, k_a, s0, B, T):
    N, C = r.shape
    Cc = RWKV_CHUNK
    nc = T // Cc
    G = 2 if nc % 2 == 0 else 1
    ng = nc // G
    H, Nh = RWKV_HEADS, RWKV_HEAD
    fwd = pl.BlockSpec((G * Cc, C), lambda b, j: (b * ng + j, 0))
    rev = pl.BlockSpec((G * Cc, C), lambda b, j: (b * ng + ng - 1 - j, 0))
    fwd2 = pl.BlockSpec((None, G * Cc, C), lambda b, j: (0, b * ng + j, 0))
    rev2 = pl.BlockSpec((None, G * Cc, C), lambda b, j: (1, b * ng + ng - 1 - j, 0))
    st = pl.BlockSpec((2, None, H, Nh, Nh), lambda b, j: (0, b, 0, 0, 0))
    yf, yr, sf = pl.pallas_call(
        functools.partial(_rwkv_scan_kernel, G=G),
        out_shape=(jax.ShapeDtypeStruct((N, C), F32), jax.ShapeDtypeStruct((N, C), F32),
                   jax.ShapeDtypeStruct((2, B, H, Nh, Nh), F32)),
        grid=(B, ng),
        in_specs=[fwd, fwd, fwd, fwd, fwd2, fwd2, rev, rev, rev, rev, rev2, rev2,
                  pl.BlockSpec((1, C), lambda b, j: (0, 0)), st],
        out_specs=(fwd, rev, st),
        scratch_shapes=[pltpu.VMEM((2, H, Nh, Nh), F32)],
        compiler_params=_cparams("arbitrary", "arbitrary"),
        name="rwkv_scan",
    )(r, k, v, kn, lw, a, r, k, v, kn, lw, a, k_a, s0)
    return yf, yr, sf


def _rwkv_post_kernel(yf_ref, yr_ref, bonus_ref, g_ref, gg_ref, gb_ref, bd_ref, o_ref):
    y = yf_ref[...] + yr_ref[...]
    bd = bd_ref[...]
    inv_n = 1.0 / RWKV_HEAD
    mean = _dot_exact_rhs(y, bd) * inv_n
    yc = y - mean
    var = _dot_exact_rhs(yc * yc, bd) * inv_n
    yn = yc * lax.rsqrt(var + GN_EPS) * gg_ref[...] + gb_ref[...]
    o_ref[...] = ((yn + bonus_ref[...].astype(F32)) * g_ref[...].astype(F32)).astype(o_ref.dtype)


def _rwkv_post(yf, yr, bonus, g, gn_g, gn_b, bd):
    N, C = yf.shape
    tm = _pick(N, (1024, 512, 256))
    row = pl.BlockSpec((tm, C), lambda i: (i, 0))
    vec = pl.BlockSpec((1, C), lambda i: (0, 0))
    return pl.pallas_call(
        _rwkv_post_kernel,
        out_shape=jax.ShapeDtypeStruct((N, C), BF16),
        grid=(N // tm,),
        in_specs=[row, row, row, row, vec, vec, pl.BlockSpec((C, C), lambda i: (0, 0))],
        out_specs=row,
        compiler_params=_cparams("arbitrary"),
        name="rwkv_post",
    )(yf, yr, bonus, g, gn_g.reshape(1, C), gn_b.reshape(1, C), bd)


def _fourier_ch_kernel(p_ref, m_ref, o_ref):
    p = p_ref[...]
    mat = m_ref[...]
    G = FOURIER_GROUP_DIM
    for g in range(p.shape[1] // G):
        o_ref[:, g * G:(g + 1) * G] = jnp.dot(p[:, g * G:(g + 1) * G], mat,
                                              preferred_element_type=F32).astype(o_ref.dtype)


def _fourier_channel(proj, col, Fd, ch_mats, B, T):
    tm = _pick(T, (1024, 512, 256))
    nt = T // tm
    G = FOURIER_GROUP_DIM
    return pl.pallas_call(
        _fourier_ch_kernel,
        out_shape=jax.ShapeDtypeStruct((B, 2, T, Fd), BF16),
        grid=(B, nt, 2),
        in_specs=[pl.BlockSpec((tm, Fd), lambda b, i, s: (b * nt + i, col)),
                  pl.BlockSpec((None, G, G), lambda b, i, s: (s, 0, 0))],
        out_specs=pl.BlockSpec((None, None, tm, Fd), lambda b, i, s: (b, s, i, 0)),
        compiler_params=_cparams("arbitrary", "arbitrary", "arbitrary"),
        name="fourier_channel",
    )(proj, ch_mats)


def _dft_mats(n):
    idx = jnp.arange(n, dtype=jnp.int32)
    ph = (idx[:, None] * idx[None, :]) % n
    ang = ph.astype(F32) * (2.0 * math.pi / n)
    s = 1.0 / math.sqrt(n)
    return jnp.cos(ang) * s, jnp.sin(ang) * s


DFT_COLS = 8


def _dft1_kernel(z_ref, m1_ref, tc_ref, ts_ref, o_ref, *, T1):
    m1 = m1_ref[...]
    tc = tc_ref[...]
    ts = ts_ref[...]
    for j in range(DFT_COLS):
        zj = jnp.concatenate([z_ref[0, :, j, :], z_ref[1, :, j, :]], axis=0)
        a = jnp.dot(m1, zj, preferred_element_type=F32)
        ar = a[:T1]
        ai = a[T1:]
        c = tc[:, j:j + 1]
        s = ts[:, j:j + 1]
        o_ref[0, :, j, :] = (ar * c + ai * s).astype(o_ref.dtype)
        o_ref[1, :, j, :] = (ai * c - ar * s).astype(o_ref.dtype)


def _dft2_kernel(b_ref, m2_ref, o_ref):
    m2 = m2_ref[...]
    for j in range(DFT_COLS):
        rhs = jnp.concatenate([b_ref[0, j], b_ref[1, j]], axis=0)
        o_ref[:, j, :] = jnp.dot(m2, rhs, preferred_element_type=F32).astype(o_ref.dtype)


def _fourier_time(z, B, T, Fd):
    T2 = GRID_W
    T1 = T // T2
    if T1 % DFT_COLS or T2 % DFT_COLS:
        cos, sin = _dft_mats(T)
        mat = jnp.concatenate([cos, sin], axis=1).astype(BF16)
        zz = z.reshape(B, 2 * T, Fd)
        return jnp.concatenate([_mm(mat, zz[b], BF16) for b in range(B)], axis=0)
    c1, s1 = _dft_mats(T1)
    m1 = jnp.concatenate([jnp.concatenate([c1, s1], axis=1),
                          jnp.concatenate([-s1, c1], axis=1)], axis=0).astype(BF16)
    c2, s2 = _dft_mats(T2)
    m2 = jnp.concatenate([c2, s2], axis=1).astype(BF16)
    ph = (jnp.arange(T1, dtype=jnp.int32)[:, None] * jnp.arange(T2, dtype=jnp.int32)[None, :]) % T
    ang = ph.astype(F32) * (2.0 * math.pi / T)
    nt2 = T2 // DFT_COLS
    twc = jnp.cos(ang).reshape(T1, nt2, DFT_COLS).transpose(1, 0, 2)
    tws = jnp.sin(ang).reshape(T1, nt2, DFT_COLS).transpose(1, 0, 2)
    blk1 = pl.BlockSpec((None, 2, T1, DFT_COLS, Fd), lambda b, i: (b, 0, 0, i, 0))
    stage1 = pl.pallas_call(
        functools.partial(_dft1_kernel, T1=T1),
        out_shape=jax.ShapeDtypeStruct((B, 2, T1, T2, Fd), BF16),
        grid=(B, nt2),
        in_specs=[blk1,
                  pl.BlockSpec((2 * T1, 2 * T1), lambda b, i: (0, 0)),
                  pl.BlockSpec((None, T1, DFT_COLS), lambda b, i: (i, 0, 0)),
                  pl.BlockSpec((None, T1, DFT_COLS), lambda b, i: (i, 0, 0))],
        out_specs=blk1,
        compiler_params=_cparams("arbitrary", "arbitrary"),
        name="dft_stage1",
    )(z.reshape(B, 2, T1, T2, Fd), m1, twc, tws)
    out = pl.pallas_call(
        _dft2_kernel,
        out_shape=jax.ShapeDtypeStruct((B, T2, T1, Fd), BF16),
        grid=(B, T1 // DFT_COLS),
        in_specs=[pl.BlockSpec((None, 2, DFT_COLS, T2, Fd), lambda b, i: (b, 0, i, 0, 0)),
                  pl.BlockSpec((T2, 2 * T2), lambda b, i: (0, 0))],
        out_specs=pl.BlockSpec((None, T2, DFT_COLS, Fd), lambda b, i: (b, 0, i, 0)),
        compiler_params=_cparams("arbitrary", "arbitrary"),
        name="dft_stage2",
    )(stage1, m2)
    return out.reshape(B * T, Fd)


def _merge_kernel(fa_ref, at_ref, rw_ref, gt_ref, x_ref, g1_ref, wa_ref, wb_ref, wc_ref, wo_ref, o_ref):
    D = x_ref.shape[1]
    ya = jnp.dot(fa_ref[...], wa_ref[...], preferred_element_type=F32)
    yb = jnp.dot(at_ref[0], wb_ref[0], preferred_element_type=F32)
    for h in range(1, at_ref.shape[0]):
        yb = yb + jnp.dot(at_ref[h], wb_ref[h], preferred_element_type=F32)
    yc = jnp.dot(rw_ref[...], wc_ref[...], preferred_element_type=F32)
    gt = gt_ref[...].astype(F32)
    merged = (_sigmoid(gt[:, :D]) * ya + _sigmoid(gt[:, D:2 * D]) * yb + _sigmoid(gt[:, 2 * D:]) * yc)
    out = jnp.dot(merged.astype(BF16), wo_ref[...], preferred_element_type=F32)
    o_ref[...] = x_ref[...] + g1_ref[...] * out


def _merge(fa, attn, rw, proj, gcol, x, g1, wa, wb, wc, wo, B, T):
    N, D = x.shape
    H = attn.shape[1]
    Fd = rw.shape[1]
    tm = _pick(T, (512, 256))
    nt = T // tm
    Bm = g1.shape[0]
    full = lambda shape: pl.BlockSpec(shape, lambda b, i: (0,) * len(shape))
    return pl.pallas_call(
        _merge_kernel,
        out_shape=jax.ShapeDtypeStruct((N, D), F32),
        grid=(B, nt),
        in_specs=[pl.BlockSpec((tm, Fd), lambda b, i: (b * nt + i, 0)),
                  pl.BlockSpec((None, H, tm, LANES), lambda b, i: (b, 0, i, 0)),
                  pl.BlockSpec((tm, Fd), lambda b, i: (b * nt + i, 0)),
                  pl.BlockSpec((tm, 3 * D), lambda b, i: (b * nt + i, gcol)),
                  pl.BlockSpec((tm, D), lambda b, i: (b * nt + i, 0)),
                  pl.BlockSpec((None, 1, D), lambda b, i: (b % Bm, 0, 0)),
                  full(wa.shape), full(wb.shape), full(wc.shape), full(wo.shape)],
        out_specs=pl.BlockSpec((tm, D), lambda b, i: (b * nt + i, 0)),
        compiler_params=_cparams("arbitrary", "arbitrary"),
        name="merge_out",
    )(fa, attn, rw, proj, x, g1, wa, wb, wc, wo)


def _route(scores, sel, n_exp):
    tm = scores.shape[0]
    lane = lax.broadcasted_iota(jnp.int32, (tm, LANES), 1)
    neg = -jnp.inf
    sel = jnp.where(lane < n_exp, sel, neg)
    epg = n_exp // N_GROUPS
    assert epg == 4
    odd = (lane % 2) == 1
    p1 = jnp.where(odd, pltpu.roll(sel, 1, axis=1), pltpu.roll(sel, LANES - 1, axis=1))
    hi1, lo1 = jnp.maximum(sel, p1), jnp.minimum(sel, p1)
    upper = (lane % 4) >= 2
    hi2 = jnp.where(upper, pltpu.roll(hi1, 2, axis=1), pltpu.roll(hi1, LANES - 2, axis=1))
    lo2 = jnp.where(upper, pltpu.roll(lo1, 2, axis=1), pltpu.roll(lo1, LANES - 2, axis=1))
    gscore = jnp.maximum(hi1, hi2) + jnp.maximum(jnp.minimum(hi1, hi2), jnp.maximum(lo1, lo2))
    gscore = jnp.where(lane < n_exp, gscore, neg)
    gmax = jnp.max(gscore, axis=1, keepdims=True)
    lane_f = lane.astype(F32)
    grp = (lane // epg).astype(F32)
    big = float(LANES)
    best = jnp.min(jnp.where(gscore == gmax, grp, big), axis=1, keepdims=True)
    masked = jnp.where(grp == best, sel, neg)
    m1 = jnp.max(masked, axis=1, keepdims=True)
    i1 = jnp.min(jnp.where(masked == m1, lane_f, big), axis=1, keepdims=True)
    masked2 = jnp.where(lane_f == i1, neg, masked)
    m2 = jnp.max(masked2, axis=1, keepdims=True)
    i2 = jnp.min(jnp.where(masked2 == m2, lane_f, big), axis=1, keepdims=True)
    pick1 = lane_f == i1
    pick2 = lane_f == i2
    s1 = jnp.sum(jnp.where(pick1, scores, 0.0), axis=1, keepdims=True)
    s2 = jnp.sum(jnp.where(pick2, scores, 0.0), axis=1, keepdims=True)
    return pick1, pick2, s1 / (s1 + s2), s2 / (s1 + s2)


def _lane_to_row(col_vals):
    ones = jnp.ones((8, LANES), BF16)
    hi, lo = _split2(col_vals)
    return (_dot_nt(ones, hi) + _dot_nt(ones, lo))[0:1]


def _moe_kernel(x_ref, g_ref, sc_ref, sh_ref, g2_ref, rw_ref, rb_ref,
                wg_ref, wu_ref, wd_ref, swg_ref, swu_ref, swd_ref, o_ref,
                m_ref, xs_ref, ys_ref, pm_ref, ws_ref, oc_ref, *, n_exp):
    e = pl.program_id(1)
    tm = x_ref.shape[0]
    S = 2 * tm
    RB = MOE_ROWS

    @pl.when(e == 0)
    def _():
        x = x_ref[...]
        m = x * lax.rsqrt(jnp.mean(x * x, axis=-1, keepdims=True) + NORM_EPS) * g_ref[...]
        m = m * (1.0 + sc_ref[...]) + sh_ref[...]
        mb = m.astype(BF16)
        m_ref[...] = mb
        scores = _sigmoid(_dot_split(m, rw_ref[...]))
        pick1, pick2, w1, w2 = _route(scores, scores + rb_ref[...], n_exp)
        p1 = pick1.astype(F32)
        p2 = pick2.astype(F32)
        chosen = p1 + p2
        cnt = jnp.sum(chosen, axis=0, keepdims=True)
        li = lax.broadcasted_iota(jnp.int32, (LANES, LANES), 0)
        lj = lax.broadcasted_iota(jnp.int32, (LANES, LANES), 1)
        off = _dot_exact_rhs(jnp.broadcast_to(cnt, (8, LANES)), (li < lj).astype(F32).astype(BF16))[0:1]
        ti = lax.broadcasted_iota(jnp.int32, (tm, tm), 0)
        tj = lax.broadcasted_iota(jnp.int32, (tm, tm), 1)
        rank = jnp.dot((tj < ti).astype(F32).astype(BF16), chosen.astype(BF16),
                       preferred_element_type=F32)
        dest = off + rank
        d1 = _lane_to_row(p1 * dest)
        d2 = _lane_to_row(p2 * dest)
        w1r = _lane_to_row(jnp.broadcast_to(w1, (tm, LANES)) * p1)
        w2r = _lane_to_row(jnp.broadcast_to(w2, (tm, LANES)) * p2)
        row = lax.broadcasted_iota(jnp.int32, (S, tm), 0).astype(F32)
        is1 = row == d1
        is2 = row == d2
        perm = jnp.where(is1, 1.0, jnp.where(is2, 1.0, 0.0)).astype(BF16)
        pm_ref[...] = perm
        ws_ref[...] = jnp.sum(jnp.where(is1, w1r, jnp.where(is2, w2r, 0.0)), axis=1, keepdims=True)
        xs_ref[...] = jnp.dot(perm, mb, preferred_element_type=F32).astype(BF16)
        ys_ref[...] = jnp.zeros_like(ys_ref)
        oc_ref[0:1, :] = off
        oc_ref[1:2, :] = cnt

    @pl.when(e < n_exp)
    def _():
        lane = lax.broadcasted_iota(jnp.int32, (1, LANES), 1)
        o_e = jnp.sum(jnp.where(lane == e, oc_ref[0:1, :], 0.0)).astype(jnp.int32)
        c_e = jnp.sum(jnp.where(lane == e, oc_ref[1:2, :], 0.0)).astype(jnp.int32)
        wg = wg_ref[...].astype(BF16)
        wu = wu_ref[...].astype(BF16)
        wd = wd_ref[...].astype(BF16)
        b0 = o_e // RB
        b1 = jnp.where(c_e > 0, (o_e + c_e + RB - 1) // RB, b0)

        def body(b, carry):
            r0 = pl.multiple_of(b * RB, RB)
            xb = xs_ref[pl.ds(r0, RB), :]
            hg = jnp.dot(xb, wg, preferred_element_type=F32)
            hu = jnp.dot(xb, wu, preferred_element_type=F32)
            rows = r0 + lax.broadcasted_iota(jnp.int32, (RB, 1), 0)
            mine = (rows >= o_e) & (rows < o_e + c_e)
            wrow = jnp.where(mine, ws_ref[pl.ds(r0, RB), :], 0.0)
            hh = hg * _sigmoid(hg) * hu * wrow
            ys_ref[pl.ds(r0, RB), :] += jnp.dot(hh.astype(BF16), wd, preferred_element_type=F32)
            return carry

        lax.fori_loop(b0, b1, body, 0)

    @pl.when(e == n_exp)
    def _():
        m = m_ref[...]
        hg = jnp.dot(m, swg_ref[...], preferred_element_type=F32)
        hu = jnp.dot(m, swu_ref[...], preferred_element_type=F32)
        hh = hg * _sigmoid(hg) * hu
        y = jnp.dot(hh.astype(BF16), swd_ref[...], preferred_element_type=F32)
        y = y + _dot_tn(pm_ref[...], ys_ref[...])
        o_ref[...] = x_ref[...] + g2_ref[...] * y


def _moe(x, g, sc, sh, g2, rw_pad, rb_pad, layer, wg, wu, wd, swg, swu, swd):
    N, D = x.shape
    _, E, _, Ff = wg.shape
    Bm = sc.shape[0]
    rpm = N // Bm
    tm = min(rpm, 1024)
    assert rpm % tm == 0 and (2 * tm) % MOE_ROWS == 0
    one = pl.Buffered(1)
    mod = pl.BlockSpec((None, 1, D), lambda i, e: (i * tm // rpm, 0, 0))
    full = lambda shape: pl.BlockSpec(shape, lambda i, e: (0,) * len(shape), pipeline_mode=one)
    return pl.pallas_call(
        functools.partial(_moe_kernel, n_exp=E),
        out_shape=jax.ShapeDtypeStruct((N, D), F32),
        grid=(N // tm, E + 1),
        in_specs=[pl.BlockSpec((tm, D), lambda i, e: (i, 0), pipeline_mode=one),
                  full((1, D)), mod, mod, mod,
                  full(rw_pad.shape), full(rb_pad.shape),
                  pl.BlockSpec((None, None, D, Ff), lambda i, e: (layer, jnp.minimum(e, E - 1), 0, 0)),
                  pl.BlockSpec((None, None, D, Ff), lambda i, e: (layer, jnp.minimum(e, E - 1), 0, 0)),
                  pl.BlockSpec((None, None, Ff, D), lambda i, e: (layer, jnp.minimum(e, E - 1), 0, 0)),
                  full(swg.shape), full(swu.shape), full(swd.shape)],
        out_specs=pl.BlockSpec((tm, D), lambda i, e: (i, 0), pipeline_mode=one),
        scratch_shapes=[pltpu.VMEM((tm, D), BF16), pltpu.VMEM((2 * tm, D), BF16),
                        pltpu.VMEM((2 * tm, D), F32), pltpu.VMEM((2 * tm, tm), BF16),
                        pltpu.VMEM((2 * tm, 1), F32), pltpu.VMEM((8, LANES), F32)],
        compiler_params=pltpu.CompilerParams(dimension_semantics=("arbitrary", "arbitrary"),
                                             vmem_limit_bytes=MOE_VMEM_LIMIT_BYTES),
        name="moe",
    )(x, g.reshape(1, D), sc, sh, g2, rw_pad, rb_pad, wg, wu, wd, swg, swu, swd)


def _rope_partner(w):
    lead = w.shape[:-1]
    wr = w.reshape(lead + (2, 2, ROPE_PAIRS))
    return jnp.stack([-wr[..., 1, :], wr[..., 0, :]], axis=-2).reshape(w.shape)


def _rope_tables(T):
    rows = T // GRID_W
    row = jnp.repeat(jnp.arange(rows), GRID_W)
    col = jnp.tile(jnp.arange(GRID_W), rows)
    inv = ROPE_BASE ** (-jnp.arange(ROPE_PAIRS, dtype=F32) / ROPE_PAIRS)
    ang = jnp.stack([row, col], axis=-1).astype(F32)[..., None] * inv
    cos = jnp.broadcast_to(jnp.cos(ang)[:, :, None, :], (T, 2, 2, ROPE_PAIRS)).reshape(T, MLA_ROPE)
    sin = jnp.broadcast_to(jnp.sin(ang)[:, :, None, :], (T, 2, 2, ROPE_PAIRS)).reshape(T, MLA_ROPE)
    pad = LANES - MLA_QK
    z_n, z_p = jnp.zeros((T, MLA_NOPE), F32), jnp.zeros((T, pad), F32)
    cq = jnp.concatenate([jnp.ones((T, MLA_NOPE), F32), cos, z_p], axis=1)
    sq = jnp.concatenate([z_n, sin, z_p], axis=1)
    ck = jnp.concatenate([z_n, cos, z_p], axis=1)
    return cq, sq, ck, sq


def _identity_tables(T):
    pad = LANES - MLA_QK
    z_n, z_p = jnp.zeros((T, MLA_NOPE), F32), jnp.zeros((T, pad), F32)
    one_r = jnp.ones((T, MLA_ROPE), F32)
    cq = jnp.concatenate([jnp.ones((T, MLA_NOPE), F32), one_r, z_p], axis=1)
    ck = jnp.concatenate([z_n, one_r, z_p], axis=1)
    zero = jnp.zeros((T, LANES), F32)
    return cq, zero, ck, zero


def kernel(x, c, ctx, c_ctx, w_ada, b_ada, norm1_g, norm2_g, w_in, q_norm_g, kv_norm_g, w_uq, w_ukv, rwkv_mu, rwkv_w0, rwkv_w_up, rwkv_a0, rwkv_a_up, rwkv_g_up, rwkv_k_k, rwkv_k_a, rwkv_r_k, rwkv_gn_g, rwkv_gn_b, w_proj_a, w_proj_b, w_proj_c, w_out, router_w, router_b, exp_w_gate, exp_w_up, exp_w_down, sh_w_gate, sh_w_up, sh_w_down, final_norm_g):
    B, T, D = x.shape
    Tc = ctx.shape[1]
    L = w_ada.shape[0]
    H = MLA_HEADS
    QL, KVL = q_norm_g.shape[1], kv_norm_g.shape[1]
    Fd = w_proj_a.shape[1]
    C = RWKV_DIM
    E = router_w.shape[1]
    N, Nc = B * T, B * Tc

    sizes = (Fd, QL, KVL, MLA_ROPE, 3 * C + DECAY_LORA + AAA_LORA + GATE_LORA, D, D, D)
    offs = np.concatenate([[0], np.cumsum(sizes)])

    lat_tabs = _rope_tables(T)
    ctx_tabs = _identity_tables(Tc)
    cc, cs = _dft_mats(FOURIER_GROUP_DIM)
    ch_mats = jnp.stack([cc, -cs]).astype(BF16)
    hid = jnp.arange(C) // RWKV_HEAD
    bd = (hid[:, None] == hid[None, :]).astype(BF16)
    rw_pad = jnp.zeros((D, LANES), F32).at[:, :E].set(router_w)
    rb_pad = jnp.zeros((1, LANES), F32).at[0, :E].set(router_b)

    R = -(-(B + 1) // 8) * 8
    c_rows = jnp.zeros((R, D), F32).at[:B].set(c).at[B].set(c_ctx)
    mod = _ada_mod(c_rows, w_ada, b_ada)

    x = x.reshape(N, D)
    ctx = ctx.reshape(Nc, D)
    zero_state = jnp.zeros((2, B, RWKV_HEADS, RWKV_HEAD, RWKV_HEAD), F32)

    for l in range(L):
        last = l == L - 1
        ml = mod[l, :B].reshape(B, 1, 6, D)
        mc = mod[l, B:B + 1].reshape(1, 1, 6, D)
        sh1, sc1, g1, sh2, sc2, g2 = [ml[:, :, j] for j in range(6)]
        csh1, csc1, cg1, csh2, csc2, cg2 = [mc[:, :, j] for j in range(6)]

        wi = w_in[l]
        w_kr = wi[:, offs[3]:offs[4]]
        zl = jnp.zeros((D, MLA_NOPE), F32)
        zr = jnp.zeros((D, LANES - MLA_QK), F32)
        segs = [wi[:, offs[5]:offs[8]],
                wi[:, offs[0]:offs[1]],
                wi[:, offs[4]:offs[5]],
                jnp.concatenate([wi[:, offs[1]:offs[3]], zl, w_kr, zr, zl, _rope_partner(w_kr), zr], axis=1)]
        seg_off, n_all = _column_layout([s.shape[1] for s in segs])
        pieces, cur = [], 0
        for s, o in zip(segs, seg_off):
            if o > cur:
                pieces.append(jnp.zeros((D, o - cur), F32))
            pieces.append(s)
            cur = o + s.shape[1]
        if n_all > cur:
            pieces.append(jnp.zeros((D, n_all - cur), F32))
        w_all = jnp.concatenate(pieces, axis=1).astype(BF16)
        gt_col, pf_col, rw_col, qkv_col = [o // s.shape[1] for s, o in zip(segs, seg_off)]
        uq = w_uq[l].reshape(QL, H, MLA_QK).transpose(1, 0, 2)
        padq = jnp.zeros((H, QL, LANES - MLA_QK), F32)
        wqa = jnp.concatenate([uq, padq], axis=2).astype(BF16)
        wqb = jnp.concatenate([jnp.zeros((H, QL, MLA_NOPE), F32), _rope_partner(uq[..., MLA_NOPE:]), padq],
                              axis=2).astype(BF16)
        ukv = w_ukv[l].reshape(KVL, H, MLA_NOPE + MLA_V).transpose(1, 0, 2)
        wk = jnp.concatenate([ukv[..., :MLA_NOPE], jnp.zeros((H, KVL, LANES - MLA_NOPE), F32)], axis=2).astype(BF16)
        zv = jnp.zeros((H, KVL, LANES - MLA_V), F32)
        even = (jnp.arange(H) % 2 == 0)[:, None, None]
        wv = jnp.where(even, jnp.concatenate([ukv[..., MLA_NOPE:], zv], axis=2),
                       jnp.concatenate([zv, ukv[..., MLA_NOPE:]], axis=2)).astype(BF16)
        wb = w_proj_b[l].reshape(H // 2, 2 * MLA_V, D).astype(BF16)
        wa = w_proj_a[l].astype(BF16)
        wc = w_proj_c[l].astype(BF16)
        wo = w_out[l].astype(BF16)
        zd = jnp.zeros((2, DECAY_LORA, C), F32)
        prm = dict(
            mu=rwkv_mu[l].reshape(1, -1), w0=rwkv_w0[l], a0=rwkv_a0[l],
            wup=jnp.concatenate([rwkv_w_up[l], zd], axis=1).astype(BF16),
            aup=jnp.concatenate([zd, rwkv_a_up[l]], axis=1).astype(BF16),
            gup=rwkv_g_up[l].astype(BF16), k_k=rwkv_k_k[l].reshape(1, C), k_a=rwkv_k_a[l].reshape(1, C),
            r_k=rwkv_r_k[l].reshape(1, C), bd=bd)

        def mixers(xx, sc, sh, tabs, Tx):
            proj = _in_proj(xx, norm1_g[l], sc, sh, w_all)
            q, k, v = _qkv_project(proj, qkv_col, q_norm_g[l], kv_norm_g[l], wqa, wqb, wk, wv, tabs, B, Tx)
            rp = _rwkv_prep(proj, rw_col, prm, B, Tx)
            return proj, q, k, v, rp

        proj_l, q_l, k_l, v_l, rp_l = mixers(x, sc1, sh1, lat_tabs, T)
        proj_c, q_c, k_c, v_c, rp_c = mixers(ctx, csc1, csh1, ctx_tabs, Tc)

        attn_l = _attention(q_l, [(k_l, v_l), (k_c, v_c)])

        yf_c, yr_c, s_ctx = _rwkv_scan(*rp_c[:6], prm["k_a"], zero_state, B, Tc)
        yf_l, yr_l, _ = _rwkv_scan(*rp_l[:6], prm["k_a"], s_ctx, B, T)
        rwo_l = _rwkv_post(yf_l, yr_l, rp_l[7], rp_l[6], rwkv_gn_g[l], rwkv_gn_b[l], bd)

        fa_l = _fourier_time(_fourier_channel(proj_l, pf_col, Fd, ch_mats, B, T), B, T, Fd)

        x = _merge(fa_l, attn_l, rwo_l, proj_l, gt_col, x, g1, wa, wb, wc, wo, B, T)
        moe_w = (rw_pad, rb_pad, l, exp_w_gate, exp_w_up, exp_w_down,
                 sh_w_gate[l].astype(BF16), sh_w_up[l].astype(BF16), sh_w_down[l].astype(BF16))
        x = _moe(x, norm2_g[l], sc2, sh2, g2, *moe_w)

        if not last:
            attn_c = _attention(q_c, [(k_c, v_c)])
            rwo_c = _rwkv_post(yf_c, yr_c, rp_c[7], rp_c[6], rwkv_gn_g[l], rwkv_gn_b[l], bd)
            fa_c = _fourier_time(_fourier_channel(proj_c, pf_col, Fd, ch_mats, B, Tc), B, Tc, Fd)
            ctx = _merge(fa_c, attn_c, rwo_c, proj_c, gt_col, ctx, cg1, wa, wb, wc, wo, B, Tc)
            ctx = _moe(ctx, norm2_g[l], csc2, csh2, cg2, *moe_w)

    ones = jnp.ones((1, 1, D), F32)
    out = _norm(x, final_norm_g, ones, ones, F32, modulate=False)
    return out.reshape(B, T, D)
```

```python
import functools
import math

import numpy as np
import jax
import jax.numpy as jnp
from jax import lax
from jax.experimental import pallas as pl
from jax.experimental.pallas import tpu as pltpu

F32 = jnp.float32
BF16 = jnp.bfloat16

GRID_W = 64
FOURIER_GROUP_DIM = 128
MLA_HEADS = 8
MLA_NOPE = 64
MLA_ROPE = 32
MLA_QK = MLA_NOPE + MLA_ROPE
MLA_V = 64
MLA_SCALE = MLA_QK ** -0.5
QK_PRESCALE = MLA_SCALE * math.log2(math.e)
ROPE_PAIRS = MLA_ROPE // 4
ROPE_BASE = 10000.0
RWKV_HEADS = 8
RWKV_HEAD = 64
RWKV_DIM = RWKV_HEADS * RWKV_HEAD
DECAY_LORA = 64
AAA_LORA = 64
GATE_LORA = 128
GN_EPS = 64e-5
N_GROUPS = 4
NORM_EPS = 1e-6

LANES = 128
VMEM_LIMIT_BYTES = 48 * 1024 * 1024
RWKV_CHUNK = 64
MOE_ROWS = 128
MOE_VMEM_LIMIT_BYTES = 56 * 1024 * 1024


def _cparams(*sem):
    return pltpu.CompilerParams(dimension_semantics=sem, vmem_limit_bytes=VMEM_LIMIT_BYTES)


def _sigmoid(z):
    return 1.0 / (1.0 + jnp.exp(-z))


def _bdot(a, b):
    return jnp.dot(a.astype(BF16), b.astype(BF16), preferred_element_type=F32)


def _dot_nt(a, b):
    return lax.dot_general(a.astype(BF16), b.astype(BF16), (((1,), (1,)), ((), ())),
                           preferred_element_type=F32)


def _dot_tn(a, b):
    return lax.dot_general(a.astype(BF16), b.astype(BF16), (((0,), (0,)), ((), ())),
                           preferred_element_type=F32)


def _split2(x):
    hi = x.astype(BF16)
    lo = (x - hi.astype(F32)).astype(BF16)
    return hi, lo


def _dot_exact_rhs(x, ones_bf16):
    hi, lo = _split2(x)
    return (jnp.dot(hi, ones_bf16, preferred_element_type=F32)
            + jnp.dot(lo, ones_bf16, preferred_element_type=F32))


def _dot_split(a, b):
    ah, al = _split2(a)
    bh, bl = _split2(b)
    return (jnp.dot(ah, bh, preferred_element_type=F32)
            + jnp.dot(ah, bl, preferred_element_type=F32)
            + jnp.dot(al, bh, preferred_element_type=F32))


def _ada_kernel(c_ref, w_ref, b_ref, o_ref):
    c = c_ref[...]
    s = c * _sigmoid(c)
    o_ref[...] = _dot_split(s, w_ref[...]) + b_ref[...]


def _ada_mod(c_rows, w_ada, b_ada):
    L, D, D6 = w_ada.shape
    R = c_rows.shape[0]
    tn = 1536 if D6 % 1536 == 0 else D6
    return pl.pallas_call(
        _ada_kernel,
        out_shape=jax.ShapeDtypeStruct((L, R, D6), F32),
        grid=(L, D6 // tn),
        in_specs=[pl.BlockSpec((R, D), lambda l, j: (0, 0)),
                  pl.BlockSpec((None, D, tn), lambda l, j: (l, 0, j)),
                  pl.BlockSpec((None, 1, tn), lambda l, j: (l, 0, j))],
        out_specs=pl.BlockSpec((None, R, tn), lambda l, j: (l, 0, j)),
        compiler_params=_cparams("arbitrary", "arbitrary"),
        name="ada_mod",
    )(c_rows, w_ada, b_ada.reshape(L, 1, D6))


def _norm_kernel(x_ref, g_ref, sc_ref, sh_ref, o_ref, *, modulate):
    x = x_ref[...].astype(F32)
    y = x * lax.rsqrt(jnp.mean(x * x, axis=-1, keepdims=True) + NORM_EPS) * g_ref[...]
    if modulate:
        y = y * (1.0 + sc_ref[...]) + sh_ref[...]
    o_ref[...] = y.astype(o_ref.dtype)


def _norm(x, g, sc, sh, out_dtype, modulate=True):
    N, D = x.shape
    Bm = sc.shape[0]
    rpm = N // Bm
    tm = min(rpm, 1024)
    assert rpm % tm == 0
    return pl.pallas_call(
        functools.partial(_norm_kernel, modulate=modulate),
        out_shape=jax.ShapeDtypeStruct((N, D), out_dtype),
        grid=(N // tm,),
        in_specs=[pl.BlockSpec((tm, D), lambda i: (i, 0)),
                  pl.BlockSpec((1, D), lambda i: (0, 0)),
                  pl.BlockSpec((None, 1, D), lambda i: (i * tm // rpm, 0, 0)),
                  pl.BlockSpec((None, 1, D), lambda i: (i * tm // rpm, 0, 0))],
        out_specs=pl.BlockSpec((tm, D), lambda i: (i, 0)),
        compiler_params=_cparams("arbitrary"),
        name="rmsnorm_mod",
    )(x, g.reshape(1, D), sc, sh)


def _in_proj_kernel(x_ref, g_ref, sc_ref, sh_ref, w_ref, o_ref, h_ref):
    @pl.when(pl.program_id(1) == 0)
    def _():
        x = x_ref[...]
        y = x * lax.rsqrt(jnp.mean(x * x, axis=-1, keepdims=True) + NORM_EPS) * g_ref[...]
        h_ref[...] = (y * (1.0 + sc_ref[...]) + sh_ref[...]).astype(BF16)

    o_ref[...] = jnp.dot(h_ref[...], w_ref[...], preferred_element_type=F32).astype(o_ref.dtype)


def _in_proj(x, g, sc, sh, w_all):
    N, D = x.shape
    Nout = w_all.shape[1]
    Bm = sc.shape[0]
    rpm = N // Bm
    tm = min(rpm, 512)
    assert rpm % tm == 0
    tn = Nout
    mod = pl.BlockSpec((None, 1, D), lambda i, j: (i * tm // rpm, 0, 0))
    return pl.pallas_call(
        _in_proj_kernel,
        out_shape=jax.ShapeDtypeStruct((N, Nout), BF16),
        grid=(N // tm, Nout // tn),
        in_specs=[pl.BlockSpec((tm, D), lambda i, j: (i, 0)),
                  pl.BlockSpec((1, D), lambda i, j: (0, 0)), mod, mod,
                  pl.BlockSpec((D, tn), lambda i, j: (0, j), pipeline_mode=pl.Buffered(1))],
        out_specs=pl.BlockSpec((tm, tn), lambda i, j: (i, j)),
        scratch_shapes=[pltpu.VMEM((tm, D), BF16)],
        compiler_params=_cparams("arbitrary", "arbitrary"),
        name="in_proj",
    )(x, g.reshape(1, D), sc, sh, w_all)


def _column_layout(widths):
    offs, cur = [], 0
    for w in widths:
        cur = -(-cur // w) * w
        offs.append(cur)
        cur += w
    return offs, -(-cur // LANES) * LANES


def _mm_kernel(a_ref, b_ref, o_ref):
    o_ref[...] = jnp.dot(a_ref[...], b_ref[...], preferred_element_type=F32).astype(o_ref.dtype)


def _mm_acc_kernel(a_ref, b_ref, o_ref, acc_ref, *, nk):
    k = pl.program_id(2)

    @pl.when(k == 0)
    def _():
        acc_ref[...] = jnp.zeros_like(acc_ref)

    acc_ref[...] += jnp.dot(a_ref[...], b_ref[...], preferred_element_type=F32)

    @pl.when(k == nk - 1)
    def _():
        o_ref[...] = acc_ref[...].astype(o_ref.dtype)


def _pick(n, prefs):
    for p in prefs:
        if n % p == 0:
            return p
    return n


def _mm(a, w, out_dtype, tn=None, tk=None):
    M, K = a.shape
    Nn = w.shape[1]
    tm = _pick(M, (1024, 512, 256))
    tn = tn or _pick(Nn, (1024, 896, 512))
    tk = tk or K
    if tk == K:
        return pl.pallas_call(
            _mm_kernel,
            out_shape=jax.ShapeDtypeStruct((M, Nn), out_dtype),
            grid=(M // tm, Nn // tn),
            in_specs=[pl.BlockSpec((tm, K), lambda i, j: (i, 0)),
                      pl.BlockSpec((K, tn), lambda i, j: (0, j))],
            out_specs=pl.BlockSpec((tm, tn), lambda i, j: (i, j)),
            compiler_params=_cparams("arbitrary", "arbitrary"),
            name="matmul",
        )(a, w)
    nk = K // tk
    return pl.pallas_call(
        functools.partial(_mm_acc_kernel, nk=nk),
        out_shape=jax.ShapeDtypeStruct((M, Nn), out_dtype),
        grid=(M // tm, Nn // tn, nk),
        in_specs=[pl.BlockSpec((tm, tk), lambda i, j, k: (i, k)),
                  pl.BlockSpec((tk, tn), lambda i, j, k: (k, j))],
        out_specs=pl.BlockSpec((tm, tn), lambda i, j, k: (i, j)),
        scratch_shapes=[pltpu.VMEM((tm, tn), F32)],
        compiler_params=_cparams("arbitrary", "arbitrary", "arbitrary"),
        name="matmul_kacc",
    )(a, w)


def _qkv_kernel(p_ref, gq_ref, gkv_ref, wqa_ref, wqb_ref, wk_ref, wv_ref,
                cq_ref, sq_ref, ck_ref, sk_ref, vone_ref, q_ref, k_ref, v_ref, *, ql, kvl):
    p = p_ref[...].astype(F32)
    pq = p[:, :ql]
    pkv = p[:, ql:ql + kvl]
    kra = p[:, ql + kvl:ql + kvl + LANES]
    krb = p[:, ql + kvl + LANES:ql + kvl + 2 * LANES]
    qn = (pq * lax.rsqrt(jnp.mean(pq * pq, axis=-1, keepdims=True) + NORM_EPS) * gq_ref[...]).astype(BF16)
    kvn = (pkv * lax.rsqrt(jnp.mean(pkv * pkv, axis=-1, keepdims=True) + NORM_EPS) * gkv_ref[...]).astype(BF16)
    kr = kra * ck_ref[...] + krb * sk_ref[...]
    cq = cq_ref[...] * QK_PRESCALE
    sq = sq_ref[...] * QK_PRESCALE
    for h in range(q_ref.shape[0]):
        qa = jnp.dot(qn, wqa_ref[h], preferred_element_type=F32)
        qb = jnp.dot(qn, wqb_ref[h], preferred_element_type=F32)
        q_ref[h] = (qa * cq + qb * sq).astype(q_ref.dtype)
        k_ref[h] = (jnp.dot(kvn, wk_ref[h], preferred_element_type=F32) + kr).astype(k_ref.dtype)
        v_ref[h] = (jnp.dot(kvn, wv_ref[h], preferred_element_type=F32) + vone_ref[h]).astype(v_ref.dtype)


def _ones_lane(h):
    return LANES - 1 if h % 2 == 0 else 0


def _value_ones(H):
    return jnp.stack([jnp.zeros((1, LANES), F32).at[0, _ones_lane(h)].set(1.0) for h in range(H)])


def _qkv_project(proj, col, gq, gkv, wqa, wqb, wk, wv, tabs, B, T):
    N = proj.shape[0]
    H = wqa.shape[0]
    ql, kvl = wqa.shape[1], wk.shape[1]
    W = ql + kvl + 2 * LANES
    tm = _pick(T, (512, 256))
    nt = T // tm
    cq, sq, ck, sk = tabs
    full = lambda shape: pl.BlockSpec(shape, lambda i: (0,) * len(shape))
    tab_spec = pl.BlockSpec((tm, LANES), lambda i: (i % nt, 0))
    out_spec = pl.BlockSpec((None, H, tm, LANES), lambda i: (i // nt, 0, i % nt, 0))
    out_sds = jax.ShapeDtypeStruct((B, H, T, LANES), BF16)
    return pl.pallas_call(
        functools.partial(_qkv_kernel, ql=ql, kvl=kvl),
        out_shape=(out_sds, out_sds, out_sds),
        grid=(N // tm,),
        in_specs=[pl.BlockSpec((tm, W), lambda i: (i, col)),
                  full((1, ql)), full((1, kvl)),
                  full(wqa.shape), full(wqb.shape), full(wk.shape), full(wv.shape),
                  tab_spec, tab_spec, tab_spec, tab_spec, full((H, 1, LANES))],
        out_specs=(out_spec, out_spec, out_spec),
        compiler_params=_cparams("arbitrary"),
        name="mla_qkv",
    )(proj, gq.reshape(1, ql), gkv.reshape(1, kvl), wqa, wqb, wk, wv, cq, sq, ck, sk, _value_ones(H))


def _attn_kernel(q_ref, *refs, tiles):
    o_ref = refs[-1]
    out = None
    for hh in range(q_ref.shape[0]):
        q = q_ref[hh]
        m = acc = None
        for src, start, size in tiles:
            k = refs[2 * src][hh, start:start + size, :]
            v = refs[2 * src + 1][hh, start:start + size, :]
            s = _dot_nt(q, k)
            mt = jnp.max(s, axis=-1, keepdims=True)
            if m is None:
                m_new = mt
                acc = jnp.dot(jnp.exp2(s - m_new).astype(BF16), v, preferred_element_type=F32)
            else:
                m_new = jnp.maximum(m, mt)
                acc = (jnp.exp2(m - m_new) * acc
                       + jnp.dot(jnp.exp2(s - m_new).astype(BF16), v, preferred_element_type=F32))
            m = m_new
        ol = _ones_lane(hh)
        lane = lax.broadcasted_iota(jnp.int32, acc.shape, 1)
        o = jnp.where(lane == ol, 0.0, acc / acc[:, ol:ol + 1])
        out = o if out is None else out + o
    o_ref[...] = out.astype(o_ref.dtype)


def _attention(q, kvs):
    B, H, T, dq = q.shape
    HP = 2
    tq = _pick(T, (1024, 512, 256))
    tiles = []
    for src, (k, _) in enumerate(kvs):
        Tk = k.shape[2]
        tk = _pick(Tk, (1024, 512, 256))
        tiles += [(src, st, tk) for st in range(0, Tk, tk)]
    in_specs = [pl.BlockSpec((None, HP, tq, dq), lambda b, h, i: (b, h, i, 0))]
    args = [q]
    for k, v in kvs:
        Tk = k.shape[2]
        in_specs += [pl.BlockSpec((None, HP, Tk, dq), lambda b, h, i: (b, h, 0, 0)),
                     pl.BlockSpec((None, HP, Tk, LANES), lambda b, h, i: (b, h, 0, 0))]
        args += [k, v]
    return pl.pallas_call(
        functools.partial(_attn_kernel, tiles=tuple(tiles)),
        out_shape=jax.ShapeDtypeStruct((B, H // HP, T, LANES), BF16),
        grid=(B, H // HP, T // tq),
        in_specs=in_specs,
        out_specs=pl.BlockSpec((None, None, tq, LANES), lambda b, h, i: (b, h, i, 0)),
        compiler_params=_cparams("arbitrary", "arbitrary", "arbitrary"),
        name="attention",
    )(*args)


def _softplus(u):
    return jnp.maximum(u, 0.0) + jnp.log(1.0 + jnp.exp(-jnp.abs(u)))


def _rwkv_prep_kernel(p_ref, pp_ref, pn_ref, mu_ref, w0_ref, wup_ref, a0_ref, aup_ref, gup_ref,
                      kk_ref, ka_ref, rk_ref, bd_ref,
                      r_o, k_o, v_o, kn_o, lw_o, a_o, g_o, bonus_o, *, tps):
    i = pl.program_id(0)
    p = p_ref[...].astype(F32)
    tt = p.shape[0]
    C = RWKV_DIM
    halo = pp_ref.shape[0]
    prev_row = jnp.where(i % tps == 0, 0.0, pp_ref[...].astype(F32)[halo - 1:halo, :])
    next_row = jnp.where(i % tps == tps - 1, 0.0, pn_ref[...].astype(F32)[0:1, :])
    rows = lax.broadcasted_iota(jnp.int32, (tt, 1), 0)
    pprev = jnp.where(rows == 0, prev_row, pltpu.roll(p, 1, axis=0))
    pnext = jnp.where(rows == tt - 1, next_row, pltpu.roll(p, tt - 1, axis=0))
    xs = p + (0.5 * (pprev + pnext) - p) * mu_ref[...]
    r = xs[:, :C]
    k = xs[:, C:2 * C]
    v = xs[:, 2 * C:3 * C]
    xwa = xs[:, 3 * C:3 * C + DECAY_LORA + AAA_LORA]
    xg = xs[:, 3 * C + DECAY_LORA + AAA_LORA:]
    bd = bd_ref[...]
    g_o[...] = _bdot(_sigmoid(xg), gup_ref[...]).astype(g_o.dtype)
    kn = k * kk_ref[...]
    kn = kn * lax.rsqrt(_dot_exact_rhs(kn * kn, bd) + 1e-12)
    th = jnp.tanh(xwa)
    bonus = jnp.zeros_like(r)
    for d in range(2):
        w = -_softplus(-(w0_ref[d:d + 1, :] + _bdot(th, wup_ref[d]))) - 0.5
        lw_o[d] = -jnp.exp(w)
        a = _sigmoid(a0_ref[d:d + 1, :] + _bdot(xwa, aup_ref[d]))
        a_o[d] = a.astype(a_o.dtype)
        kd = k * (1.0 + (a - 1.0) * ka_ref[...])
        bonus = bonus + _dot_exact_rhs(r * kd * rk_ref[...], bd) * v
    r_o[...] = r.astype(r_o.dtype)
    k_o[...] = k.astype(k_o.dtype)
    v_o[...] = v.astype(v_o.dtype)
    kn_o[...] = kn.astype(kn_o.dtype)
    bonus_o[...] = bonus.astype(bonus_o.dtype)


def _rwkv_prep(proj, col, prm, B, T):
    N = proj.shape[0]
    W = prm["mu"].shape[1]
    C = RWKV_DIM
    tt = _pick(T, (256,))
    tps = T // tt
    halo = 16
    nh = N // halo
    full = lambda shape: pl.BlockSpec(shape, lambda i: (0,) * len(shape))
    o_spec = pl.BlockSpec((tt, C), lambda i: (i, 0))
    o2_spec = pl.BlockSpec((2, tt, C), lambda i: (0, i, 0))
    sds = jax.ShapeDtypeStruct((N, C), BF16)
    lw_sds = jax.ShapeDtypeStruct((2, N, C), F32)
    a_sds = jax.ShapeDtypeStruct((2, N, C), BF16)
    return pl.pallas_call(
        functools.partial(_rwkv_prep_kernel, tps=tps),
        out_shape=(sds, sds, sds, sds, lw_sds, a_sds, sds, sds),
        grid=(N // tt,),
        in_specs=[pl.BlockSpec((tt, W), lambda i: (i, col)),
                  pl.BlockSpec((halo, W), lambda i: (jnp.maximum(i * (tt // halo) - 1, 0), col)),
                  pl.BlockSpec((halo, W), lambda i: (jnp.minimum((i + 1) * (tt // halo), nh - 1), col)),
                  full((1, W)), full((2, C)), full((2, LANES, C)), full((2, C)), full((2, LANES, C)),
                  full((GATE_LORA, C)), full((1, C)), full((1, C)), full((1, C)), full((C, C))],
        out_specs=(o_spec, o_spec, o_spec, o_spec, o2_spec, o2_spec, o_spec, o_spec),
        compiler_params=_cparams("arbitrary"),
        name="rwkv_prep",
    )(proj, proj, proj, prm["mu"], prm["w0"], prm["wup"], prm["a0"], prm["aup"], prm["gup"],
      prm["k_k"], prm["k_a"], prm["r_k"], prm["bd"])


def _bmm(a, b):
    return lax.dot_general(a.astype(BF16), b.astype(BF16), (((2,), (1,)), ((0,), (0,))),
                           preferred_element_type=F32)


def _bmm_nt(a, b):
    return lax.dot_general(a.astype(BF16), b.astype(BF16), (((2,), (2,)), ((0,), (0,))),
                           preferred_element_type=F32)


def _bmm_tn(a, b):
    return lax.dot_general(a.astype(BF16), b.astype(BF16), (((1,), (1,)), ((0,), (0,))),
                           preferred_element_type=F32)


def _rwkv_scan_kernel(rf_ref, kf_ref, vf_ref, knf_ref, lwf_ref, af_ref,
                      rr_ref, kr_ref, vr_ref, knr_ref, lwr_ref, ar_ref, ka_ref, s0_ref,
                      yf_ref, yr_ref, sf_ref, S_ref, *, G):
    j = pl.program_id(1)

    @pl.when(j == 0)
    def _():
        S_ref[...] = s0_ref[...]

    Cc, N, H = RWKV_CHUNK, RWKV_HEAD, RWKV_HEADS
    R = G * Cc
    ti = lax.broadcasted_iota(jnp.int32, (R, R), 0)
    ii = lax.broadcasted_iota(jnp.int32, (R, R), 1)
    same = (ti // Cc) == (ii // Cc)

    def heads(x):
        return jnp.stack([x[g * Cc:(g + 1) * Cc, h * N:(h + 1) * N] for g in range(G) for h in range(H)])

    per_dir = []
    for d, refs in enumerate(((rf_ref, kf_ref, vf_ref, knf_ref, lwf_ref, af_ref),
                              (rr_ref, kr_ref, vr_ref, knr_ref, lwr_ref, ar_ref))):
        r, k, v, kn, lw, a = [x[...].astype(F32) for x in refs]
        before = (ii <= ti) if d == 0 else (ii >= ti)
        m_incl = (same & before).astype(F32).astype(BF16)
        h1 = lw.astype(BF16)
        r1 = lw - h1.astype(F32)
        h2 = r1.astype(BF16)
        h3 = (r1 - h2.astype(F32)).astype(BF16)
        cum = (jnp.dot(m_incl, h1, preferred_element_type=F32)
               + jnp.dot(m_incl, h2, preferred_element_type=F32)
               + jnp.dot(m_incl, h3, preferred_element_type=F32))
        last = [(g + 1) * Cc - 1 if d == 0 else g * Cc for g in range(G)]
        tot = jnp.concatenate([jnp.broadcast_to(cum[t:t + 1, :], (Cc, cum.shape[1])) for t in last],
                              axis=0)
        e_neg = jnp.exp(-cum)
        e_rem = jnp.exp(tot - cum)
        b = kn * a
        kd = k * (1.0 + (a - 1.0) * ka_ref[...])
        per_dir.append([heads(x) for x in (-kn * jnp.exp(cum - lw), r * jnp.exp(cum), b * e_neg, kd * e_neg,
                                           b * e_rem, kd * e_rem, v, jnp.exp(tot))])

    A, Rt, Bt, Kt, Bh, Kh, V, Wc = [jnp.concatenate([per_dir[0][i], per_dir[1][i]], axis=0) for i in range(8)]
    nb = 2 * G * H
    bi = lax.broadcasted_iota(jnp.int32, (nb, Cc, Cc), 0)
    t3 = lax.broadcasted_iota(jnp.int32, (nb, Cc, Cc), 1)
    i3 = lax.broadcasted_iota(jnp.int32, (nb, Cc, Cc), 2)
    order = jnp.where(bi < G * H, t3 - i3, i3 - t3)
    strict = order > 0
    incl = order >= 0

    AR = jnp.concatenate([A, Rt], axis=1)
    ab = _bmm_nt(AR, Bt)
    ak = _bmm_nt(AR, Kt)
    Lab = jnp.where(strict, ab[:, :Cc], 0.0)
    Mrb = jnp.where(incl, ab[:, Cc:], 0.0)
    Lak = jnp.where(strict, ak[:, :Cc], 0.0)
    Mrk = jnp.where(incl, ak[:, Cc:], 0.0)
    lv = _bmm(jnp.concatenate([Lak, Mrk], axis=1), V)
    Z = jnp.concatenate([A, lv[:, :Cc]], axis=2)
    P = Lab
    steps = int(math.log2(Cc))
    for s in range(steps):
        if s < steps - 1:
            out = _bmm(P, jnp.concatenate([Z, P], axis=2))
            Z = Z + out[:, :, :2 * N]
            P = out[:, :, 2 * N:]
        else:
            Z = Z + _bmm(P, Z)
    mz = _bmm(Mrb, Z)
    Rh = Rt + mz[:, :, :N]
    Yh = mz[:, :, N:] + lv[:, Cc:]
    gh = _bmm_tn(Z, Bh)
    Gm = gh[:, :N]
    Hm = gh[:, N:] + _bmm_tn(V, Kh)

    S = S_ref[...].reshape(2 * H, N, N)
    for g in range(G):
        gr = G - 1 - g
        pick = lambda x: jnp.concatenate([x[g * H:(g + 1) * H], x[(G + gr) * H:(G + gr + 1) * H]], axis=0)
        y = _bmm_nt(pick(Rh), S) + pick(Yh)
        S = S * pick(Wc)[:, 0:1, :] + _bmm(S, pick(Gm)) + pick(Hm)
        yf_ref[g * Cc:(g + 1) * Cc, :] = jnp.concatenate([y[h] for h in range(H)], axis=1)
        yr_ref[gr * Cc:(gr + 1) * Cc, :] = jnp.concatenate([y[H + h] for h in range(H)], axis=1)
    S = S.reshape(2, H, N, N)
    S_ref[...] = S
    sf_ref[...] = S


def _rwkv_scan(r, k, v, kn, lw, a, k_a, s0, B, T):
    N, C = r.shape
    Cc = RWKV_CHUNK
    nc = T // Cc
    G = 2 if nc % 2 == 0 else 1
    ng = nc // G
    H, Nh = RWKV_HEADS, RWKV_HEAD
    fwd = pl.BlockSpec((G * Cc, C), lambda b, j: (b * ng + j, 0))
    rev = pl.BlockSpec((G * Cc, C), lambda b, j: (b * ng + ng - 1 - j, 0))
    fwd2 = pl.BlockSpec((None, G * Cc, C), lambda b, j: (0, b * ng + j, 0))
    rev2 = pl.BlockSpec((None, G * Cc, C), lambda b, j: (1, b * ng + ng - 1 - j, 0))
    st = pl.BlockSpec((2, None, H, Nh, Nh), lambda b, j: (0, b, 0, 0, 0))
    yf, yr, sf = pl.pallas_call(
        functools.partial(_rwkv_scan_kernel, G=G),
        out_shape=(jax.ShapeDtypeStruct((N, C), F32), jax.ShapeDtypeStruct((N, C), F32),
                   jax.ShapeDtypeStruct((2, B, H, Nh, Nh), F32)),
        grid=(B, ng),
        in_specs=[fwd, fwd, fwd, fwd, fwd2, fwd2, rev, rev, rev, rev, rev2, rev2,
                  pl.BlockSpec((1, C), lambda b, j: (0, 0)), st],
        out_specs=(fwd, rev, st),
        scratch_shapes=[pltpu.VMEM((2, H, Nh, Nh), F32)],
        compiler_params=_cparams("arbitrary", "arbitrary"),
        name="rwkv_scan",
    )(r, k, v, kn, lw, a, r, k, v, kn, lw, a, k_a, s0)
    return yf, yr, sf


def _rwkv_post_kernel(yf_ref, yr_ref, bonus_ref, g_ref, gg_ref, gb_ref, bd_ref, o_ref):
    y = yf_ref[...] + yr_ref[...]
    bd = bd_ref[...]
    inv_n = 1.0 / RWKV_HEAD
    mean = _dot_exact_rhs(y, bd) * inv_n
    yc = y - mean
    var = _dot_exact_rhs(yc * yc, bd) * inv_n
    yn = yc * lax.rsqrt(var + GN_EPS) * gg_ref[...] + gb_ref[...]
    o_ref[...] = ((yn + bonus_ref[...].astype(F32)) * g_ref[...].astype(F32)).astype(o_ref.dtype)


def _rwkv_post(yf, yr, bonus, g, gn_g, gn_b, bd):
    N, C = yf.shape
    tm = _pick(N, (1024, 512, 256))
    row = pl.BlockSpec((tm, C), lambda i: (i, 0))
    vec = pl.BlockSpec((1, C), lambda i: (0, 0))
    return pl.pallas_call(
        _rwkv_post_kernel,
        out_shape=jax.ShapeDtypeStruct((N, C), BF16),
        grid=(N // tm,),
        in_specs=[row, row, row, row, vec, vec, pl.BlockSpec((C, C), lambda i: (0, 0))],
        out_specs=row,
        compiler_params=_cparams("arbitrary"),
        name="rwkv_post",
    )(yf, yr, bonus, g, gn_g.reshape(1, C), gn_b.reshape(1, C), bd)


def _fourier_ch_kernel(p_ref, m_ref, o_ref):
    p = p_ref[...]
    mat = m_ref[...]
    G = FOURIER_GROUP_DIM
    for g in range(p.shape[1] // G):
        o_ref[:, g * G:(g + 1) * G] = jnp.dot(p[:, g * G:(g + 1) * G], mat,
                                              preferred_element_type=F32).astype(o_ref.dtype)


def _fourier_channel(proj, col, Fd, ch_mats, B, T):
    tm = _pick(T, (1024, 512, 256))
    nt = T // tm
    G = FOURIER_GROUP_DIM
    return pl.pallas_call(
        _fourier_ch_kernel,
        out_shape=jax.ShapeDtypeStruct((B, 2, T, Fd), BF16),
        grid=(B, nt, 2),
        in_specs=[pl.BlockSpec((tm, Fd), lambda b, i, s: (b * nt + i, col)),
                  pl.BlockSpec((None, G, G), lambda b, i, s: (s, 0, 0))],
        out_specs=pl.BlockSpec((None, None, tm, Fd), lambda b, i, s: (b, s, i, 0)),
        compiler_params=_cparams("arbitrary", "arbitrary", "arbitrary"),
        name="fourier_channel",
    )(proj, ch_mats)


def _dft_mats(n):
    idx = jnp.arange(n, dtype=jnp.int32)
    ph = (idx[:, None] * idx[None, :]) % n
    ang = ph.astype(F32) * (2.0 * math.pi / n)
    s = 1.0 / math.sqrt(n)
    return jnp.cos(ang) * s, jnp.sin(ang) * s


DFT_COLS = 8


def _dft1_kernel(z_ref, m1_ref, tc_ref, ts_ref, o_ref, *, T1):
    m1 = m1_ref[...]
    tc = tc_ref[...]
    ts = ts_ref[...]
    for j in range(DFT_COLS):
        zj = jnp.concatenate([z_ref[0, :, j, :], z_ref[1, :, j, :]], axis=0)
        a = jnp.dot(m1, zj, preferred_element_type=F32)
        ar = a[:T1]
        ai = a[T1:]
        c = tc[:, j:j + 1]
        s = ts[:, j:j + 1]
        o_ref[0, :, j, :] = (ar * c + ai * s).astype(o_ref.dtype)
        o_ref[1, :, j, :] = (ai * c - ar * s).astype(o_ref.dtype)


def _dft2_kernel(b_ref, m2_ref, o_ref):
    m2 = m2_ref[...]
    for j in range(DFT_COLS):
        rhs = jnp.concatenate([b_ref[0, j], b_ref[1, j]], axis=0)
        o_ref[:, j, :] = jnp.dot(m2, rhs, preferred_element_type=F32).astype(o_ref.dtype)


def _fourier_time(z, B, T, Fd):
    T2 = GRID_W
    T1 = T // T2
    if T1 % DFT_COLS or T2 % DFT_COLS:
        cos, sin = _dft_mats(T)
        mat = jnp.concatenate([cos, sin], axis=1).astype(BF16)
        zz = z.reshape(B, 2 * T, Fd)
        return jnp.concatenate([_mm(mat, zz[b], BF16) for b in range(B)], axis=0)
    c1, s1 = _dft_mats(T1)
    m1 = jnp.concatenate([jnp.concatenate([c1, s1], axis=1),
                          jnp.concatenate([-s1, c1], axis=1)], axis=0).astype(BF16)
    c2, s2 = _dft_mats(T2)
    m2 = jnp.concatenate([c2, s2], axis=1).astype(BF16)
    ph = (jnp.arange(T1, dtype=jnp.int32)[:, None] * jnp.arange(T2, dtype=jnp.int32)[None, :]) % T
    ang = ph.astype(F32) * (2.0 * math.pi / T)
    nt2 = T2 // DFT_COLS
    twc = jnp.cos(ang).reshape(T1, nt2, DFT_COLS).transpose(1, 0, 2)
    tws = jnp.sin(ang).reshape(T1, nt2, DFT_COLS).transpose(1, 0, 2)
    blk1 = pl.BlockSpec((None, 2, T1, DFT_COLS, Fd), lambda b, i: (b, 0, 0, i, 0))
    stage1 = pl.pallas_call(
        functools.partial(_dft1_kernel, T1=T1),
        out_shape=jax.ShapeDtypeStruct((B, 2, T1, T2, Fd), BF16),
        grid=(B, nt2),
        in_specs=[blk1,
                  pl.BlockSpec((2 * T1, 2 * T1), lambda b, i: (0, 0)),
                  pl.BlockSpec((None, T1, DFT_COLS), lambda b, i: (i, 0, 0)),
                  pl.BlockSpec((None, T1, DFT_COLS), lambda b, i: (i, 0, 0))],
        out_specs=blk1,
        compiler_params=_cparams("arbitrary", "arbitrary"),
        name="dft_stage1",
    )(z.reshape(B, 2, T1, T2, Fd), m1, twc, tws)
    out = pl.pallas_call(
        _dft2_kernel,
        out_shape=jax.ShapeDtypeStruct((B, T2, T1, Fd), BF16),
        grid=(B, T1 // DFT_COLS),
        in_specs=[pl.BlockSpec((None, 2, DFT_COLS, T2, Fd), lambda b, i: (b, 0, i, 0, 0)),
                  pl.BlockSpec((T2, 2 * T2), lambda b, i: (0, 0))],
        out_specs=pl.BlockSpec((None, T2, DFT_COLS, Fd), lambda b, i: (b, 0, i, 0)),
        compiler_params=_cparams("arbitrary", "arbitrary"),
        name="dft_stage2",
    )(stage1, m2)
    return out.reshape(B * T, Fd)


def _merge_kernel(fa_ref, at_ref, rw_ref, gt_ref, x_ref, g1_ref, wa_ref, wb_ref, wc_ref, wo_ref, o_ref):
    D = x_ref.shape[1]
    ya = jnp.dot(fa_ref[...], wa_ref[...], preferred_element_type=F32)
    yb = jnp.dot(at_ref[0], wb_ref[0], preferred_element_type=F32)
    for h in range(1, at_ref.shape[0]):
        yb = yb + jnp.dot(at_ref[h], wb_ref[h], preferred_element_type=F32)
    yc = jnp.dot(rw_ref[...], wc_ref[...], preferred_element_type=F32)
    gt = gt_ref[...].astype(F32)
    merged = (_sigmoid(gt[:, :D]) * ya + _sigmoid(gt[:, D:2 * D]) * yb + _sigmoid(gt[:, 2 * D:]) * yc)
    out = jnp.dot(merged.astype(BF16), wo_ref[...], preferred_element_type=F32)
    o_ref[...] = x_ref[...] + g1_ref[...] * out


def _merge(fa, attn, rw, proj, gcol, x, g1, wa, wb, wc, wo, B, T):
    N, D = x.shape
    H = attn.shape[1]
    Fd = rw.shape[1]
    tm = _pick(T, (512, 256))
    nt = T // tm
    Bm = g1.shape[0]
    full = lambda shape: pl.BlockSpec(shape, lambda b, i: (0,) * len(shape))
    return pl.pallas_call(
        _merge_kernel,
        out_shape=jax.ShapeDtypeStruct((N, D), F32),
        grid=(B, nt),
        in_specs=[pl.BlockSpec((tm, Fd), lambda b, i: (b * nt + i, 0)),
                  pl.BlockSpec((None, H, tm, LANES), lambda b, i: (b, 0, i, 0)),
                  pl.BlockSpec((tm, Fd), lambda b, i: (b * nt + i, 0)),
                  pl.BlockSpec((tm, 3 * D), lambda b, i: (b * nt + i, gcol)),
                  pl.BlockSpec((tm, D), lambda b, i: (b * nt + i, 0)),
                  pl.BlockSpec((None, 1, D), lambda b, i: (b % Bm, 0, 0)),
                  full(wa.shape), full(wb.shape), full(wc.shape), full(wo.shape)],
        out_specs=pl.BlockSpec((tm, D), lambda b, i: (b * nt + i, 0)),
        compiler_params=_cparams("arbitrary", "arbitrary"),
        name="merge_out",
    )(fa, attn, rw, proj, x, g1, wa, wb, wc, wo)


def _route(scores, sel, n_exp):
    tm = scores.shape[0]
    lane = lax.broadcasted_iota(jnp.int32, (tm, LANES), 1)
    neg = -jnp.inf
    sel = jnp.where(lane < n_exp, sel, neg)
    epg = n_exp // N_GROUPS
    assert epg == 4
    odd = (lane % 2) == 1
    p1 = jnp.where(odd, pltpu.roll(sel, 1, axis=1), pltpu.roll(sel, LANES - 1, axis=1))
    hi1, lo1 = jnp.maximum(sel, p1), jnp.minimum(sel, p1)
    upper = (lane % 4) >= 2
    hi2 = jnp.where(upper, pltpu.roll(hi1, 2, axis=1), pltpu.roll(hi1, LANES - 2, axis=1))
    lo2 = jnp.where(upper, pltpu.roll(lo1, 2, axis=1), pltpu.roll(lo1, LANES - 2, axis=1))
    gscore = jnp.maximum(hi1, hi2) + jnp.maximum(jnp.minimum(hi1, hi2), jnp.maximum(lo1, lo2))
    gscore = jnp.where(lane < n_exp, gscore, neg)
    gmax = jnp.max(gscore, axis=1, keepdims=True)
    lane_f = lane.astype(F32)
    grp = (lane // epg).astype(F32)
    big = float(LANES)
    best = jnp.min(jnp.where(gscore == gmax, grp, big), axis=1, keepdims=True)
    masked = jnp.where(grp == best, sel, neg)
    m1 = jnp.max(masked, axis=1, keepdims=True)
    i1 = jnp.min(jnp.where(masked == m1, lane_f, big), axis=1, keepdims=True)
    masked2 = jnp.where(lane_f == i1, neg, masked)
    m2 = jnp.max(masked2, axis=1, keepdims=True)
    i2 = jnp.min(jnp.where(masked2 == m2, lane_f, big), axis=1, keepdims=True)
    pick1 = lane_f == i1
    pick2 = lane_f == i2
    s1 = jnp.sum(jnp.where(pick1, scores, 0.0), axis=1, keepdims=True)
    s2 = jnp.sum(jnp.where(pick2, scores, 0.0), axis=1, keepdims=True)
    return pick1, pick2, s1 / (s1 + s2), s2 / (s1 + s2)


def _lane_to_row(col_vals):
    ones = jnp.ones((8, LANES), BF16)
    hi, lo = _split2(col_vals)
    return (_dot_nt(ones, hi) + _dot_nt(ones, lo))[0:1]


def _moe_kernel(x_ref, g_ref, sc_ref, sh_ref, g2_ref, rw_ref, rb_ref,
                wg_ref, wu_ref, wd_ref, swg_ref, swu_ref, swd_ref, o_ref,
                m_ref, xs_ref, ys_ref, pm_ref, ws_ref, oc_ref, *, n_exp):
    e = pl.program_id(1)
    tm = x_ref.shape[0]
    S = 2 * tm
    RB = MOE_ROWS

    @pl.when(e == 0)
    def _():
        x = x_ref[...]
        m = x * lax.rsqrt(jnp.mean(x * x, axis=-1, keepdims=True) + NORM_EPS) * g_ref[...]
        m = m * (1.0 + sc_ref[...]) + sh_ref[...]
        mb = m.astype(BF16)
        m_ref[...] = mb
        scores = _sigmoid(_dot_split(m, rw_ref[...]))
        pick1, pick2, w1, w2 = _route(scores, scores + rb_ref[...], n_exp)
        p1 = pick1.astype(F32)
        p2 = pick2.astype(F32)
        chosen = p1 + p2
        cnt = jnp.sum(chosen, axis=0, keepdims=True)
        li = lax.broadcasted_iota(jnp.int32, (LANES, LANES), 0)
        lj = lax.broadcasted_iota(jnp.int32, (LANES, LANES), 1)
        off = _dot_exact_rhs(jnp.broadcast_to(cnt, (8, LANES)), (li < lj).astype(F32).astype(BF16))[0:1]
        ti = lax.broadcasted_iota(jnp.int32, (tm, tm), 0)
        tj = lax.broadcasted_iota(jnp.int32, (tm, tm), 1)
        rank = jnp.dot((tj < ti).astype(F32).astype(BF16), chosen.astype(BF16),
                       preferred_element_type=F32)
        dest = off + rank
        d1 = _lane_to_row(p1 * dest)
        d2 = _lane_to_row(p2 * dest)
        w1r = _lane_to_row(jnp.broadcast_to(w1, (tm, LANES)) * p1)
        w2r = _lane_to_row(jnp.broadcast_to(w2, (tm, LANES)) * p2)
        row = lax.broadcasted_iota(jnp.int32, (S, tm), 0).astype(F32)
        is1 = row == d1
        is2 = row == d2
        perm = jnp.where(is1, 1.0, jnp.where(is2, 1.0, 0.0)).astype(BF16)
        pm_ref[...] = perm
        ws_ref[...] = jnp.sum(jnp.where(is1, w1r, jnp.where(is2, w2r, 0.0)), axis=1, keepdims=True)
        xs_ref[...] = jnp.dot(perm, mb, preferred_element_type=F32).astype(BF16)
        ys_ref[...] = jnp.zeros_like(ys_ref)
        oc_ref[0:1, :] = off
        oc_ref[1:2, :] = cnt

    @pl.when(e < n_exp)
    def _():
        lane = lax.broadcasted_iota(jnp.int32, (1, LANES), 1)
        o_e = jnp.sum(jnp.where(lane == e, oc_ref[0:1, :], 0.0)).astype(jnp.int32)
        c_e = jnp.sum(jnp.where(lane == e, oc_ref[1:2, :], 0.0)).astype(jnp.int32)
        wg = wg_ref[...].astype(BF16)
        wu = wu_ref[...].astype(BF16)
        wd = wd_ref[...].astype(BF16)
        b0 = o_e // RB
        b1 = jnp.where(c_e > 0, (o_e + c_e + RB - 1) // RB, b0)

        def body(b, carry):
            r0 = pl.multiple_of(b * RB, RB)
            xb = xs_ref[pl.ds(r0, RB), :]
            hg = jnp.dot(xb, wg, preferred_element_type=F32)
            hu = jnp.dot(xb, wu, preferred_element_type=F32)
            rows = r0 + lax.broadcasted_iota(jnp.int32, (RB, 1), 0)
            mine = (rows >= o_e) & (rows < o_e + c_e)
            wrow = jnp.where(mine, ws_ref[pl.ds(r0, RB), :], 0.0)
            hh = hg * _sigmoid(hg) * hu * wrow
            ys_ref[pl.ds(r0, RB), :] += jnp.dot(hh.astype(BF16), wd, preferred_element_type=F32)
            return carry

        lax.fori_loop(b0, b1, body, 0)

    @pl.when(e == n_exp)
    def _():
        m = m_ref[...]
        hg = jnp.dot(m, swg_ref[...], preferred_element_type=F32)
        hu = jnp.dot(m, swu_ref[...], preferred_element_type=F32)
        hh = hg * _sigmoid(hg) * hu
        y = jnp.dot(hh.astype(BF16), swd_ref[...], preferred_element_type=F32)
        y = y + _dot_tn(pm_ref[...], ys_ref[...])
        o_ref[...] = x_ref[...] + g2_ref[...] * y


def _moe(x, g, sc, sh, g2, rw_pad, rb_pad, layer, wg, wu, wd, swg, swu, swd):
    N, D = x.shape
    _, E, _, Ff = wg.shape
    Bm = sc.shape[0]
    rpm = N // Bm
    tm = min(rpm, 1024)
    assert rpm % tm == 0 and (2 * tm) % MOE_ROWS == 0
    one = pl.Buffered(1)
    mod = pl.BlockSpec((None, 1, D), lambda i, e: (i * tm // rpm, 0, 0))
    full = lambda shape: pl.BlockSpec(shape, lambda i, e: (0,) * len(shape), pipeline_mode=one)
    return pl.pallas_call(
        functools.partial(_moe_kernel, n_exp=E),
        out_shape=jax.ShapeDtypeStruct((N, D), F32),
        grid=(N // tm, E + 1),
        in_specs=[pl.BlockSpec((tm, D), lambda i, e: (i, 0), pipeline_mode=one),
                  full((1, D)), mod, mod, mod,
                  full(rw_pad.shape), full(rb_pad.shape),
                  pl.BlockSpec((None, None, D, Ff), lambda i, e: (layer, jnp.minimum(e, E - 1), 0, 0)),
                  pl.BlockSpec((None, None, D, Ff), lambda i, e: (layer, jnp.minimum(e, E - 1), 0, 0)),
                  pl.BlockSpec((None, None, Ff, D), lambda i, e: (layer, jnp.minimum(e, E - 1), 0, 0)),
                  full(swg.shape), full(swu.shape), full(swd.shape)],
        out_specs=pl.BlockSpec((tm, D), lambda i, e: (i, 0), pipeline_mode=one),
        scratch_shapes=[pltpu.VMEM((tm, D), BF16), pltpu.VMEM((2 * tm, D), BF16),
                        pltpu.VMEM((2 * tm, D), F32), pltpu.VMEM((2 * tm, tm), BF16),
                        pltpu.VMEM((2 * tm, 1), F32), pltpu.VMEM((8, LANES), F32)],
        compiler_params=pltpu.CompilerParams(dimension_semantics=("arbitrary", "arbitrary"),
                                             vmem_limit_bytes=MOE_VMEM_LIMIT_BYTES),
        name="moe",
    )(x, g.reshape(1, D), sc, sh, g2, rw_pad, rb_pad, wg, wu, wd, swg, swu, swd)


def _rope_partner(w):
    lead = w.shape[:-1]
    wr = w.reshape(lead + (2, 2, ROPE_PAIRS))
    return jnp.stack([-wr[..., 1, :], wr[..., 0, :]], axis=-2).reshape(w.shape)


def _rope_tables(T):
    rows = T // GRID_W
    row = jnp.repeat(jnp.arange(rows), GRID_W)
    col = jnp.tile(jnp.arange(GRID_W), rows)
    inv = ROPE_BASE ** (-jnp.arange(ROPE_PAIRS, dtype=F32) / ROPE_PAIRS)
    ang = jnp.stack([row, col], axis=-1).astype(F32)[..., None] * inv
    cos = jnp.broadcast_to(jnp.cos(ang)[:, :, None, :], (T, 2, 2, ROPE_PAIRS)).reshape(T, MLA_ROPE)
    sin = jnp.broadcast_to(jnp.sin(ang)[:, :, None, :], (T, 2, 2, ROPE_PAIRS)).reshape(T, MLA_ROPE)
    pad = LANES - MLA_QK
    z_n, z_p = jnp.zeros((T, MLA_NOPE), F32), jnp.zeros((T, pad), F32)
    cq = jnp.concatenate([jnp.ones((T, MLA_NOPE), F32), cos, z_p], axis=1)
    sq = jnp.concatenate([z_n, sin, z_p], axis=1)
    ck = jnp.concatenate([z_n, cos, z_p], axis=1)
    return cq, sq, ck, sq


def _identity_tables(T):
    pad = LANES - MLA_QK
    z_n, z_p = jnp.zeros((T, MLA_NOPE), F32), jnp.zeros((T, pad), F32)
    one_r = jnp.ones((T, MLA_ROPE), F32)
    cq = jnp.concatenate([jnp.ones((T, MLA_NOPE), F32), one_r, z_p], axis=1)
    ck = jnp.concatenate([z_n, one_r, z_p], axis=1)
    zero = jnp.zeros((T, LANES), F32)
    return cq, zero, ck, zero


def kernel(x, c, ctx, c_ctx, w_ada, b_ada, norm1_g, norm2_g, w_in, q_norm_g, kv_norm_g, w_uq, w_ukv, rwkv_mu, rwkv_w0, rwkv_w_up, rwkv_a0, rwkv_a_up, rwkv_g_up, rwkv_k_k, rwkv_k_a, rwkv_r_k, rwkv_gn_g, rwkv_gn_b, w_proj_a, w_proj_b, w_proj_c, w_out, router_w, router_b, exp_w_gate, exp_w_up, exp_w_down, sh_w_gate, sh_w_up, sh_w_down, final_norm_g):
    B, T, D = x.shape
    Tc = ctx.shape[1]
    L = w_ada.shape[0]
    H = MLA_HEADS
    QL, KVL = q_norm_g.shape[1], kv_norm_g.shape[1]
    Fd = w_proj_a.shape[1]
    C = RWKV_DIM
    E = router_w.shape[1]
    N, Nc = B * T, B * Tc

    sizes = (Fd, QL, KVL, MLA_ROPE, 3 * C + DECAY_LORA + AAA_LORA + GATE_LORA, D, D, D)
    offs = np.concatenate([[0], np.cumsum(sizes)])

    lat_tabs = _rope_tables(T)
    ctx_tabs = _identity_tables(Tc)
    cc, cs = _dft_mats(FOURIER_GROUP_DIM)
    ch_mats = jnp.stack([cc, -cs]).astype(BF16)
    hid = jnp.arange(C) // RWKV_HEAD
    bd = (hid[:, None] == hid[None, :]).astype(BF16)
    rw_pad = jnp.zeros((D, LANES), F32).at[:, :E].set(router_w)
    rb_pad = jnp.zeros((1, LANES), F32).at[0, :E].set(router_b)

    R = -(-(B + 1) // 8) * 8
    c_rows = jnp.zeros((R, D), F32).at[:B].set(c).at[B].set(c_ctx)
    mod = _ada_mod(c_rows, w_ada, b_ada)

    x = x.reshape(N, D)
    ctx = ctx.reshape(Nc, D)
    zero_state = jnp.zeros((2, B, RWKV_HEADS, RWKV_HEAD, RWKV_HEAD), F32)

    for l in range(L):
        last = l == L - 1
        ml = mod[l, :B].reshape(B, 1, 6, D)
        mc = mod[l, B:B + 1].reshape(1, 1, 6, D)
        sh1, sc1, g1, sh2, sc2, g2 = [ml[:, :, j] for j in range(6)]
        csh1, csc1, cg1, csh2, csc2, cg2 = [mc[:, :, j] for j in range(6)]

        wi = w_in[l]
        w_kr = wi[:, offs[3]:offs[4]]
        zl = jnp.zeros((D, MLA_NOPE), F32)
        zr = jnp.zeros((D, LANES - MLA_QK), F32)
        segs = [wi[:, offs[5]:offs[8]],
                wi[:, offs[0]:offs[1]],
                wi[:, offs[4]:offs[5]],
                jnp.concatenate([wi[:, offs[1]:offs[3]], zl, w_kr, zr, zl, _rope_partner(w_kr), zr], axis=1)]
        seg_off, n_all = _column_layout([s.shape[1] for s in segs])
        pieces, cur = [], 0
        for s, o in zip(segs, seg_off):
            if o > cur:
                pieces.append(jnp.zeros((D, o - cur), F32))
            pieces.append(s)
            cur = o + s.shape[1]
        if n_all > cur:
            pieces.append(jnp.zeros((D, n_all - cur), F32))
        w_all = jnp.concatenate(pieces, axis=1).astype(BF16)
        gt_col, pf_col, rw_col, qkv_col = [o // s.shape[1] for s, o in zip(segs, seg_off)]
        uq = w_uq[l].reshape(QL, H, MLA_QK).transpose(1, 0, 2)
        padq = jnp.zeros((H, QL, LANES - MLA_QK), F32)
        wqa = jnp.concatenate([uq, padq], axis=2).astype(BF16)
        wqb = jnp.concatenate([jnp.zeros((H, QL, MLA_NOPE), F32), _rope_partner(uq[..., MLA_NOPE:]), padq],
                              axis=2).astype(BF16)
        ukv = w_ukv[l].reshape(KVL, H, MLA_NOPE + MLA_V).transpose(1, 0, 2)
        wk = jnp.concatenate([ukv[..., :MLA_NOPE], jnp.zeros((H, KVL, LANES - MLA_NOPE), F32)], axis=2).astype(BF16)
        zv = jnp.zeros((H, KVL, LANES - MLA_V), F32)
        even = (jnp.arange(H) % 2 == 0)[:, None, None]
        wv = jnp.where(even, jnp.concatenate([ukv[..., MLA_NOPE:], zv], axis=2),
                       jnp.concatenate([zv, ukv[..., MLA_NOPE:]], axis=2)).astype(BF16)
        wb = w_proj_b[l].reshape(H // 2, 2 * MLA_V, D).astype(BF16)
        wa = w_proj_a[l].astype(BF16)
        wc = w_proj_c[l].astype(BF16)
        wo = w_out[l].astype(BF16)
        zd = jnp.zeros((2, DECAY_LORA, C), F32)
        prm = dict(
            mu=rwkv_mu[l].reshape(1, -1), w0=rwkv_w0[l], a0=rwkv_a0[l],
            wup=jnp.concatenate([rwkv_w_up[l], zd], axis=1).astype(BF16),
            aup=jnp.concatenate([zd, rwkv_a_up[l]], axis=1).astype(BF16),
            gup=rwkv_g_up[l].astype(BF16), k_k=rwkv_k_k[l].reshape(1, C), k_a=rwkv_k_a[l].reshape(1, C),
            r_k=rwkv_r_k[l].reshape(1, C), bd=bd)

        def mixers(xx, sc, sh, tabs, Tx):
            proj = _in_proj(xx, norm1_g[l], sc, sh, w_all)
            q, k, v = _qkv_project(proj, qkv_col, q_norm_g[l], kv_norm_g[l], wqa, wqb, wk, wv, tabs, B, Tx)
            rp = _rwkv_prep(proj, rw_col, prm, B, Tx)
            return proj, q, k, v, rp

        proj_l, q_l, k_l, v_l, rp_l = mixers(x, sc1, sh1, lat_tabs, T)
        proj_c, q_c, k_c, v_c, rp_c = mixers(ctx, csc1, csh1, ctx_tabs, Tc)

        attn_l = _attention(q_l, [(k_l, v_l), (k_c, v_c)])

        yf_c, yr_c, s_ctx = _rwkv_scan(*rp_c[:6], prm["k_a"], zero_state, B, Tc)
        yf_l, yr_l, _ = _rwkv_scan(*rp_l[:6], prm["k_a"], s_ctx, B, T)
        rwo_l = _rwkv_post(yf_l, yr_l, rp_l[7], rp_l[6], rwkv_gn_g[l], rwkv_gn_b[l], bd)

        fa_l = _fourier_time(_fourier_channel(proj_l, pf_col, Fd, ch_mats, B, T), B, T, Fd)

        x = _merge(fa_l, attn_l, rwo_l, proj_l, gt_col, x, g1, wa, wb, wc, wo, B, T)
        moe_w = (rw_pad, rb_pad, l, exp_w_gate, exp_w_up, exp_w_down,
                 sh_w_gate[l].astype(BF16), sh_w_up[l].astype(BF16), sh_w_down[l].astype(BF16))
        x = _moe(x, norm2_g[l], sc2, sh2, g2, *moe_w)

        if not last:
            attn_c = _attention(q_c, [(k_c, v_c)])
            rwo_c = _rwkv_post(yf_c, yr_c, rp_c[7], rp_c[6], rwkv_gn_g[l], rwkv_gn_b[l], bd)
            fa_c = _fourier_time(_fourier_channel(proj_c, pf_col, Fd, ch_mats, B, Tc), B, Tc, Fd)
            ctx = _merge(fa_c, attn_c, rwo_c, proj_c, gt_col, ctx, cg1, wa, wb, wc, wo, B, Tc)
            ctx = _moe(ctx, norm2_g[l], csc2, csh2, cg2, *moe_w)

    ones = jnp.ones((1, 1, D), F32)
    out = _norm(x, final_norm_g, ones, ones, F32, modulate=False)
    return out.reshape(B, T, D)
```

```python
import functools
import math

import numpy as np
import jax
import jax.numpy as jnp
from jax import lax
from jax.experimental import pallas as pl
from jax.experimental.pallas import tpu as pltpu

F32 = jnp.float32
BF16 = jnp.bfloat16

GRID_W = 64
FOURIER_GROUP_DIM = 128
MLA_HEADS = 8
MLA_NOPE = 64
MLA_ROPE = 32
MLA_QK = MLA_NOPE + MLA_ROPE
MLA_V = 64
MLA_SCALE = MLA_QK ** -0.5
QK_PRESCALE = MLA_SCALE * math.log2(math.e)
ROPE_PAIRS = MLA_ROPE // 4
ROPE_BASE = 10000.0
RWKV_HEADS = 8
RWKV_HEAD = 64
RWKV_DIM = RWKV_HEADS * RWKV_HEAD
DECAY_LORA = 64
AAA_LORA = 64
GATE_LORA = 128
GN_EPS = 64e-5
N_GROUPS = 4
NORM_EPS = 1e-6

LANES = 128
VMEM_LIMIT_BYTES = 48 * 1024 * 1024
RWKV_CHUNK = 64
MOE_ROWS = 128
MOE_VMEM_LIMIT_BYTES = 56 * 1024 * 1024


def _cparams(*sem, fuse_inputs=None):
    return pltpu.CompilerParams(dimension_semantics=sem, vmem_limit_bytes=VMEM_LIMIT_BYTES,
                                allow_input_fusion=fuse_inputs)


def _sigmoid(z):
    return 1.0 / (1.0 + jnp.exp(-z))


def _bdot(a, b):
    return jnp.dot(a.astype(BF16), b.astype(BF16), preferred_element_type=F32)


def _dot_nt(a, b):
    return lax.dot_general(a.astype(BF16), b.astype(BF16), (((1,), (1,)), ((), ())),
                           preferred_element_type=F32)


def _dot_tn(a, b):
    return lax.dot_general(a.astype(BF16), b.astype(BF16), (((0,), (0,)), ((), ())),
                           preferred_element_type=F32)


def _split2(x):
    hi = x.astype(BF16)
    lo = (x - hi.astype(F32)).astype(BF16)
    return hi, lo


def _dot_exact_rhs(x, ones_bf16):
    hi, lo = _split2(x)
    return (jnp.dot(hi, ones_bf16, preferred_element_type=F32)
            + jnp.dot(lo, ones_bf16, preferred_element_type=F32))


def _dot_split(a, b):
    ah, al = _split2(a)
    bh, bl = _split2(b)
    return (jnp.dot(ah, bh, preferred_element_type=F32)
            + jnp.dot(ah, bl, preferred_element_type=F32)
            + jnp.dot(al, bh, preferred_element_type=F32))


def _ada_kernel(c_ref, w_ref, b_ref, o_ref):
    c = c_ref[...]
    s = c * _sigmoid(c)
    o_ref[...] = _dot_split(s, w_ref[...]) + b_ref[...]


def _ada_mod(c_rows, w_ada, b_ada):
    L, D, D6 = w_ada.shape
    R = c_rows.shape[0]
    tn = 1536 if D6 % 1536 == 0 else D6
    return pl.pallas_call(
        _ada_kernel,
        out_shape=jax.ShapeDtypeStruct((L, R, D6), F32),
        grid=(L, D6 // tn),
        in_specs=[pl.BlockSpec((R, D), lambda l, j: (0, 0)),
                  pl.BlockSpec((None, D, tn), lambda l, j: (l, 0, j)),
                  pl.BlockSpec((None, 1, tn), lambda l, j: (l, 0, j))],
        out_specs=pl.BlockSpec((None, R, tn), lambda l, j: (l, 0, j)),
        compiler_params=_cparams("arbitrary", "arbitrary"),
        name="ada_mod",
    )(c_rows, w_ada, b_ada.reshape(L, 1, D6))


def _norm_kernel(x_ref, g_ref, sc_ref, sh_ref, o_ref, *, modulate):
    x = x_ref[...].astype(F32)
    y = x * lax.rsqrt(jnp.mean(x * x, axis=-1, keepdims=True) + NORM_EPS) * g_ref[...]
    if modulate:
        y = y * (1.0 + sc_ref[...]) + sh_ref[...]
    o_ref[...] = y.astype(o_ref.dtype)


def _norm(x, g, sc, sh, out_dtype, modulate=True):
    N, D = x.shape
    Bm = sc.shape[0]
    rpm = N // Bm
    tm = min(rpm, 1024)
    assert rpm % tm == 0
    return pl.pallas_call(
        functools.partial(_norm_kernel, modulate=modulate),
        out_shape=jax.ShapeDtypeStruct((N, D), out_dtype),
        grid=(N // tm,),
        in_specs=[pl.BlockSpec((tm, D), lambda i: (i, 0)),
                  pl.BlockSpec((1, D), lambda i: (0, 0)),
                  pl.BlockSpec((None, 1, D), lambda i: (i * tm // rpm, 0, 0)),
                  pl.BlockSpec((None, 1, D), lambda i: (i * tm // rpm, 0, 0))],
        out_specs=pl.BlockSpec((tm, D), lambda i: (i, 0)),
        compiler_params=_cparams("arbitrary"),
        name="rmsnorm_mod",
    )(x, g.reshape(1, D), sc, sh)


def _in_proj_kernel(x_ref, g_ref, sc_ref, sh_ref, w_ref, o_ref, h_ref):
    @pl.when(pl.program_id(1) == 0)
    def _():
        x = x_ref[...]
        y = x * lax.rsqrt(jnp.mean(x * x, axis=-1, keepdims=True) + NORM_EPS) * g_ref[...]
        h_ref[...] = (y * (1.0 + sc_ref[...]) + sh_ref[...]).astype(BF16)

    o_ref[...] = jnp.dot(h_ref[...], w_ref[...], preferred_element_type=F32).astype(o_ref.dtype)


def _in_proj(x, g, sc, sh, w_all):
    N, D = x.shape
    Nout = w_all.shape[1]
    Bm = sc.shape[0]
    rpm = N // Bm
    tm = min(rpm, 512)
    assert rpm % tm == 0
    tn = Nout
    mod = pl.BlockSpec((None, 1, D), lambda i, j: (i * tm // rpm, 0, 0))
    return pl.pallas_call(
        _in_proj_kernel,
        out_shape=jax.ShapeDtypeStruct((N, Nout), BF16),
        grid=(N // tm, Nout // tn),
        in_specs=[pl.BlockSpec((tm, D), lambda i, j: (i, 0)),
                  pl.BlockSpec((1, D), lambda i, j: (0, 0)), mod, mod,
                  pl.BlockSpec((D, tn), lambda i, j: (0, j), pipeline_mode=pl.Buffered(1))],
        out_specs=pl.BlockSpec((tm, tn), lambda i, j: (i, j)),
        scratch_shapes=[pltpu.VMEM((tm, D), BF16)],
        compiler_params=_cparams("arbitrary", "arbitrary", fuse_inputs=[False] * 4 + [True]),
        name="in_proj",
    )(x, g.reshape(1, D), sc, sh, w_all)


def _column_layout(widths):
    offs, cur = [], 0
    for w in widths:
        cur = -(-cur // w) * w
        offs.append(cur)
        cur += w
    return offs, -(-cur // LANES) * LANES


def _mm_kernel(a_ref, b_ref, o_ref):
    o_ref[...] = jnp.dot(a_ref[...], b_ref[...], preferred_element_type=F32).astype(o_ref.dtype)


def _mm_acc_kernel(a_ref, b_ref, o_ref, acc_ref, *, nk):
    k = pl.program_id(2)

    @pl.when(k == 0)
    def _():
        acc_ref[...] = jnp.zeros_like(acc_ref)

    acc_ref[...] += jnp.dot(a_ref[...], b_ref[...], preferred_element_type=F32)

    @pl.when(k == nk - 1)
    def _():
        o_ref[...] = acc_ref[...].astype(o_ref.dtype)


def _pick(n, prefs):
    for p in prefs:
        if n % p == 0:
            return p
    return n


def _mm(a, w, out_dtype, tn=None, tk=None):
    M, K = a.shape
    Nn = w.shape[1]
    tm = _pick(M, (1024, 512, 256))
    tn = tn or _pick(Nn, (1024, 896, 512))
    tk = tk or K
    if tk == K:
        return pl.pallas_call(
            _mm_kernel,
            out_shape=jax.ShapeDtypeStruct((M, Nn), out_dtype),
            grid=(M // tm, Nn // tn),
            in_specs=[pl.BlockSpec((tm, K), lambda i, j: (i, 0)),
                      pl.BlockSpec((K, tn), lambda i, j: (0, j))],
            out_specs=pl.BlockSpec((tm, tn), lambda i, j: (i, j)),
            compiler_params=_cparams("arbitrary", "arbitrary"),
            name="matmul",
        )(a, w)
    nk = K // tk
    return pl.pallas_call(
        functools.partial(_mm_acc_kernel, nk=nk),
        out_shape=jax.ShapeDtypeStruct((M, Nn), out_dtype),
        grid=(M // tm, Nn // tn, nk),
        in_specs=[pl.BlockSpec((tm, tk), lambda i, j, k: (i, k)),
                  pl.BlockSpec((tk, tn), lambda i, j, k: (k, j))],
        out_specs=pl.BlockSpec((tm, tn), lambda i, j, k: (i, j)),
        scratch_shapes=[pltpu.VMEM((tm, tn), F32)],
        compiler_params=_cparams("arbitrary", "arbitrary", "arbitrary"),
        name="matmul_kacc",
    )(a, w)


def _qkv_kernel(p_ref, gq_ref, gkv_ref, wqa_ref, wqb_ref, wk_ref, wv_ref,
                cq_ref, sq_ref, ck_ref, sk_ref, vone_ref, q_ref, k_ref, v_ref, *, ql, kvl):
    p = p_ref[...].astype(F32)
    pq = p[:, :ql]
    pkv = p[:, ql:ql + kvl]
    kra = p[:, ql + kvl:ql + kvl + LANES]
    krb = p[:, ql + kvl + LANES:ql + kvl + 2 * LANES]
    qn = (pq * lax.rsqrt(jnp.mean(pq * pq, axis=-1, keepdims=True) + NORM_EPS) * gq_ref[...]).astype(BF16)
    kvn = (pkv * lax.rsqrt(jnp.mean(pkv * pkv, axis=-1, keepdims=True) + NORM_EPS) * gkv_ref[...]).astype(BF16)
    kr = kra * ck_ref[...] + krb * sk_ref[...]
    cq = cq_ref[...] * QK_PRESCALE
    sq = sq_ref[...] * QK_PRESCALE
    for h in range(q_ref.shape[0]):
        qa = jnp.dot(qn, wqa_ref[h], preferred_element_type=F32)
        qb = jnp.dot(qn, wqb_ref[h], preferred_element_type=F32)
        q_ref[h] = (qa * cq + qb * sq).astype(q_ref.dtype)
        k_ref[h] = (jnp.dot(kvn, wk_ref[h], preferred_element_type=F32) + kr).astype(k_ref.dtype)
        v_ref[h] = (jnp.dot(kvn, wv_ref[h], preferred_element_type=F32) + vone_ref[h]).astype(v_ref.dtype)


def _ones_lane(h):
    return LANES - 1 if h % 2 == 0 else 0


def _value_ones(H):
    return jnp.stack([jnp.zeros((1, LANES), F32).at[0, _ones_lane(h)].set(1.0) for h in range(H)])


def _qkv_project(proj, col, gq, gkv, wqa, wqb, wk, wv, tabs, B, T):
    N = proj.shape[0]
    H = wqa.shape[0]
    ql, kvl = wqa.shape[1], wk.shape[1]
    W = ql + kvl + 2 * LANES
    tm = _pick(T, (512, 256))
    nt = T // tm
    cq, sq, ck, sk = tabs
    full = lambda shape: pl.BlockSpec(shape, lambda i: (0,) * len(shape))
    tab_spec = pl.BlockSpec((tm, LANES), lambda i: (i % nt, 0))
    out_spec = pl.BlockSpec((None, H, tm, LANES), lambda i: (i // nt, 0, i % nt, 0))
    out_sds = jax.ShapeDtypeStruct((B, H, T, LANES), BF16)
    return pl.pallas_call(
        functools.partial(_qkv_kernel, ql=ql, kvl=kvl),
        out_shape=(out_sds, out_sds, out_sds),
        grid=(N // tm,),
        in_specs=[pl.BlockSpec((tm, W), lambda i: (i, col)),
                  full((1, ql)), full((1, kvl)),
                  full(wqa.shape), full(wqb.shape), full(wk.shape), full(wv.shape),
                  tab_spec, tab_spec, tab_spec, tab_spec, full((H, 1, LANES))],
        out_specs=(out_spec, out_spec, out_spec),
        compiler_params=_cparams("arbitrary"),
        name="mla_qkv",
    )(proj, gq.reshape(1, ql), gkv.reshape(1, kvl), wqa, wqb, wk, wv, cq, sq, ck, sk, _value_ones(H))


def _attn_kernel(q_ref, *refs, tiles):
    o_ref = refs[-1]
    out = None
    for hh in range(q_ref.shape[0]):
        q = q_ref[hh]
        m = acc = None
        for src, start, size in tiles:
            k = refs[2 * src][hh, start:start + size, :]
            v = refs[2 * src + 1][hh, start:start + size, :]
            s = _dot_nt(q, k)
            mt = jnp.max(s, axis=-1, keepdims=True)
            if m is None:
                m_new = mt
                acc = jnp.dot(jnp.exp2(s - m_new).astype(BF16), v, preferred_element_type=F32)
            else:
                m_new = jnp.maximum(m, mt)
                acc = (jnp.exp2(m - m_new) * acc
                       + jnp.dot(jnp.exp2(s - m_new).astype(BF16), v, preferred_element_type=F32))
            m = m_new
        ol = _ones_lane(hh)
        lane = lax.broadcasted_iota(jnp.int32, acc.shape, 1)
        o = jnp.where(lane == ol, 0.0, acc / acc[:, ol:ol + 1])
        out = o if out is None else out + o
    o_ref[...] = out.astype(o_ref.dtype)


def _attention(q, kvs):
    B, H, T, dq = q.shape
    HP = 2
    tq = _pick(T, (1024, 512, 256))
    tiles = []
    for src, (k, _) in enumerate(kvs):
        Tk = k.shape[2]
        tk = _pick(Tk, (1024, 512, 256))
        tiles += [(src, st, tk) for st in range(0, Tk, tk)]
    in_specs = [pl.BlockSpec((None, HP, tq, dq), lambda b, h, i: (b, h, i, 0))]
    args = [q]
    for k, v in kvs:
        Tk = k.shape[2]
        in_specs += [pl.BlockSpec((None, HP, Tk, dq), lambda b, h, i: (b, h, 0, 0)),
                     pl.BlockSpec((None, HP, Tk, LANES), lambda b, h, i: (b, h, 0, 0))]
        args += [k, v]
    return pl.pallas_call(
        functools.partial(_attn_kernel, tiles=tuple(tiles)),
        out_shape=jax.ShapeDtypeStruct((B, H // HP, T, LANES), BF16),
        grid=(B, H // HP, T // tq),
        in_specs=in_specs,
        out_specs=pl.BlockSpec((None, None, tq, LANES), lambda b, h, i: (b, h, i, 0)),
        compiler_params=_cparams("arbitrary", "arbitrary", "arbitrary"),
        name="attention",
    )(*args)


def _softplus(u):
    return jnp.maximum(u, 0.0) + jnp.log(1.0 + jnp.exp(-jnp.abs(u)))


def _rwkv_prep_kernel(p_ref, pp_ref, pn_ref, mu_ref, w0_ref, wup_ref, a0_ref, aup_ref, gup_ref,
                      kk_ref, ka_ref, rk_ref, bd_ref,
                      r_o, k_o, v_o, kn_o, lw_o, a_o, g_o, bonus_o, *, tps):
    i = pl.program_id(0)
    p = p_ref[...].astype(F32)
    tt = p.shape[0]
    C = RWKV_DIM
    halo = pp_ref.shape[0]
    prev_row = jnp.where(i % tps == 0, 0.0, pp_ref[...].astype(F32)[halo - 1:halo, :])
    next_row = jnp.where(i % tps == tps - 1, 0.0, pn_ref[...].astype(F32)[0:1, :])
    rows = lax.broadcasted_iota(jnp.int32, (tt, 1), 0)
    pprev = jnp.where(rows == 0, prev_row, pltpu.roll(p, 1, axis=0))
    pnext = jnp.where(rows == tt - 1, next_row, pltpu.roll(p, tt - 1, axis=0))
    xs = p + (0.5 * (pprev + pnext) - p) * mu_ref[...]
    r = xs[:, :C]
    k = xs[:, C:2 * C]
    v = xs[:, 2 * C:3 * C]
    xwa = xs[:, 3 * C:3 * C + DECAY_LORA + AAA_LORA]
    xg = xs[:, 3 * C + DECAY_LORA + AAA_LORA:]
    bd = bd_ref[...]
    g_o[...] = _bdot(_sigmoid(xg), gup_ref[...]).astype(g_o.dtype)
    kn = k * kk_ref[...]
    kn = kn * lax.rsqrt(_dot_exact_rhs(kn * kn, bd) + 1e-12)
    th = jnp.tanh(xwa)
    bonus = jnp.zeros_like(r)
    for d in range(2):
        w = -_softplus(-(w0_ref[d:d + 1, :] + _bdot(th, wup_ref[d]))) - 0.5
        lw_o[d] = -jnp.exp(w)
        a = _sigmoid(a0_ref[d:d + 1, :] + _bdot(xwa, aup_ref[d]))
        a_o[d] = a.astype(a_o.dtype)
        kd = k * (1.0 + (a - 1.0) * ka_ref[...])
        bonus = bonus + _dot_exact_rhs(r * kd * rk_ref[...], bd) * v
    r_o[...] = r.astype(r_o.dtype)
    k_o[...] = k.astype(k_o.dtype)
    v_o[...] = v.astype(v_o.dtype)
    kn_o[...] = kn.astype(kn_o.dtype)
    bonus_o[...] = bonus.astype(bonus_o.dtype)


def _rwkv_prep(proj, col, prm, B, T):
    N = proj.shape[0]
    W = prm["mu"].shape[1]
    C = RWKV_DIM
    tt = _pick(T, (256,))
    tps = T // tt
    halo = 16
    nh = N // halo
    full = lambda shape: pl.BlockSpec(shape, lambda i: (0,) * len(shape))
    o_spec = pl.BlockSpec((tt, C), lambda i: (i, 0))
    o2_spec = pl.BlockSpec((2, tt, C), lambda i: (0, i, 0))
    sds = jax.ShapeDtypeStruct((N, C), BF16)
    lw_sds = jax.ShapeDtypeStruct((2, N, C), F32)
    a_sds = jax.ShapeDtypeStruct((2, N, C), BF16)
    return pl.pallas_call(
        functools.partial(_rwkv_prep_kernel, tps=tps),
        out_shape=(sds, sds, sds, sds, lw_sds, a_sds, sds, sds),
        grid=(N // tt,),
        in_specs=[pl.BlockSpec((tt, W), lambda i: (i, col)),
                  pl.BlockSpec((halo, W), lambda i: (jnp.maximum(i * (tt // halo) - 1, 0), col)),
                  pl.BlockSpec((halo, W), lambda i: (jnp.minimum((i + 1) * (tt // halo), nh - 1), col)),
                  full((1, W)), full((2, C)), full((2, LANES, C)), full((2, C)), full((2, LANES, C)),
                  full((GATE_LORA, C)), full((1, C)), full((1, C)), full((1, C)), full((C, C))],
        out_specs=(o_spec, o_spec, o_spec, o_spec, o2_spec, o2_spec, o_spec, o_spec),
        compiler_params=_cparams("arbitrary"),
        name="rwkv_prep",
    )(proj, proj, proj, prm["mu"], prm["w0"], prm["wup"], prm["a0"], prm["aup"], prm["gup"],
      prm["k_k"], prm["k_a"], prm["r_k"], prm["bd"])


def _bmm(a, b):
    return lax.dot_general(a.astype(BF16), b.astype(BF16), (((2,), (1,)), ((0,), (0,))),
                           preferred_element_type=F32)


def _bmm_nt(a, b):
    return lax.dot_general(a.astype(BF16), b.astype(BF16), (((2,), (2,)), ((0,), (0,))),
                           preferred_element_type=F32)


def _bmm_tn(a, b):
    return lax.dot_general(a.astype(BF16), b.astype(BF16), (((1,), (1,)), ((0,), (0,))),
                           preferred_element_type=F32)


def _rwkv_scan_kernel(rf_ref, kf_ref, vf_ref, knf_ref, lwf_ref, af_ref,
                      rr_ref, kr_ref, vr_ref, knr_ref, lwr_ref, ar_ref, ka_ref, s0_ref,
                      yf_ref, yr_ref, sf_ref, S_ref, *, G):
    j = pl.program_id(1)

    @pl.when(j == 0)
    def _():
        S_ref[...] = s0_ref[...]

    Cc, N, H = RWKV_CHUNK, RWKV_HEAD, RWKV_HEADS
    R = G * Cc
    ti = lax.broadcasted_iota(jnp.int32, (R, R), 0)
    ii = lax.broadcasted_iota(jnp.int32, (R, R), 1)
    same = (ti // Cc) == (ii // Cc)

    def heads(x):
        return jnp.stack([x[g * Cc:(g + 1) * Cc, h * N:(h + 1) * N] for g in range(G) for h in range(H)])

    per_dir = []
    for d, refs in enumerate(((rf_ref, kf_ref, vf_ref, knf_ref, lwf_ref, af_ref),
                              (rr_ref, kr_ref, vr_ref, knr_ref, lwr_ref, ar_ref))):
        r, k, v, kn, lw, a = [x[...].astype(F32) for x in refs]
        before = (ii <= ti) if d == 0 else (ii >= ti)
        m_incl = (same & before).astype(F32).astype(BF16)
        h1 = lw.astype(BF16)
        r1 = lw - h1.astype(F32)
        h2 = r1.astype(BF16)
        h3 = (r1 - h2.astype(F32)).astype(BF16)
        cum = (jnp.dot(m_incl, h1, preferred_element_type=F32)
               + jnp.dot(m_incl, h2, preferred_element_type=F32)
               + jnp.dot(m_incl, h3, preferred_element_type=F32))
        last = [(g + 1) * Cc - 1 if d == 0 else g * Cc for g in range(G)]
        tot = jnp.concatenate([jnp.broadcast_to(cum[t:t + 1, :], (Cc, cum.shape[1])) for t in last],
                              axis=0)
        e_neg = jnp.exp(-cum)
        e_rem = jnp.exp(tot - cum)
        b = kn * a
        kd = k * (1.0 + (a - 1.0) * ka_ref[...])
        per_dir.append([heads(x) for x in (-kn * jnp.exp(cum - lw), r * jnp.exp(cum), b * e_neg, kd * e_neg,
                                           b * e_rem, kd * e_rem, v, jnp.exp(tot))])

    A, Rt, Bt, Kt, Bh, Kh, V, Wc = [jnp.concatenate([per_dir[0][i], per_dir[1][i]], axis=0) for i in range(8)]
    nb = 2 * G * H
    bi = lax.broadcasted_iota(jnp.int32, (nb, Cc, Cc), 0)
    t3 = lax.broadcasted_iota(jnp.int32, (nb, Cc, Cc), 1)
    i3 = lax.broadcasted_iota(jnp.int32, (nb, Cc, Cc), 2)
    order = jnp.where(bi < G * H, t3 - i3, i3 - t3)
    strict = order > 0
    incl = order >= 0

    AR = jnp.concatenate([A, Rt], axis=1)
    ab = _bmm_nt(AR, Bt)
    ak = _bmm_nt(AR, Kt)
    Lab = jnp.where(strict, ab[:, :Cc], 0.0)
    Mrb = jnp.where(incl, ab[:, Cc:], 0.0)
    Lak = jnp.where(strict, ak[:, :Cc], 0.0)
    Mrk = jnp.where(incl, ak[:, Cc:], 0.0)
    lv = _bmm(jnp.concatenate([Lak, Mrk], axis=1), V)
    Z = jnp.concatenate([A, lv[:, :Cc]], axis=2)
    P = Lab
    steps = int(math.log2(Cc))
    for s in range(steps):
        if s < steps - 1:
            out = _bmm(P, jnp.concatenate([Z, P], axis=2))
            Z = Z + out[:, :, :2 * N]
            P = out[:, :, 2 * N:]
        else:
            Z = Z + _bmm(P, Z)
    mz = _bmm(Mrb, Z)
    Rh = Rt + mz[:, :, :N]
    Yh = mz[:, :, N:] + lv[:, Cc:]
    gh = _bmm_tn(Z, Bh)
    Gm = gh[:, :N]
    Hm = gh[:, N:] + _bmm_tn(V, Kh)

    S = S_ref[...].reshape(2 * H, N, N)
    for g in range(G):
        gr = G - 1 - g
        pick = lambda x: jnp.concatenate([x[g * H:(g + 1) * H], x[(G + gr) * H:(G + gr + 1) * H]], axis=0)
        y = _bmm_nt(pick(Rh), S) + pick(Yh)
        S = S * pick(Wc)[:, 0:1, :] + _bmm(S, pick(Gm)) + pick(Hm)
        yf_ref[g * Cc:(g + 1) * Cc, :] = jnp.concatenate([y[h] for h in range(H)], axis=1)
        yr_ref[gr * Cc:(gr + 1) * Cc, :] = jnp.concatenate([y[H + h] for h in range(H)], axis=1)
    S = S.reshape(2, H, N, N)
    S_ref[...] = S
    sf_ref[...] = S


def _rwkv_scan(r, k, v, kn, lw, a, k_a, s0, B, T):
    N, C = r.shape
    Cc = RWKV_CHUNK
    nc = T // Cc
    G = 2 if nc % 2 == 0 else 1
    ng = nc // G
    H, Nh = RWKV_HEADS, RWKV_HEAD
    fwd = pl.BlockSpec((G * Cc, C), lambda b, j: (b * ng + j, 0))
    rev = pl.BlockSpec((G * Cc, C), lambda b, j: (b * ng + ng - 1 - j, 0))
    fwd2 = pl.BlockSpec((None, G * Cc, C), lambda b, j: (0, b * ng + j, 0))
    rev2 = pl.BlockSpec((None, G * Cc, C), lambda b, j: (1, b * ng + ng - 1 - j, 0))
    st = pl.BlockSpec((2, None, H, Nh, Nh), lambda b, j: (0, b, 0, 0, 0))
    yf, yr, sf = pl.pallas_call(
        functools.partial(_rwkv_scan_kernel, G=G),
        out_shape=(jax.ShapeDtypeStruct((N, C), F32), jax.ShapeDtypeStruct((N, C), F32),
                   jax.ShapeDtypeStruct((2, B, H, Nh, Nh), F32)),
        grid=(B, ng),
        in_specs=[fwd, fwd, fwd, fwd, fwd2, fwd2, rev, rev, rev, rev, rev2, rev2,
                  pl.BlockSpec((1, C), lambda b, j: (0, 0)), st],
        out_specs=(fwd, rev, st),
        scratch_shapes=[pltpu.VMEM((2, H, Nh, Nh), F32)],
        compiler_params=_cparams("arbitrary", "arbitrary"),
        name="rwkv_scan",
    )(r, k, v, kn, lw, a, r, k, v, kn, lw, a, k_a, s0)
    return yf, yr, sf


def _rwkv_post_kernel(yf_ref, yr_ref, bonus_ref, g_ref, gg_ref, gb_ref, bd_ref, o_ref):
    y = yf_ref[...] + yr_ref[...]
    bd = bd_ref[...]
    inv_n = 1.0 / RWKV_HEAD
    mean = _dot_exact_rhs(y, bd) * inv_n
    yc = y - mean
    var = _dot_exact_rhs(yc * yc, bd) * inv_n
    yn = yc * lax.rsqrt(var + GN_EPS) * gg_ref[...] + gb_ref[...]
    o_ref[...] = ((yn + bonus_ref[...].astype(F32)) * g_ref[...].astype(F32)).astype(o_ref.dtype)


def _rwkv_post(yf, yr, bonus, g, gn_g, gn_b, bd):
    N, C = yf.shape
    tm = _pick(N, (1024, 512, 256))
    row = pl.BlockSpec((tm, C), lambda i: (i, 0))
    vec = pl.BlockSpec((1, C), lambda i: (0, 0))
    return pl.pallas_call(
        _rwkv_post_kernel,
        out_shape=jax.ShapeDtypeStruct((N, C), BF16),
        grid=(N // tm,),
        in_specs=[row, row, row, row, vec, vec, pl.BlockSpec((C, C), lambda i: (0, 0))],
        out_specs=row,
        compiler_params=_cparams("arbitrary"),
        name="rwkv_post",
    )(yf, yr, bonus, g, gn_g.reshape(1, C), gn_b.reshape(1, C), bd)


def _fourier_ch_kernel(p_ref, m_ref, o_ref):
    p = p_ref[...]
    mat = m_ref[...]
    G = FOURIER_GROUP_DIM
    for g in range(p.shape[1] // G):
        o_ref[:, g * G:(g + 1) * G] = jnp.dot(p[:, g * G:(g + 1) * G], mat,
                                              preferred_element_type=F32).astype(o_ref.dtype)


def _fourier_channel(proj, col, Fd, ch_mats, B, T):
    tm = _pick(T, (1024, 512, 256))
    nt = T // tm
    G = FOURIER_GROUP_DIM
    return pl.pallas_call(
        _fourier_ch_kernel,
        out_shape=jax.ShapeDtypeStruct((B, 2, T, Fd), BF16),
        grid=(B, nt, 2),
        in_specs=[pl.BlockSpec((tm, Fd), lambda b, i, s: (b * nt + i, col)),
                  pl.BlockSpec((None, G, G), lambda b, i, s: (s, 0, 0))],
        out_specs=pl.BlockSpec((None, None, tm, Fd), lambda b, i, s: (b, s, i, 0)),
        compiler_params=_cparams("arbitrary", "arbitrary", "arbitrary"),
        name="fourier_channel",
    )(proj, ch_mats)


def _dft_mats(n):
    idx = jnp.arange(n, dtype=jnp.int32)
    ph = (idx[:, None] * idx[None, :]) % n
    ang = ph.astype(F32) * (2.0 * math.pi / n)
    s = 1.0 / math.sqrt(n)
    return jnp.cos(ang) * s, jnp.sin(ang) * s


DFT_COLS = 8


def _dft1_kernel(z_ref, m1_ref, tc_ref, ts_ref, o_ref, *, T1):
    m1 = m1_ref[...]
    tc = tc_ref[...]
    ts = ts_ref[...]
    for j in range(DFT_COLS):
        zj = jnp.concatenate([z_ref[0, :, j, :], z_ref[1, :, j, :]], axis=0)
        a = jnp.dot(m1, zj, preferred_element_type=F32)
        ar = a[:T1]
        ai = a[T1:]
        c = tc[:, j:j + 1]
        s = ts[:, j:j + 1]
        o_ref[0, :, j, :] = (ar * c + ai * s).astype(o_ref.dtype)
        o_ref[1, :, j, :] = (ai * c - ar * s).astype(o_ref.dtype)


def _dft2_kernel(b_ref, m2_ref, o_ref):
    m2 = m2_ref[...]
    for j in range(DFT_COLS):
        rhs = jnp.concatenate([b_ref[0, j], b_ref[1, j]], axis=0)
        o_ref[:, j, :] = jnp.dot(m2, rhs, preferred_element_type=F32).astype(o_ref.dtype)


def _fourier_time(z, B, T, Fd):
    T2 = GRID_W
    T1 = T // T2
    if T1 % DFT_COLS or T2 % DFT_COLS:
        cos, sin = _dft_mats(T)
        mat = jnp.concatenate([cos, sin], axis=1).astype(BF16)
        zz = z.reshape(B, 2 * T, Fd)
        return jnp.concatenate([_mm(mat, zz[b], BF16) for b in range(B)], axis=0)
    c1, s1 = _dft_mats(T1)
    m1 = jnp.concatenate([jnp.concatenate([c1, s1], axis=1),
                          jnp.concatenate([-s1, c1], axis=1)], axis=0).astype(BF16)
    c2, s2 = _dft_mats(T2)
    m2 = jnp.concatenate([c2, s2], axis=1).astype(BF16)
    ph = (jnp.arange(T1, dtype=jnp.int32)[:, None] * jnp.arange(T2, dtype=jnp.int32)[None, :]) % T
    ang = ph.astype(F32) * (2.0 * math.pi / T)
    nt2 = T2 // DFT_COLS
    twc = jnp.cos(ang).reshape(T1, nt2, DFT_COLS).transpose(1, 0, 2)
    tws = jnp.sin(ang).reshape(T1, nt2, DFT_COLS).transpose(1, 0, 2)
    blk1 = pl.BlockSpec((None, 2, T1, DFT_COLS, Fd), lambda b, i: (b, 0, 0, i, 0))
    stage1 = pl.pallas_call(
        functools.partial(_dft1_kernel, T1=T1),
        out_shape=jax.ShapeDtypeStruct((B, 2, T1, T2, Fd), BF16),
        grid=(B, nt2),
        in_specs=[blk1,
                  pl.BlockSpec((2 * T1, 2 * T1), lambda b, i: (0, 0)),
                  pl.BlockSpec((None, T1, DFT_COLS), lambda b, i: (i, 0, 0)),
                  pl.BlockSpec((None, T1, DFT_COLS), lambda b, i: (i, 0, 0))],
        out_specs=blk1,
        compiler_params=_cparams("arbitrary", "arbitrary"),
        name="dft_stage1",
    )(z.reshape(B, 2, T1, T2, Fd), m1, twc, tws)
    out = pl.pallas_call(
        _dft2_kernel,
        out_shape=jax.ShapeDtypeStruct((B, T2, T1, Fd), BF16),
        grid=(B, T1 // DFT_COLS),
        in_specs=[pl.BlockSpec((None, 2, DFT_COLS, T2, Fd), lambda b, i: (b, 0, i, 0, 0)),
                  pl.BlockSpec((T2, 2 * T2), lambda b, i: (0, 0))],
        out_specs=pl.BlockSpec((None, T2, DFT_COLS, Fd), lambda b, i: (b, 0, i, 0)),
        compiler_params=_cparams("arbitrary", "arbitrary"),
        name="dft_stage2",
    )(stage1, m2)
    return out.reshape(B * T, Fd)


def _merge_kernel(fa_ref, at_ref, rw_ref, gt_ref, x_ref, g1_ref, wa_ref, wb_ref, wc_ref, wo_ref, o_ref):
    D = x_ref.shape[1]
    ya = jnp.dot(fa_ref[...], wa_ref[...], preferred_element_type=F32)
    yb = jnp.dot(at_ref[0], wb_ref[0], preferred_element_type=F32)
    for h in range(1, at_ref.shape[0]):
        yb = yb + jnp.dot(at_ref[h], wb_ref[h], preferred_element_type=F32)
    yc = jnp.dot(rw_ref[...], wc_ref[...], preferred_element_type=F32)
    gt = gt_ref[...].astype(F32)
    merged = (_sigmoid(gt[:, :D]) * ya + _sigmoid(gt[:, D:2 * D]) * yb + _sigmoid(gt[:, 2 * D:]) * yc)
    out = jnp.dot(merged.astype(BF16), wo_ref[...], preferred_element_type=F32)
    o_ref[...] = x_ref[...] + g1_ref[...] * out


def _merge(fa, attn, rw, proj, gcol, x, g1, wa, wb, wc, wo, B, T):
    N, D = x.shape
    H = attn.shape[1]
    Fd = rw.shape[1]
    tm = _pick(T, (512, 256))
    nt = T // tm
    Bm = g1.shape[0]
    full = lambda shape: pl.BlockSpec(shape, lambda b, i: (0,) * len(shape))
    return pl.pallas_call(
        _merge_kernel,
        out_shape=jax.ShapeDtypeStruct((N, D), F32),
        grid=(B, nt),
        in_specs=[pl.BlockSpec((tm, Fd), lambda b, i: (b * nt + i, 0)),
                  pl.BlockSpec((None, H, tm, LANES), lambda b, i: (b, 0, i, 0)),
                  pl.BlockSpec((tm, Fd), lambda b, i: (b * nt + i, 0)),
                  pl.BlockSpec((tm, 3 * D), lambda b, i: (b * nt + i, gcol)),
                  pl.BlockSpec((tm, D), lambda b, i: (b * nt + i, 0)),
                  pl.BlockSpec((None, 1, D), lambda b, i: (b % Bm, 0, 0)),
                  full(wa.shape), full(wb.shape), full(wc.shape), full(wo.shape)],
        out_specs=pl.BlockSpec((tm, D), lambda b, i: (b * nt + i, 0)),
        compiler_params=_cparams("arbitrary", "arbitrary", fuse_inputs=[False] * 6 + [True] * 4),
        name="merge_out",
    )(fa, attn, rw, proj, x, g1, wa, wb, wc, wo)


def _route(scores, sel, n_exp):
    tm = scores.shape[0]
    lane = lax.broadcasted_iota(jnp.int32, (tm, LANES), 1)
    neg = -jnp.inf
    sel = jnp.where(lane < n_exp, sel, neg)
    epg = n_exp // N_GROUPS
    assert epg == 4
    odd = (lane % 2) == 1
    p1 = jnp.where(odd, pltpu.roll(sel, 1, axis=1), pltpu.roll(sel, LANES - 1, axis=1))
    hi1, lo1 = jnp.maximum(sel, p1), jnp.minimum(sel, p1)
    upper = (lane % 4) >= 2
    hi2 = jnp.where(upper, pltpu.roll(hi1, 2, axis=1), pltpu.roll(hi1, LANES - 2, axis=1))
    lo2 = jnp.where(upper, pltpu.roll(lo1, 2, axis=1), pltpu.roll(lo1, LANES - 2, axis=1))
    gscore = jnp.maximum(hi1, hi2) + jnp.maximum(jnp.minimum(hi1, hi2), jnp.maximum(lo1, lo2))
    gscore = jnp.where(lane < n_exp, gscore, neg)
    gmax = jnp.max(gscore, axis=1, keepdims=True)
    lane_f = lane.astype(F32)
    grp = (lane // epg).astype(F32)
    big = float(LANES)
    best = jnp.min(jnp.where(gscore == gmax, grp, big), axis=1, keepdims=True)
    masked = jnp.where(grp == best, sel, neg)
    m1 = jnp.max(masked, axis=1, keepdims=True)
    i1 = jnp.min(jnp.where(masked == m1, lane_f, big), axis=1, keepdims=True)
    masked2 = jnp.where(lane_f == i1, neg, masked)
    m2 = jnp.max(masked2, axis=1, keepdims=True)
    i2 = jnp.min(jnp.where(masked2 == m2, lane_f, big), axis=1, keepdims=True)
    pick1 = lane_f == i1
    pick2 = lane_f == i2
    s1 = jnp.sum(jnp.where(pick1, scores, 0.0), axis=1, keepdims=True)
    s2 = jnp.sum(jnp.where(pick2, scores, 0.0), axis=1, keepdims=True)
    return pick1, pick2, s1 / (s1 + s2), s2 / (s1 + s2)


def _lane_to_row(col_vals):
    ones = jnp.ones((8, LANES), BF16)
    hi, lo = _split2(col_vals)
    return (_dot_nt(ones, hi) + _dot_nt(ones, lo))[0:1]


def _moe_kernel(x_ref, g_ref, sc_ref, sh_ref, g2_ref, rw_ref, rb_ref,
                wg_ref, wu_ref, wd_ref, swg_ref, swu_ref, swd_ref, o_ref,
                m_ref, xs_ref, ys_ref, pm_ref, ws_ref, oc_ref, *, n_exp):
    e = pl.program_id(1)
    tm = x_ref.shape[0]
    S = 2 * tm
    RB = MOE_ROWS

    @pl.when(e == 0)
    def _():
        x = x_ref[...]
        m = x * lax.rsqrt(jnp.mean(x * x, axis=-1, keepdims=True) + NORM_EPS) * g_ref[...]
        m = m * (1.0 + sc_ref[...]) + sh_ref[...]
        mb = m.astype(BF16)
        m_ref[...] = mb
        scores = _sigmoid(_dot_split(m, rw_ref[...]))
        pick1, pick2, w1, w2 = _route(scores, scores + rb_ref[...], n_exp)
        p1 = pick1.astype(F32)
        p2 = pick2.astype(F32)
        chosen = p1 + p2
        cnt = jnp.sum(chosen, axis=0, keepdims=True)
        li = lax.broadcasted_iota(jnp.int32, (LANES, LANES), 0)
        lj = lax.broadcasted_iota(jnp.int32, (LANES, LANES), 1)
        off = _dot_exact_rhs(jnp.broadcast_to(cnt, (8, LANES)), (li < lj).astype(F32).astype(BF16))[0:1]
        ti = lax.broadcasted_iota(jnp.int32, (tm, tm), 0)
        tj = lax.broadcasted_iota(jnp.int32, (tm, tm), 1)
        rank = jnp.dot((tj < ti).astype(F32).astype(BF16), chosen.astype(BF16),
                       preferred_element_type=F32)
        dest = off + rank
        d1 = _lane_to_row(p1 * dest)
        d2 = _lane_to_row(p2 * dest)
        w1r = _lane_to_row(jnp.broadcast_to(w1, (tm, LANES)) * p1)
        w2r = _lane_to_row(jnp.broadcast_to(w2, (tm, LANES)) * p2)
        row = lax.broadcasted_iota(jnp.int32, (S, tm), 0).astype(F32)
        is1 = row == d1
        is2 = row == d2
        perm = jnp.where(is1, 1.0, jnp.where(is2, 1.0, 0.0)).astype(BF16)
        pm_ref[...] = perm
        ws_ref[...] = jnp.sum(jnp.where(is1, w1r, jnp.where(is2, w2r, 0.0)), axis=1, keepdims=True)
        xs_ref[...] = jnp.dot(perm, mb, preferred_element_type=F32).astype(BF16)
        ys_ref[...] = jnp.zeros_like(ys_ref)
        oc_ref[0:1, :] = off
        oc_ref[1:2, :] = cnt

    @pl.when(e < n_exp)
    def _():
        lane = lax.broadcasted_iota(jnp.int32, (1, LANES), 1)
        o_e = jnp.sum(jnp.where(lane == e, oc_ref[0:1, :], 0.0)).astype(jnp.int32)
        c_e = jnp.sum(jnp.where(lane == e, oc_ref[1:2, :], 0.0)).astype(jnp.int32)
        wg = wg_ref[...].astype(BF16)
        wu = wu_ref[...].astype(BF16)
        wd = wd_ref[...].astype(BF16)
        b0 = o_e // RB
        b1 = jnp.where(c_e > 0, (o_e + c_e + RB - 1) // RB, b0)

        def body(b, carry):
            r0 = pl.multiple_of(b * RB, RB)
            xb = xs_ref[pl.ds(r0, RB), :]
            hg = jnp.dot(xb, wg, preferred_element_type=F32)
            hu = jnp.dot(xb, wu, preferred_element_type=F32)
            rows = r0 + lax.broadcasted_iota(jnp.int32, (RB, 1), 0)
            mine = (rows >= o_e) & (rows < o_e + c_e)
            wrow = jnp.where(mine, ws_ref[pl.ds(r0, RB), :], 0.0)
            hh = hg * _sigmoid(hg) * hu * wrow
            ys_ref[pl.ds(r0, RB), :] += jnp.dot(hh.astype(BF16), wd, preferred_element_type=F32)
            return carry

        lax.fori_loop(b0, b1, body, 0)

    @pl.when(e == n_exp)
    def _():
        m = m_ref[...]
        hg = jnp.dot(m, swg_ref[...], preferred_element_type=F32)
        hu = jnp.dot(m, swu_ref[...], preferred_element_type=F32)
        hh = hg * _sigmoid(hg) * hu
        y = jnp.dot(hh.astype(BF16), swd_ref[...], preferred_element_type=F32)
        y = y + _dot_tn(pm_ref[...], ys_ref[...])
        o_ref[...] = x_ref[...] + g2_ref[...] * y


def _moe(x, g, sc, sh, g2, rw_pad, rb_pad, layer, wg, wu, wd, swg, swu, swd):
    N, D = x.shape
    _, E, _, Ff = wg.shape
    Bm = sc.shape[0]
    rpm = N // Bm
    tm = min(rpm, 1024)
    assert rpm % tm == 0 and (2 * tm) % MOE_ROWS == 0
    one = pl.Buffered(1)
    mod = pl.BlockSpec((None, 1, D), lambda i, e: (i * tm // rpm, 0, 0))
    full = lambda shape: pl.BlockSpec(shape, lambda i, e: (0,) * len(shape), pipeline_mode=one)
    return pl.pallas_call(
        functools.partial(_moe_kernel, n_exp=E),
        out_shape=jax.ShapeDtypeStruct((N, D), F32),
        grid=(N // tm, E + 1),
        in_specs=[pl.BlockSpec((tm, D), lambda i, e: (i, 0), pipeline_mode=one),
                  full((1, D)), mod, mod, mod,
                  full(rw_pad.shape), full(rb_pad.shape),
                  pl.BlockSpec((None, None, D, Ff), lambda i, e: (layer, jnp.minimum(e, E - 1), 0, 0)),
                  pl.BlockSpec((None, None, D, Ff), lambda i, e: (layer, jnp.minimum(e, E - 1), 0, 0)),
                  pl.BlockSpec((None, None, Ff, D), lambda i, e: (layer, jnp.minimum(e, E - 1), 0, 0)),
                  full(swg.shape), full(swu.shape), full(swd.shape)],
        out_specs=pl.BlockSpec((tm, D), lambda i, e: (i, 0), pipeline_mode=one),
        scratch_shapes=[pltpu.VMEM((tm, D), BF16), pltpu.VMEM((2 * tm, D), BF16),
                        pltpu.VMEM((2 * tm, D), F32), pltpu.VMEM((2 * tm, tm), BF16),
                        pltpu.VMEM((2 * tm, 1), F32), pltpu.VMEM((8, LANES), F32)],
        compiler_params=pltpu.CompilerParams(dimension_semantics=("arbitrary", "arbitrary"),
                                             vmem_limit_bytes=MOE_VMEM_LIMIT_BYTES),
        name="moe",
    )(x, g.reshape(1, D), sc, sh, g2, rw_pad, rb_pad, wg, wu, wd, swg, swu, swd)


def _rope_partner(w):
    lead = w.shape[:-1]
    wr = w.reshape(lead + (2, 2, ROPE_PAIRS))
    return jnp.stack([-wr[..., 1, :], wr[..., 0, :]], axis=-2).reshape(w.shape)


def _rope_tables(T):
    rows = T // GRID_W
    row = jnp.repeat(jnp.arange(rows), GRID_W)
    col = jnp.tile(jnp.arange(GRID_W), rows)
    inv = ROPE_BASE ** (-jnp.arange(ROPE_PAIRS, dtype=F32) / ROPE_PAIRS)
    ang = jnp.stack([row, col], axis=-1).astype(F32)[..., None] * inv
    cos = jnp.broadcast_to(jnp.cos(ang)[:, :, None, :], (T, 2, 2, ROPE_PAIRS)).reshape(T, MLA_ROPE)
    sin = jnp.broadcast_to(jnp.sin(ang)[:, :, None, :], (T, 2, 2, ROPE_PAIRS)).reshape(T, MLA_ROPE)
    pad = LANES - MLA_QK
    z_n, z_p = jnp.zeros((T, MLA_NOPE), F32), jnp.zeros((T, pad), F32)
    cq = jnp.concatenate([jnp.ones((T, MLA_NOPE), F32), cos, z_p], axis=1)
    sq = jnp.concatenate([z_n, sin, z_p], axis=1)
    ck = jnp.concatenate([z_n, cos, z_p], axis=1)
    return cq, sq, ck, sq


def _identity_tables(T):
    pad = LANES - MLA_QK
    z_n, z_p = jnp.zeros((T, MLA_NOPE), F32), jnp.zeros((T, pad), F32)
    one_r = jnp.ones((T, MLA_ROPE), F32)
    cq = jnp.concatenate([jnp.ones((T, MLA_NOPE), F32), one_r, z_p], axis=1)
    ck = jnp.concatenate([z_n, one_r, z_p], axis=1)
    zero = jnp.zeros((T, LANES), F32)
    return cq, zero, ck, zero


def kernel(x, c, ctx, c_ctx, w_ada, b_ada, norm1_g, norm2_g, w_in, q_norm_g, kv_norm_g, w_uq, w_ukv, rwkv_mu, rwkv_w0, rwkv_w_up, rwkv_a0, rwkv_a_up, rwkv_g_up, rwkv_k_k, rwkv_k_a, rwkv_r_k, rwkv_gn_g, rwkv_gn_b, w_proj_a, w_proj_b, w_proj_c, w_out, router_w, router_b, exp_w_gate, exp_w_up, exp_w_down, sh_w_gate, sh_w_up, sh_w_down, final_norm_g):
    B, T, D = x.shape
    Tc = ctx.shape[1]
    L = w_ada.shape[0]
    H = MLA_HEADS
    QL, KVL = q_norm_g.shape[1], kv_norm_g.shape[1]
    Fd = w_proj_a.shape[1]
    C = RWKV_DIM
    E = router_w.shape[1]
    N, Nc = B * T, B * Tc

    sizes = (Fd, QL, KVL, MLA_ROPE, 3 * C + DECAY_LORA + AAA_LORA + GATE_LORA, D, D, D)
    offs = np.concatenate([[0], np.cumsum(sizes)])

    lat_tabs = _rope_tables(T)
    ctx_tabs = _identity_tables(Tc)
    cc, cs = _dft_mats(FOURIER_GROUP_DIM)
    ch_mats = jnp.stack([cc, -cs]).astype(BF16)
    hid = jnp.arange(C) // RWKV_HEAD
    bd = (hid[:, None] == hid[None, :]).astype(BF16)
    rw_pad = jnp.zeros((D, LANES), F32).at[:, :E].set(router_w)
    rb_pad = jnp.zeros((1, LANES), F32).at[0, :E].set(router_b)

    R = -(-(B + 1) // 8) * 8
    c_rows = jnp.zeros((R, D), F32).at[:B].set(c).at[B].set(c_ctx)
    mod = _ada_mod(c_rows, w_ada, b_ada)

    x = x.reshape(N, D)
    ctx = ctx.reshape(Nc, D)
    zero_state = jnp.zeros((2, B, RWKV_HEADS, RWKV_HEAD, RWKV_HEAD), F32)

    for l in range(L):
        last = l == L - 1
        ml = mod[l, :B].reshape(B, 1, 6, D)
        mc = mod[l, B:B + 1].reshape(1, 1, 6, D)
        sh1, sc1, g1, sh2, sc2, g2 = [ml[:, :, j] for j in range(6)]
        csh1, csc1, cg1, csh2, csc2, cg2 = [mc[:, :, j] for j in range(6)]

        wi = w_in[l]
        w_kr = wi[:, offs[3]:offs[4]]
        zl = jnp.zeros((D, MLA_NOPE), F32)
        zr = jnp.zeros((D, LANES - MLA_QK), F32)
        segs = [wi[:, offs[5]:offs[8]],
                wi[:, offs[0]:offs[1]],
                wi[:, offs[4]:offs[5]],
                jnp.concatenate([wi[:, offs[1]:offs[3]], zl, w_kr, zr, zl, _rope_partner(w_kr), zr], axis=1)]
        seg_off, n_all = _column_layout([s.shape[1] for s in segs])
        pieces, cur = [], 0
        for s, o in zip(segs, seg_off):
            if o > cur:
                pieces.append(jnp.zeros((D, o - cur), F32))
            pieces.append(s)
            cur = o + s.shape[1]
        if n_all > cur:
            pieces.append(jnp.zeros((D, n_all - cur), F32))
        w_all = jnp.concatenate(pieces, axis=1).astype(BF16)
        gt_col, pf_col, rw_col, qkv_col = [o // s.shape[1] for s, o in zip(segs, seg_off)]
        uq = w_uq[l].reshape(QL, H, MLA_QK).transpose(1, 0, 2)
        padq = jnp.zeros((H, QL, LANES - MLA_QK), F32)
        wqa = jnp.concatenate([uq, padq], axis=2).astype(BF16)
        wqb = jnp.concatenate([jnp.zeros((H, QL, MLA_NOPE), F32), _rope_partner(uq[..., MLA_NOPE:]), padq],
                              axis=2).astype(BF16)
        ukv = w_ukv[l].reshape(KVL, H, MLA_NOPE + MLA_V).transpose(1, 0, 2)
        wk = jnp.concatenate([ukv[..., :MLA_NOPE], jnp.zeros((H, KVL, LANES - MLA_NOPE), F32)], axis=2).astype(BF16)
        zv = jnp.zeros((H, KVL, LANES - MLA_V), F32)
        even = (jnp.arange(H) % 2 == 0)[:, None, None]
        wv = jnp.where(even, jnp.concatenate([ukv[..., MLA_NOPE:], zv], axis=2),
                       jnp.concatenate([zv, ukv[..., MLA_NOPE:]], axis=2)).astype(BF16)
        wb = w_proj_b[l].reshape(H // 2, 2 * MLA_V, D).astype(BF16)
        wa = w_proj_a[l].astype(BF16)
        wc = w_proj_c[l].astype(BF16)
        wo = w_out[l].astype(BF16)
        zd = jnp.zeros((2, DECAY_LORA, C), F32)
        prm = dict(
            mu=rwkv_mu[l].reshape(1, -1), w0=rwkv_w0[l], a0=rwkv_a0[l],
            wup=jnp.concatenate([rwkv_w_up[l], zd], axis=1).astype(BF16),
            aup=jnp.concatenate([zd, rwkv_a_up[l]], axis=1).astype(BF16),
            gup=rwkv_g_up[l].astype(BF16), k_k=rwkv_k_k[l].reshape(1, C), k_a=rwkv_k_a[l].reshape(1, C),
            r_k=rwkv_r_k[l].reshape(1, C), bd=bd)

        def mixers(xx, sc, sh, tabs, Tx):
            proj = _in_proj(xx, norm1_g[l], sc, sh, w_all)
            q, k, v = _qkv_project(proj, qkv_col, q_norm_g[l], kv_norm_g[l], wqa, wqb, wk, wv, tabs, B, Tx)
            rp = _rwkv_prep(proj, rw_col, prm, B, Tx)
            return proj, q, k, v, rp

        proj_l, q_l, k_l, v_l, rp_l = mixers(x, sc1, sh1, lat_tabs, T)
        proj_c, q_c, k_c, v_c, rp_c = mixers(ctx, csc1, csh1, ctx_tabs, Tc)

        attn_l = _attention(q_l, [(k_l, v_l), (k_c, v_c)])

        yf_c, yr_c, s_ctx = _rwkv_scan(*rp_c[:6], prm["k_a"], zero_state, B, Tc)
        yf_l, yr_l, _ = _rwkv_scan(*rp_l[:6], prm["k_a"], s_ctx, B, T)
        rwo_l = _rwkv_post(yf_l, yr_l, rp_l[7], rp_l[6], rwkv_gn_g[l], rwkv_gn_b[l], bd)

        fa_l = _fourier_time(_fourier_channel(proj_l, pf_col, Fd, ch_mats, B, T), B, T, Fd)

        x = _merge(fa_l, attn_l, rwo_l, proj_l, gt_col, x, g1, wa, wb, wc, wo, B, T)
        moe_w = (rw_pad, rb_pad, l, exp_w_gate, exp_w_up, exp_w_down,
                 sh_w_gate[l].astype(BF16), sh_w_up[l].astype(BF16), sh_w_down[l].astype(BF16))
        x = _moe(x, norm2_g[l], sc2, sh2, g2, *moe_w)

        if not last:
            attn_c = _attention(q_c, [(k_c, v_c)])
            rwo_c = _rwkv_post(yf_c, yr_c, rp_c[7], rp_c[6], rwkv_gn_g[l], rwkv_gn_b[l], bd)
            fa_c = _fourier_time(_fourier_channel(proj_c, pf_col, Fd, ch_mats, B, Tc), B, Tc, Fd)
            ctx = _merge(fa_c, attn_c, rwo_c, proj_c, gt_col, ctx, cg1, wa, wb, wc, wo, B, Tc)
            ctx = _moe(ctx, norm2_g[l], csc2, csh2, cg2, *moe_w)

    ones = jnp.ones((1, 1, D), F32)
    out = _norm(x, final_norm_g, ones, ones, F32, modulate=False)
    return out.reshape(B, T, D)
```
